```python
import jax, jax.numpy as jnp
from jax import lax
import numpy as np

D_MODEL = 1024
BATCH = 4
SEQ = 4096
DEPTH = 1

CHUNK = 64
RMS_EPS = 1e-6
ROPE_BASE = 10000.0
RET_HEADS = 4
RET_V = D_MODEL // 2
RET_QK = RET_V // 2
GLA_HEADS = 4
GLA_V = D_MODEL - RET_V
GLA_QK = GLA_V // 2
GLA_GATE_RANK = 16
GLA_GATE_NORM = 16.0
MIX_WIDTH = RET_V + GLA_V
IN_SIZES = (RET_QK, RET_QK, RET_V, RET_V, GLA_QK, GLA_QK, GLA_V, GLA_V, GLA_GATE_RANK)
IN_WIDTH = sum(IN_SIZES)
D_FF = ((8 * D_MODEL // 3 + 127) // 128) * 128

kernel_name = "macaron_retention_gla_hybrid"


def rms_norm(x, g):
    xf = x.astype(jnp.float32)
    y = xf * lax.rsqrt(jnp.mean(xf * xf, axis=-1, keepdims=True) + RMS_EPS)
    return (y * g.astype(jnp.float32)).astype(x.dtype)


def swiglu(h, w_gate, w_up, w_down):
    return (jax.nn.silu(h @ w_gate) * (h @ w_up)) @ w_down


def split_heads(t, n_heads):
    b, t_len, w = t.shape
    return t.reshape(b, t_len, n_heads, w // n_heads).transpose(0, 2, 1, 3)


def merge_heads(t):
    b, h, t_len, d = t.shape
    return t.transpose(0, 2, 1, 3).reshape(b, t_len, h * d)


def rotary(t, pos):
    dk = t.shape[-1]
    half = dk // 2
    inv = ROPE_BASE ** (-jnp.arange(half, dtype=jnp.float32) * 2.0 / dk)
    ang = pos[:, None] * inv[None, :]
    cos, sin = jnp.cos(ang), jnp.sin(ang)
    t1, t2 = t[..., :half], t[..., half:]
    return jnp.concatenate([t1 * cos - t2 * sin, t1 * sin + t2 * cos], axis=-1)


def head_rms(o):
    return o * lax.rsqrt(jnp.mean(o * o, axis=-1, keepdims=True) + RMS_EPS)


def chunk_decay_attention(q, k, v, log_a):
    bsz, n_h, t_len, dk = q.shape
    n_chunks = t_len // CHUNK
    per_key = log_a.shape[-1] != 1
    qf, kf, vf = (z.astype(jnp.float32) for z in (q, k, v))
    la = log_a.astype(jnp.float32)

    def to_chunks(z):
        return z.reshape(z.shape[0], z.shape[1], n_chunks, CHUNK, z.shape[-1]).transpose(2, 0, 1, 3, 4)

    b_cum = jnp.cumsum(to_chunks(la), axis=3)
    xs = (to_chunks(qf), to_chunks(kf), to_chunks(vf), b_cum)

    def step(state, inp):
        q_c, k_c, v_c, b_c = inp
        decay = jnp.exp(-jnp.abs(b_c[..., :, None, :] - b_c[..., None, :, :]))
        if per_key:
            scores = jnp.einsum('bhik,bhjk,bhijk->bhij', q_c, k_c, decay)
        else:
            scores = jnp.einsum('bhik,bhjk->bhij', q_c, k_c) * decay[..., 0]
        out = (jnp.einsum('bhij,bhjv->bhiv', scores, v_c)
               + jnp.einsum('bhik,bhkv->bhiv', q_c * jnp.exp(b_c), state))
        b_last = b_c[..., -1:, :]
        state = (jnp.exp(b_last[..., 0, :])[..., None] * state
                 + jnp.einsum('bhjk,bhjv->bhkv', k_c * jnp.exp(b_last - b_c), v_c))
        return state, out

    s0 = jnp.zeros((bsz, n_h, dk, v.shape[-1]), jnp.float32)
    _, outs = lax.scan(step, s0, xs)
    return outs.transpose(1, 2, 0, 3, 4).reshape(bsz, n_h, t_len, v.shape[-1])


def hybrid_mixer(h, w_in, ret_norm_g, gla_w_a2, gla_b_a, gla_norm_g, w_out):
    t_len = h.shape[1]
    pos = jnp.arange(t_len, dtype=jnp.float32)
    proj = h @ w_in
    offsets = [int(o) for o in np.cumsum(IN_SIZES)[:-1]]
    (r_q, r_k, r_v, r_g, g_q, g_k, g_v, g_g, g_low) = jnp.split(proj, offsets, axis=-1)

    rdk = RET_QK // RET_HEADS
    rq = rotary(split_heads(r_q, RET_HEADS), pos)
    rk = rotary(split_heads(r_k, RET_HEADS), pos) * (rdk ** -0.5)
    rv = split_heads(r_v, RET_HEADS)
    log_gamma = jnp.log(1.0 - 2.0 ** (-5.0 - jnp.arange(RET_HEADS, dtype=jnp.float32)))
    ret_log_a = jnp.broadcast_to(log_gamma[None, :, None, None], (1, RET_HEADS, t_len, 1))
    o_ret = head_rms(chunk_decay_attention(rq, rk, rv, ret_log_a))
    o_ret = (merge_heads(o_ret) * ret_norm_g.astype(jnp.float32)).astype(h.dtype) * jax.nn.silu(r_g)

    gdk = GLA_QK // GLA_HEADS
    gq = split_heads(g_q, GLA_HEADS) * (gdk ** -0.5)
    gk = split_heads(g_k, GLA_HEADS)
    gv = split_heads(g_v, GLA_HEADS)
    gate_logit = (g_low @ gla_w_a2 + gla_b_a).astype(jnp.float32)
    gla_log_a = split_heads(jax.nn.log_sigmoid(gate_logit) / GLA_GATE_NORM, GLA_HEADS)
    o_gla = head_rms(chunk_decay_attention(gq, gk, gv, gla_log_a))
    o_gla = (merge_heads(o_gla) * gla_norm_g.astype(jnp.float32)).astype(h.dtype) * jax.nn.silu(g_g)

    return jnp.concatenate([o_ret, o_gla], axis=-1) @ w_out


def setup_inputs(seed: int = 0) -> dict:
    key = jax.random.key(seed)
    ks = jax.random.split(key, 20)
    f32 = jnp.float32

    def nrm(k, shape, fan_in):
        return jax.random.normal(k, shape, f32) * (fan_in ** -0.5)

    def gain(k, shape):
        return 1.0 + 0.02 * jax.random.normal(k, shape, f32)

    return {
        "x": jax.random.normal(ks[0], (BATCH, SEQ, D_MODEL), f32),
        "ffn1_norm_g": gain(ks[1], (DEPTH, D_MODEL)),
        "ffn1_w_gate": nrm(ks[2], (DEPTH, D_MODEL, D_FF), D_MODEL),
        "ffn1_w_up": nrm(ks[3], (DEPTH, D_MODEL, D_FF), D_MODEL),
        "ffn1_w_down": nrm(ks[4], (DEPTH, D_FF, D_MODEL), D_FF),
        "mix_norm_g": gain(ks[5], (DEPTH, D_MODEL)),
        "w_in": nrm(ks[6], (DEPTH, D_MODEL, IN_WIDTH), D_MODEL),
        "ret_norm_g": gain(ks[7], (DEPTH, RET_V)),
        "gla_w_a2": nrm(ks[8], (DEPTH, GLA_GATE_RANK, GLA_QK), GLA_GATE_RANK),
        "gla_b_a": 0.1 * jax.random.normal(ks[9], (DEPTH, GLA_QK), f32),
        "gla_norm_g": gain(ks[10], (DEPTH, GLA_V)),
        "w_out": nrm(ks[11], (DEPTH, MIX_WIDTH, D_MODEL), MIX_WIDTH),
        "ffn2_norm_g": gain(ks[12], (DEPTH, D_MODEL)),
        "ffn2_w_gate": nrm(ks[13], (DEPTH, D_MODEL, D_FF), D_MODEL),
        "ffn2_w_up": nrm(ks[14], (DEPTH, D_MODEL, D_FF), D_MODEL),
        "ffn2_w_down": nrm(ks[15], (DEPTH, D_FF, D_MODEL), D_FF),
        "final_norm_g": gain(ks[16], (D_MODEL,)),
    }


def reference(x, ffn1_norm_g, ffn1_w_gate, ffn1_w_up, ffn1_w_down, mix_norm_g, w_in,
              ret_norm_g, gla_w_a2, gla_b_a, gla_norm_g, w_out, ffn2_norm_g,
              ffn2_w_gate, ffn2_w_up, ffn2_w_down, final_norm_g):
    for l in range(DEPTH):
        x = x + 0.5 * swiglu(rms_norm(x, ffn1_norm_g[l]), ffn1_w_gate[l], ffn1_w_up[l], ffn1_w_down[l])
        x = x + hybrid_mixer(rms_norm(x, mix_norm_g[l]), w_in[l], ret_norm_g[l], gla_w_a2[l],
                             gla_b_a[l], gla_norm_g[l], w_out[l])
        x = x + 0.5 * swiglu(rms_norm(x, ffn2_norm_g[l]), ffn2_w_gate[l], ffn2_w_up[l], ffn2_w_down[l])
    return rms_norm(x, final_norm_g)
```

```python
import functools

import numpy as np
import jax
import jax.numpy as jnp
from jax import lax
from jax.experimental import pallas as pl
from jax.experimental.pallas import tpu as pltpu

D_MODEL = 1024
CHUNK = 64
RMS_EPS = 1e-6
ROPE_BASE = 10000.0
N_HEADS = 4
QK_W = 256
V_W = 512
HEAD_V = V_W // N_HEADS
HEAD_QK = QK_W // N_HEADS
GATE_RANK = 16
GATE_NORM = 16.0
D_FF = 2816
PROJ_W = 2 * (2 * QK_W + 2 * V_W)
RQ, RK, RV, RG = 0, 256, 512, 1024
GQ, GK, GV, GG = 1536, 1792, 2048, 2560
GLOW = 3072

LANE = 128
MXU_N = 256
VMEM_LIMIT = 56 * 1024 * 1024

GLA_LEVELS = (64, 32, 16, 8, 4, 2)
N_EXP_BLOCKS = 2 + len(GLA_LEVELS)

BF16 = jnp.bfloat16
F32 = jnp.float32


def _dot(a, b):
    return jnp.dot(a, b, preferred_element_type=F32)


def _dot_nt(a, b):
    return lax.dot_general(a, b, (((1,), (1,)), ((), ())), preferred_element_type=F32)


def _dot_tn(a, b):
    return lax.dot_general(a, b, (((0,), (0,)), ((), ())), preferred_element_type=F32)


def _rms(x, g):
    ms = jnp.mean(x * x, axis=-1, keepdims=True)
    return x * lax.rsqrt(ms + RMS_EPS) * g


def _silu(x):
    return x * (1.0 / (1.0 + jnp.exp(-x)))


def _ffn(h, wg_ref, wu_ref, wd_ref, act_ref):
    for c in range(D_FF // MXU_N):
        cols = slice(c * MXU_N, (c + 1) * MXU_N)
        gate = _dot(h, wg_ref[:, cols])
        up = _dot(h, wu_ref[:, cols])
        act_ref[:, cols] = (_silu(gate) * up).astype(BF16)
    return _dot(act_ref[...], wd_ref[...])


def _ffn1_proj_kernel(x_ref, cos_ref, sin_ref, g1_ref, wg_ref, wu_ref, wd_ref, gm_ref, win_ref,
                      wlow_ref, wa2_ref, ba_ref, x1_ref, proj_ref, la_ref, act_ref):
    x = x_ref[...]
    x1 = x + 0.5 * _ffn(_rms(x, g1_ref[...]).astype(BF16), wg_ref, wu_ref, wd_ref, act_ref)
    x1_ref[...] = x1
    h = _rms(x1, gm_ref[...]).astype(BF16)

    cos = cos_ref[...]
    sin = sin_ref[...]
    for off, scale in ((RQ, 1.0), (RK, HEAD_QK ** -0.5)):
        t = _dot(h, win_ref[:, off:off + QK_W])
        t1, t2 = t[:, :LANE], t[:, LANE:]
        proj_ref[:, off:off + LANE] = ((t1 * cos - t2 * sin) * scale).astype(BF16)
        proj_ref[:, off + LANE:off + QK_W] = ((t1 * sin + t2 * cos) * scale).astype(BF16)
    proj_ref[:, RV:GQ] = _dot(h, win_ref[:, RV:GQ]).astype(BF16)
    proj_ref[:, GQ:GK] = (_dot(h, win_ref[:, GQ:GK]) * (HEAD_QK ** -0.5)).astype(BF16)
    proj_ref[:, GK:PROJ_W] = _dot(h, win_ref[:, GK:PROJ_W]).astype(BF16)

    low = _dot(h, wlow_ref[...]).astype(BF16)
    z = _dot(low, wa2_ref[...]) + ba_ref[...]
    la_ref[...] = (jnp.minimum(z, 0.0) - jnp.log1p(jnp.exp(-jnp.abs(z)))) * (1.0 / GATE_NORM)


def _head_norm_gate(o, norm_g, gate):
    outs = []
    for h in range(N_HEADS):
        cols = slice(h * HEAD_V, (h + 1) * HEAD_V)
        oh = o[:, cols]
        ms = jnp.mean(oh * oh, axis=-1, keepdims=True)
        outs.append(oh * lax.rsqrt(ms + RMS_EPS) * norm_g[:, cols] * _silu(gate[:, cols]))
    return jnp.concatenate(outs, axis=-1)


def _mixer_ffn2_kernel(x1_ref, proj_ref, la_ref,
                       kmask_r_ref, kmask_g_ref, vmask_ref, smask_r_ref, smask_g_ref,
                       dret_ref, gq_ref, gk_ref, aret_ref, tall_ref, lmask_ref,
                       rn_ref, gn_ref, wout_ref, g2_ref, wg_ref, wu_ref, wd_ref, gf_ref,
                       out_ref, o_ref, act_ref, sr_ref, sg_ref):
    rows_per_step = x1_ref.shape[0]

    @pl.when(pl.program_id(1) == 0)
    def _():
        sr_ref[...] = jnp.zeros_like(sr_ref)
        sg_ref[...] = jnp.zeros_like(sg_ref)

    def block_diag(t, mask_ref):
        return jnp.concatenate([t] * N_HEADS, axis=0) * mask_ref[...]

    def chunk_body(c, carry):
        rows = pl.ds(pl.multiple_of(c * CHUNK, CHUNK), CHUNK)

        q = proj_ref[rows, RQ:RQ + QK_W]
        k = proj_ref[rows, RK:RK + QK_W]
        v = proj_ref[rows, RV:RV + V_W]
        s = _dot_nt(q, block_diag(k, kmask_r_ref)) * dret_ref[...]
        v_bd = block_diag(v, vmask_ref)
        q_in = (q.astype(F32) * gq_ref[...]).astype(BF16)
        o = _dot(s.astype(BF16), v_bd) + _dot(q_in, sr_ref[...].astype(BF16))
        k_out = (k.astype(F32) * gk_ref[...]).astype(BF16)
        sr_ref[...] = sr_ref[...] * aret_ref[...] + _dot_tn(k_out, v) * smask_r_ref[...]
        gate = proj_ref[rows, RG:RG + V_W].astype(F32)
        o_ref[rows, 0:V_W] = _head_norm_gate(o, rn_ref[...], gate).astype(BF16)

        la = la_ref[rows, :]
        la_hi = la.astype(BF16)
        la_lo = (la - la_hi.astype(F32)).astype(BF16)
        la2 = jnp.concatenate([la_hi, la_lo], axis=0)
        e = jnp.exp(_dot(tall_ref[...], la2))
        a_col = jnp.exp(_dot_tn(la2, jnp.ones((2 * CHUNK, LANE), BF16)))
        q = proj_ref[rows, GQ:GQ + QK_W].astype(F32)
        k = proj_ref[rows, GK:GK + QK_W].astype(F32)
        v = proj_ref[rows, GV:GV + V_W]
        s = _dot_nt(q.astype(BF16), block_diag(k.astype(BF16), kmask_g_ref)) * lmask_ref[0]
        for l in range(len(GLA_LEVELS)):
            f = e[(2 + l) * CHUNK:(3 + l) * CHUNK, :]
            ql = (q * f).astype(BF16)
            kl = (k * f).astype(BF16)
            s = s + _dot_nt(ql, block_diag(kl, kmask_g_ref)) * lmask_ref[1 + l]
        v_bd = block_diag(v, vmask_ref)
        q_in = (q * e[0:CHUNK, :]).astype(BF16)
        o = _dot(s.astype(BF16), v_bd) + _dot(q_in, sg_ref[...].astype(BF16))
        k_out = (k * e[CHUNK:2 * CHUNK, :]).astype(BF16)
        a_full = jnp.concatenate([a_col] * (V_W // LANE), axis=1)
        sg_ref[...] = sg_ref[...] * a_full + _dot_tn(k_out, v) * smask_g_ref[...]
        gate = proj_ref[rows, GG:GG + V_W].astype(F32)
        o_ref[rows, V_W:2 * V_W] = _head_norm_gate(o, gn_ref[...], gate).astype(BF16)
        return carry

    lax.fori_loop(0, rows_per_step // CHUNK, chunk_body, 0)

    x2 = x1_ref[...] + _dot(o_ref[...], wout_ref[...])
    x3 = x2 + 0.5 * _ffn(_rms(x2, g2_ref[...]).astype(BF16), wg_ref, wu_ref, wd_ref, act_ref)
    out_ref[...] = _rms(x3, gf_ref[...])


def _gla_tables():
    t = np.arange(CHUNK)[:, None]
    u = np.arange(CHUNK)[None, :]
    blocks = [(u <= t), (u > t)]
    masks = [np.eye(CHUNK, dtype=bool)]
    for s in GLA_LEVELS:
        mid = (t // s) * s + s // 2 - 1
        blocks.append(np.where(t > mid, (u > mid) & (u <= t), (u > t) & (u <= mid)))
        same_block = (t // s) == (u // s)
        masks.append(same_block & (((t % s) < s // 2) != ((u % s) < s // 2)))
    tall = np.concatenate(blocks, axis=0).astype(np.float32)
    tall2 = np.concatenate([tall, tall], axis=1)
    lmask = np.stack([np.tile(m, (1, N_HEADS)) for m in masks]).astype(np.float32)
    return tall2, lmask


def _retention_tables():
    gamma = 1.0 - 2.0 ** (-5.0 - np.arange(N_HEADS, dtype=np.float64))
    lane = np.arange(QK_W)
    head_of_lane = (lane % LANE) // (HEAD_QK // 2)
    i = np.arange(CHUNK)
    dist = np.abs(i[:, None] - i[None, :])
    dret = np.concatenate([gamma[h] ** dist for h in range(N_HEADS)], axis=1)
    gq = gamma[head_of_lane][None, :] ** (i[:, None] + 1.0)
    gk = gamma[head_of_lane][None, :] ** (CHUNK - 1.0 - i[:, None])
    aret = np.broadcast_to((gamma[head_of_lane] ** CHUNK)[:, None], (QK_W, V_W))
    return tuple(np.asarray(a, np.float32) for a in (dret, gq, gk, aret)), head_of_lane


def _block_masks(head_of_lane_r):
    row_head = np.repeat(np.arange(N_HEADS), CHUNK)
    lane = np.arange(QK_W)
    kmask_r = row_head[:, None] == head_of_lane_r[None, :]
    kmask_g = row_head[:, None] == (lane // HEAD_QK)[None, :]
    vmask = row_head[:, None] == (np.arange(V_W) // HEAD_V)[None, :]
    smask_g = (lane // HEAD_QK)[:, None] == (np.arange(V_W) // HEAD_V)[None, :]
    smask_r = head_of_lane_r[:, None] == (np.arange(V_W) // HEAD_V)[None, :]
    return kmask_r, kmask_g, vmask, smask_r, smask_g


def _const_spec(shape):
    nd = len(shape)
    return pl.BlockSpec(shape, lambda *_: (0,) * nd, pipeline_mode=pl.Buffered(1))


def _tile_rows(total_rows):
    tile = 512
    assert total_rows % tile == 0
    return tile


@jax.jit
def kernel(x, ffn1_norm_g, ffn1_w_gate, ffn1_w_up, ffn1_w_down, mix_norm_g, w_in, ret_norm_g, gla_w_a2,
           gla_b_a, gla_norm_g, w_out, ffn2_norm_g, ffn2_w_gate, ffn2_w_up, ffn2_w_down, final_norm_g):
    bsz, seq, d = x.shape
    assert d == D_MODEL and seq % CHUNK == 0 and ffn1_norm_g.shape[0] == 1
    tile = _tile_rows(seq)
    steps = seq // tile
    n_rows = bsz * seq

    row = lambda g: g.reshape(1, -1).astype(F32)
    w_in0 = w_in[0]
    half = HEAD_QK // 2
    perm = np.array([h * HEAD_QK + hf * half + c
                     for hf in range(2) for h in range(N_HEADS) for c in range(half)])
    w_main = jnp.concatenate([w_in0[:, RQ:RK][:, perm], w_in0[:, RK:RV][:, perm], w_in0[:, RV:GLOW]],
                             axis=1).astype(BF16)
    w_low = jnp.pad(w_in0[:, GLOW:], ((0, 0), (0, LANE - GATE_RANK))).astype(BF16)
    w_a2 = jnp.pad(gla_w_a2[0], ((0, LANE - GATE_RANK), (0, 0))).astype(BF16)

    pos = jnp.arange(seq, dtype=F32)
    inv = ROPE_BASE ** (-jnp.arange(half, dtype=F32) * 2.0 / HEAD_QK)
    ang = pos[:, None] * inv[None, :]
    cos = jnp.tile(jnp.cos(ang), (1, N_HEADS))
    sin = jnp.tile(jnp.sin(ang), (1, N_HEADS))

    x2d = x.reshape(n_rows, d)
    row_spec = lambda w: pl.BlockSpec((tile, w), lambda i: (i, 0))
    pos_spec = pl.BlockSpec((tile, LANE), lambda i: (i % steps, 0))
    x1, proj, la = pl.pallas_call(
        _ffn1_proj_kernel,
        grid=(n_rows // tile,),
        in_specs=[row_spec(d), pos_spec, pos_spec, _const_spec((1, d)),
                  _const_spec((d, D_FF)), _const_spec((d, D_FF)), _const_spec((D_FF, d)),
                  _const_spec((1, d)), _const_spec((d, PROJ_W)), _const_spec((d, LANE)),
                  _const_spec((LANE, QK_W)), _const_spec((1, QK_W))],
        out_specs=[row_spec(d), row_spec(PROJ_W), row_spec(QK_W)],
        out_shape=[jax.ShapeDtypeStruct((n_rows, d), F32),
                   jax.ShapeDtypeStruct((n_rows, PROJ_W), BF16),
                   jax.ShapeDtypeStruct((n_rows, QK_W), F32)],
        scratch_shapes=[pltpu.VMEM((tile, D_FF), BF16)],
        compiler_params=pltpu.CompilerParams(dimension_semantics=("arbitrary",),
                                             vmem_limit_bytes=VMEM_LIMIT),
        name="ffn1_proj",
    )(x2d, cos, sin, row(ffn1_norm_g), ffn1_w_gate[0].astype(BF16), ffn1_w_up[0].astype(BF16),
      ffn1_w_down[0].astype(BF16), row(mix_norm_g), w_main, w_low, w_a2, row(gla_b_a))

    tall2, lmask = _gla_tables()
    (dret, gq, gk, aret), head_of_lane_r = _retention_tables()
    kmask_r, kmask_g, vmask, smask_r, smask_g = _block_masks(head_of_lane_r)
    consts = [jnp.asarray(kmask_r, BF16), jnp.asarray(kmask_g, BF16), jnp.asarray(vmask, BF16),
              jnp.asarray(smask_r, F32), jnp.asarray(smask_g, F32), jnp.asarray(dret), jnp.asarray(gq),
              jnp.asarray(gk), jnp.asarray(aret), jnp.asarray(tall2, BF16), jnp.asarray(lmask)]

    tok_spec = lambda w: pl.BlockSpec((tile, w), lambda b, t: (b * steps + t, 0))
    out = pl.pallas_call(
        _mixer_ffn2_kernel,
        grid=(bsz, steps),
        in_specs=[tok_spec(d), tok_spec(PROJ_W), tok_spec(QK_W)]
                 + [_const_spec(c.shape) for c in consts]
                 + [_const_spec((1, V_W)), _const_spec((1, V_W)), _const_spec((d, d)), _const_spec((1, d)),
                    _const_spec((d, D_FF)), _const_spec((d, D_FF)), _const_spec((D_FF, d)),
                    _const_spec((1, d))],
        out_specs=tok_spec(d),
        out_shape=jax.ShapeDtypeStruct((n_rows, d), F32),
        scratch_shapes=[pltpu.VMEM((tile, 2 * V_W), BF16), pltpu.VMEM((tile, D_FF), BF16),
                        pltpu.VMEM((QK_W, V_W), F32), pltpu.VMEM((QK_W, V_W), F32)],
        compiler_params=pltpu.CompilerParams(dimension_semantics=("arbitrary", "arbitrary"),
                                             vmem_limit_bytes=VMEM_LIMIT),
        name="mixer_ffn2",
    )(x1, proj, la, *consts, row(ret_norm_g), row(gla_norm_g), w_out[0].astype(BF16), row(ffn2_norm_g),
      ffn2_w_gate[0].astype(BF16), ffn2_w_up[0].astype(BF16), ffn2_w_down[0].astype(BF16),
      row(final_norm_g))
    return out.reshape(bsz, seq, d)
```

```python
import functools

import numpy as np
import jax
import jax.numpy as jnp
from jax import lax
from jax.experimental import pallas as pl
from jax.experimental.pallas import tpu as pltpu

D_MODEL = 1024
CHUNK = 64
RMS_EPS = 1e-6
ROPE_BASE = 10000.0
N_HEADS = 4
QK_W = 256
V_W = 512
HEAD_V = V_W // N_HEADS
HEAD_QK = QK_W // N_HEADS
GATE_RANK = 16
GATE_NORM = 16.0
D_FF = 2816
PROJ_W = 2 * (2 * QK_W + 2 * V_W)
RQ, RK, RV, RG = 0, 256, 512, 1024
GQ, GK, GV, GG = 1536, 1792, 2048, 2560
GLOW = 3072

LANE = 128
MXU_N = 256
VMEM_LIMIT = 56 * 1024 * 1024

GLA_LEVELS = (64, 32, 16, 8, 4, 2)
N_EXP_BLOCKS = 2 + len(GLA_LEVELS)

BF16 = jnp.bfloat16
F32 = jnp.float32


def _dot(a, b):
    return jnp.dot(a, b, preferred_element_type=F32)


def _dot_nt(a, b):
    return lax.dot_general(a, b, (((1,), (1,)), ((), ())), preferred_element_type=F32)


def _dot_tn(a, b):
    return lax.dot_general(a, b, (((0,), (0,)), ((), ())), preferred_element_type=F32)


def _rms(x, g):
    ms = jnp.mean(x * x, axis=-1, keepdims=True)
    return x * lax.rsqrt(ms + RMS_EPS) * g


def _silu(x):
    return x * (1.0 / (1.0 + jnp.exp(-x)))


def _ffn(h, wg_ref, wu_ref, wd_ref, act_ref):
    for c in range(D_FF // MXU_N):
        cols = slice(c * MXU_N, (c + 1) * MXU_N)
        gate = _dot(h, wg_ref[:, cols])
        up = _dot(h, wu_ref[:, cols])
        act_ref[:, cols] = (_silu(gate) * up).astype(BF16)
    return _dot(act_ref[...], wd_ref[...])


def _ffn1_proj_kernel(x_ref, cos_ref, sin_ref, g1_ref, wg_ref, wu_ref, wd_ref, gm_ref, win_ref,
                      wlow_ref, wa2_ref, ba_ref, x1_ref, proj_ref, la_ref, act_ref):
    x = x_ref[...]
    x1 = x + 0.5 * _ffn(_rms(x, g1_ref[...]).astype(BF16), wg_ref, wu_ref, wd_ref, act_ref)
    x1_ref[...] = x1
    h = _rms(x1, gm_ref[...]).astype(BF16)

    cos = cos_ref[...]
    sin = sin_ref[...]
    for off, scale in ((RQ, 1.0), (RK, HEAD_QK ** -0.5)):
        t = _dot(h, win_ref[:, off:off + QK_W])
        t1, t2 = t[:, :LANE], t[:, LANE:]
        proj_ref[:, off:off + LANE] = ((t1 * cos - t2 * sin) * scale).astype(BF16)
        proj_ref[:, off + LANE:off + QK_W] = ((t1 * sin + t2 * cos) * scale).astype(BF16)
    proj_ref[:, RV:GQ] = _dot(h, win_ref[:, RV:GQ]).astype(BF16)
    proj_ref[:, GQ:GK] = (_dot(h, win_ref[:, GQ:GK]) * (HEAD_QK ** -0.5)).astype(BF16)
    proj_ref[:, GK:PROJ_W] = _dot(h, win_ref[:, GK:PROJ_W]).astype(BF16)

    low = _dot(h, wlow_ref[...]).astype(BF16)
    z = _dot(low, wa2_ref[...]) + ba_ref[...]
    la_ref[...] = (jnp.minimum(z, 0.0) - jnp.log1p(jnp.exp(-jnp.abs(z)))) * (1.0 / GATE_NORM)


def _head_norm_gate(o, norm_g, gate):
    outs = []
    for h in range(N_HEADS):
        cols = slice(h * HEAD_V, (h + 1) * HEAD_V)
        oh = o[:, cols]
        ms = jnp.mean(oh * oh, axis=-1, keepdims=True)
        outs.append(oh * lax.rsqrt(ms + RMS_EPS) * norm_g[:, cols] * _silu(gate[:, cols]))
    return jnp.concatenate(outs, axis=-1)


_GLA_ROW_HEADS = tuple((h * HEAD_QK, HEAD_QK, h) for h in range(N_HEADS))
_RET_ROW_HEADS = tuple((g * (HEAD_QK // 2), HEAD_QK // 2, g % N_HEADS) for g in range(2 * N_HEADS))


def _state_block_diag(state, row_heads):
    zero = None
    out = []
    for r0, n, h in row_heads:
        blk = state[r0:r0 + n, :].astype(BF16)
        zero = jnp.zeros_like(blk) if zero is None else zero
        out.append(jnp.concatenate([blk if j == h else zero for j in range(N_HEADS)], axis=1))
    return jnp.concatenate(out, axis=0)


def _state_compact(kv, row_heads):
    return jnp.concatenate([kv[r0:r0 + n, h * HEAD_V:(h + 1) * HEAD_V] for r0, n, h in row_heads], axis=0)


def _value_block_diag(v):
    zero = jnp.zeros((CHUNK, HEAD_V), BF16)
    return jnp.concatenate(
        [jnp.concatenate([v[:, h * HEAD_V:(h + 1) * HEAD_V] if j == h else zero for j in range(N_HEADS)], axis=1)
         for h in range(N_HEADS)], axis=0)


def _mixer_ffn2_kernel(x1_ref, proj_ref, la_ref,
                       kmask_r_ref, kmask_g_ref, dret_ref, gq_ref, gk_ref, aret_ref, tall_ref, lmask_ref,
                       rn_ref, gn_ref, wout_ref, g2_ref, wg_ref, wu_ref, wd_ref, gf_ref,
                       out_ref, o_ref, act_ref, sr_ref, sg_ref):
    rows_per_step = x1_ref.shape[0]

    @pl.when(pl.program_id(1) == 0)
    def _():
        sr_ref[...] = jnp.zeros_like(sr_ref)
        sg_ref[...] = jnp.zeros_like(sg_ref)

    def scores_ret(q, k):
        w = jnp.concatenate([k] * N_HEADS, axis=0).T * kmask_r_ref[...]
        return _dot(q, w)

    def scores_gla(ql, kl):
        out = []
        for col in range(QK_W // LANE):
            lanes = slice(col * LANE, (col + 1) * LANE)
            w = jnp.concatenate([kl[:, lanes]] * 2, axis=0).T * kmask_g_ref[...]
            out.append(_dot(ql[:, lanes], w))
        return jnp.concatenate(out, axis=1)

    def attend(s, q_in, v, state, row_heads):
        lhs = jnp.concatenate([s.astype(BF16), q_in], axis=1)
        rhs = jnp.concatenate([_value_block_diag(v), _state_block_diag(state, row_heads)], axis=0)
        return _dot(lhs, rhs)

    def state_update(k_out, v, row_heads):
        k_t = k_out.T
        head_rows = [[(r0, n) for r0, n, hh in row_heads if hh == h] for h in range(N_HEADS)]
        per_head = []
        for h in range(N_HEADS):
            lhs = jnp.concatenate([k_t[r0:r0 + n, :] for r0, n in head_rows[h]], axis=0)
            per_head.append(_dot(lhs, v[:, h * HEAD_V:(h + 1) * HEAD_V]))
        pieces, seen = [], [0] * N_HEADS
        for r0, n, h in row_heads:
            pieces.append(per_head[h][seen[h]:seen[h] + n, :])
            seen[h] += n
        return jnp.concatenate(pieces, axis=0)

    chunk_rows = [pl.ds(c * CHUNK, CHUNK) for c in range(rows_per_step // CHUNK)]

    q_r = [proj_ref[r, RQ:RQ + QK_W] for r in chunk_rows]
    k_r = [proj_ref[r, RK:RK + QK_W] for r in chunk_rows]
    v_r = [proj_ref[r, RV:RV + V_W] for r in chunk_rows]
    s_r = [scores_ret(q, k) * dret_ref[...] for q, k in zip(q_r, k_r)]
    qin_r = [(q.astype(F32) * gq_ref[...]).astype(BF16) for q in q_r]
    kv_r = [state_update((k.astype(F32) * gk_ref[...]).astype(BF16), v, _RET_ROW_HEADS)
            for k, v in zip(k_r, v_r)]

    la2 = []
    for r in chunk_rows:
        la = la_ref[r, :]
        la_hi = la.astype(BF16)
        la_lo = (la - la_hi.astype(F32)).astype(BF16)
        la2.append(jnp.concatenate([la_hi, la_lo], axis=0))
    e = [jnp.exp(_dot(tall_ref[...], l)) for l in la2]
    a_col = [jnp.exp(_dot_tn(l, jnp.ones((2 * CHUNK, LANE), BF16))) for l in la2]
    q_g = [proj_ref[r, GQ:GQ + QK_W].astype(F32) for r in chunk_rows]
    k_g = [proj_ref[r, GK:GK + QK_W].astype(F32) for r in chunk_rows]
    v_g = [proj_ref[r, GV:GV + V_W] for r in chunk_rows]
    s_g = [scores_gla(q.astype(BF16), k.astype(BF16)) * lmask_ref[0] for q, k in zip(q_g, k_g)]
    for l in range(len(GLA_LEVELS)):
        for c in range(len(chunk_rows)):
            f = e[c][(2 + l) * CHUNK:(3 + l) * CHUNK, :]
            s_g[c] = s_g[c] + scores_gla((q_g[c] * f).astype(BF16), (k_g[c] * f).astype(BF16)) * lmask_ref[1 + l]
    qin_g = [(q * ec[0:CHUNK, :]).astype(BF16) for q, ec in zip(q_g, e)]
    kv_g = [state_update((k * ec[CHUNK:2 * CHUNK, :]).astype(BF16), v, _GLA_ROW_HEADS)
            for k, ec, v in zip(k_g, e, v_g)]

    sr = sr_ref[...]
    sg = sg_ref[...]
    for c, r in enumerate(chunk_rows):
        o = attend(s_r[c], qin_r[c], v_r[c], sr, _RET_ROW_HEADS)
        sr = sr * aret_ref[...] + kv_r[c]
        gate = proj_ref[r, RG:RG + V_W].astype(F32)
        o_ref[r, 0:V_W] = _head_norm_gate(o, rn_ref[...], gate).astype(BF16)
        o = attend(s_g[c], qin_g[c], v_g[c], sg, _GLA_ROW_HEADS)
        sg = sg * a_col[c] + kv_g[c]
        gate = proj_ref[r, GG:GG + V_W].astype(F32)
        o_ref[r, V_W:2 * V_W] = _head_norm_gate(o, gn_ref[...], gate).astype(BF16)
    sr_ref[...] = sr
    sg_ref[...] = sg

    x2 = x1_ref[...] + _dot(o_ref[...], wout_ref[...])
    x3 = x2 + 0.5 * _ffn(_rms(x2, g2_ref[...]).astype(BF16), wg_ref, wu_ref, wd_ref, act_ref)
    out_ref[...] = _rms(x3, gf_ref[...])


def _gla_tables():
    t = np.arange(CHUNK)[:, None]
    u = np.arange(CHUNK)[None, :]
    blocks = [(u <= t), (u > t)]
    masks = [np.eye(CHUNK, dtype=bool)]
    for s in GLA_LEVELS:
        mid = (t // s) * s + s // 2 - 1
        blocks.append(np.where(t > mid, (u > mid) & (u <= t), (u > t) & (u <= mid)))
        same_block = (t // s) == (u // s)
        masks.append(same_block & (((t % s) < s // 2) != ((u % s) < s // 2)))
    tall = np.concatenate(blocks, axis=0).astype(np.float32)
    tall2 = np.concatenate([tall, tall], axis=1)
    lmask = np.stack([np.tile(m, (1, N_HEADS)) for m in masks]).astype(np.float32)
    return tall2, lmask


def _retention_tables():
    gamma = 1.0 - 2.0 ** (-5.0 - np.arange(N_HEADS, dtype=np.float64))
    lane = np.arange(QK_W)
    head_of_lane = (lane % LANE) // (HEAD_QK // 2)
    i = np.arange(CHUNK)
    dist = np.abs(i[:, None] - i[None, :])
    dret = np.concatenate([gamma[h] ** dist for h in range(N_HEADS)], axis=1)
    gq = gamma[head_of_lane][None, :] ** (i[:, None] + 1.0)
    gk = gamma[head_of_lane][None, :] ** (CHUNK - 1.0 - i[:, None])
    aret = np.broadcast_to((gamma[head_of_lane] ** CHUNK)[:, None], (QK_W, HEAD_V))
    return tuple(np.asarray(a, np.float32) for a in (dret, gq, gk, aret)), head_of_lane


def _block_masks(head_of_lane_r):
    row_head = np.repeat(np.arange(N_HEADS), CHUNK)
    lane = np.arange(QK_W)
    kmask_r = head_of_lane_r[:, None] == row_head[None, :]
    kmask_g = (lane[:LANE] // HEAD_QK)[:, None] == row_head[None, :2 * CHUNK]
    return kmask_r, kmask_g


def _const_spec(shape):
    nd = len(shape)
    return pl.BlockSpec(shape, lambda *_: (0,) * nd, pipeline_mode=pl.Buffered(1))


def _tile_rows(total_rows):
    tile = 512
    assert total_rows % tile == 0
    return tile


@jax.jit
def kernel(x, ffn1_norm_g, ffn1_w_gate, ffn1_w_up, ffn1_w_down, mix_norm_g, w_in, ret_norm_g, gla_w_a2,
           gla_b_a, gla_norm_g, w_out, ffn2_norm_g, ffn2_w_gate, ffn2_w_up, ffn2_w_down, final_norm_g):
    bsz, seq, d = x.shape
    assert d == D_MODEL and seq % CHUNK == 0 and ffn1_norm_g.shape[0] == 1
    tile = _tile_rows(seq)
    steps = seq // tile
    n_rows = bsz * seq

    row = lambda g: g.reshape(1, -1).astype(F32)
    w_in0 = w_in[0]
    half = HEAD_QK // 2
    perm = np.array([h * HEAD_QK + hf * half + c
                     for hf in range(2) for h in range(N_HEADS) for c in range(half)])
    w_main = jnp.concatenate([w_in0[:, RQ:RK][:, perm], w_in0[:, RK:RV][:, perm], w_in0[:, RV:GLOW]],
                             axis=1).astype(BF16)
    w_low = jnp.pad(w_in0[:, GLOW:], ((0, 0), (0, LANE - GATE_RANK))).astype(BF16)
    w_a2 = jnp.pad(gla_w_a2[0], ((0, LANE - GATE_RANK), (0, 0))).astype(BF16)

    pos = jnp.arange(seq, dtype=F32)
    inv = ROPE_BASE ** (-jnp.arange(half, dtype=F32) * 2.0 / HEAD_QK)
    ang = pos[:, None] * inv[None, :]
    cos = jnp.tile(jnp.cos(ang), (1, N_HEADS))
    sin = jnp.tile(jnp.sin(ang), (1, N_HEADS))

    x2d = x.reshape(n_rows, d)
    row_spec = lambda w: pl.BlockSpec((tile, w), lambda i: (i, 0))
    pos_spec = pl.BlockSpec((tile, LANE), lambda i: (i % steps, 0))
    x1, proj, la = pl.pallas_call(
        _ffn1_proj_kernel,
        grid=(n_rows // tile,),
        in_specs=[row_spec(d), pos_spec, pos_spec, _const_spec((1, d)),
                  _const_spec((d, D_FF)), _const_spec((d, D_FF)), _const_spec((D_FF, d)),
                  _const_spec((1, d)), _const_spec((d, PROJ_W)), _const_spec((d, LANE)),
                  _const_spec((LANE, QK_W)), _const_spec((1, QK_W))],
        out_specs=[row_spec(d), row_spec(PROJ_W), row_spec(QK_W)],
        out_shape=[jax.ShapeDtypeStruct((n_rows, d), F32),
                   jax.ShapeDtypeStruct((n_rows, PROJ_W), BF16),
                   jax.ShapeDtypeStruct((n_rows, QK_W), F32)],
        scratch_shapes=[pltpu.VMEM((tile, D_FF), BF16)],
        compiler_params=pltpu.CompilerParams(dimension_semantics=("arbitrary",),
                                             vmem_limit_bytes=VMEM_LIMIT),
        name="ffn1_proj",
    )(x2d, cos, sin, row(ffn1_norm_g), ffn1_w_gate[0].astype(BF16), ffn1_w_up[0].astype(BF16),
      ffn1_w_down[0].astype(BF16), row(mix_norm_g), w_main, w_low, w_a2, row(gla_b_a))

    tall2, lmask = _gla_tables()
    (dret, gq, gk, aret), head_of_lane_r = _retention_tables()
    kmask_r, kmask_g = _block_masks(head_of_lane_r)
    consts = [jnp.asarray(kmask_r, BF16), jnp.asarray(kmask_g, BF16), jnp.asarray(dret), jnp.asarray(gq),
              jnp.asarray(gk), jnp.asarray(aret), jnp.asarray(tall2, BF16), jnp.asarray(lmask)]

    tok_spec = lambda w: pl.BlockSpec((tile, w), lambda b, t: (b * steps + t, 0))
    out = pl.pallas_call(
        _mixer_ffn2_kernel,
        grid=(bsz, steps),
        in_specs=[tok_spec(d), tok_spec(PROJ_W), tok_spec(QK_W)]
                 + [_const_spec(c.shape) for c in consts]
                 + [_const_spec((1, V_W)), _const_spec((1, V_W)), _const_spec((d, d)), _const_spec((1, d)),
                    _const_spec((d, D_FF)), _const_spec((d, D_FF)), _const_spec((D_FF, d)),
                    _const_spec((1, d))],
        out_specs=tok_spec(d),
        out_shape=jax.ShapeDtypeStruct((n_rows, d), F32),
        scratch_shapes=[pltpu.VMEM((tile, 2 * V_W), BF16), pltpu.VMEM((tile, D_FF), BF16),
                        pltpu.VMEM((QK_W, HEAD_V), F32), pltpu.VMEM((QK_W, HEAD_V), F32)],
        compiler_params=pltpu.CompilerParams(dimension_semantics=("arbitrary", "arbitrary"),
                                             vmem_limit_bytes=VMEM_LIMIT),
        name="mixer_ffn2",
    )(x1, proj, la, *consts, row(ret_norm_g), row(gla_norm_g), w_out[0].astype(BF16), row(ffn2_norm_g),
      ffn2_w_gate[0].astype(BF16), ffn2_w_up[0].astype(BF16), ffn2_w_down[0].astype(BF16),
      row(final_norm_g))
    return out.reshape(bsz, seq, d)
```

```python
import functools

import numpy as np
import jax
import jax.numpy as jnp
from jax import lax
from jax.experimental import pallas as pl
from jax.experimental.pallas import tpu as pltpu

D_MODEL = 1024
CHUNK = 64
RMS_EPS = 1e-6
ROPE_BASE = 10000.0
N_HEADS = 4
QK_W = 256
V_W = 512
HEAD_V = V_W // N_HEADS
HEAD_QK = QK_W // N_HEADS
GATE_RANK = 16
GATE_NORM = 16.0
D_FF = 2816
PROJ_W = 2 * (2 * QK_W + 2 * V_W)
RQ, RK, RV, RG = 0, 256, 512, 1024
GQ, GK, GV, GG = 1536, 1792, 2048, 2560
GLOW = 3072

LANE = 128
MXU_N = 256
VMEM_LIMIT = 56 * 1024 * 1024

GLA_LEVELS = (64, 32, 16, 8, 4, 2)
N_EXP_BLOCKS = 2 + len(GLA_LEVELS)

BF16 = jnp.bfloat16
F32 = jnp.float32


def _dot(a, b):
    return jnp.dot(a, b, preferred_element_type=F32)


def _dot_nt(a, b):
    return lax.dot_general(a, b, (((1,), (1,)), ((), ())), preferred_element_type=F32)


def _dot_tn(a, b):
    return lax.dot_general(a, b, (((0,), (0,)), ((), ())), preferred_element_type=F32)


def _rms(x, g):
    ms = jnp.mean(x * x, axis=-1, keepdims=True)
    return x * lax.rsqrt(ms + RMS_EPS) * g


def _silu(x):
    return x * (1.0 / (1.0 + jnp.exp(-x)))


def _ffn(h, wg_ref, wu_ref, wd_ref, act_ref, interleave=None):
    for c in range(D_FF // MXU_N):
        cols = slice(c * MXU_N, (c + 1) * MXU_N)
        gate = _dot(h, wg_ref[:, cols])
        up = _dot(h, wu_ref[:, cols])
        act_ref[:, cols] = (_silu(gate) * up).astype(BF16)
        if interleave is not None:
            next(interleave, None)
    if interleave is not None:
        for _ in interleave:
            pass
    return _dot(act_ref[...], wd_ref[...])


def _ffn1_proj_kernel(x_ref, cos_ref, sin_ref, g1_ref, wg_ref, wu_ref, wd_ref, gm_ref, win_ref,
                      wlow_ref, wa2_ref, ba_ref, x1_ref, proj_ref, la_ref, act_ref):
    def half_tile(rows):
        x = x_ref[rows, :]
        h = _rms(x, g1_ref[...]).astype(BF16)
        yield
        for c in range(D_FF // MXU_N):
            cols = slice(c * MXU_N, (c + 1) * MXU_N)
            gate = _dot(h, wg_ref[:, cols])
            up = _dot(h, wu_ref[:, cols])
            act_ref[rows, cols] = (_silu(gate) * up).astype(BF16)
        yield
        x1 = x + 0.5 * _dot(act_ref[rows, :], wd_ref[...])
        x1_ref[rows, :] = x1
        h = _rms(x1, gm_ref[...]).astype(BF16)
        yield
        cos = cos_ref[rows, :]
        sin = sin_ref[rows, :]
        for off, scale in ((RQ, 1.0), (RK, HEAD_QK ** -0.5)):
            t = _dot(h, win_ref[:, off:off + QK_W])
            t1, t2 = t[:, :LANE], t[:, LANE:]
            proj_ref[rows, off:off + LANE] = ((t1 * cos - t2 * sin) * scale).astype(BF16)
            proj_ref[rows, off + LANE:off + QK_W] = ((t1 * sin + t2 * cos) * scale).astype(BF16)
        proj_ref[rows, RV:GQ] = _dot(h, win_ref[:, RV:GQ]).astype(BF16)
        proj_ref[rows, GQ:GK] = (_dot(h, win_ref[:, GQ:GK]) * (HEAD_QK ** -0.5)).astype(BF16)
        proj_ref[rows, GK:PROJ_W] = _dot(h, win_ref[:, GK:PROJ_W]).astype(BF16)
        low = _dot(h, wlow_ref[...]).astype(BF16)
        yield
        z = _dot(low, wa2_ref[...]) + ba_ref[...]
        la_ref[rows, :] = (jnp.minimum(z, 0.0) - jnp.log(1.0 + jnp.exp(-jnp.abs(z)))) * (1.0 / GATE_NORM)

    n_half = x_ref.shape[0] // 2
    halves = [half_tile(pl.ds(i * n_half, n_half)) for i in range(2)]
    while halves:
        halves = [g for g in halves if next(g, StopIteration) is not StopIteration]


def _head_norm_gate(o, norm_g, gate):
    outs = []
    for h in range(N_HEADS):
        cols = slice(h * HEAD_V, (h + 1) * HEAD_V)
        oh = o[:, cols]
        ms = jnp.mean(oh * oh, axis=-1, keepdims=True)
        outs.append(oh * lax.rsqrt(ms + RMS_EPS) * norm_g[:, cols] * _silu(gate[:, cols]))
    return jnp.concatenate(outs, axis=-1)


_GLA_ROW_HEADS = tuple((h * HEAD_QK, HEAD_QK, h) for h in range(N_HEADS))
_RET_ROW_HEADS = tuple((g * (HEAD_QK // 2), HEAD_QK // 2, g % N_HEADS) for g in range(2 * N_HEADS))


def _state_block_diag(state, row_heads):
    zero = None
    out = []
    for r0, n, h in row_heads:
        blk = state[r0:r0 + n, :].astype(BF16)
        zero = jnp.zeros_like(blk) if zero is None else zero
        out.append(jnp.concatenate([blk if j == h else zero for j in range(N_HEADS)], axis=1))
    return jnp.concatenate(out, axis=0)


def _state_compact(kv, row_heads):
    return jnp.concatenate([kv[r0:r0 + n, h * HEAD_V:(h + 1) * HEAD_V] for r0, n, h in row_heads], axis=0)


def _value_block_diag(v):
    zero = jnp.zeros((CHUNK, HEAD_V), BF16)
    return jnp.concatenate(
        [jnp.concatenate([v[:, h * HEAD_V:(h + 1) * HEAD_V] if j == h else zero for j in range(N_HEADS)], axis=1)
         for h in range(N_HEADS)], axis=0)


SUBLANES = 8


def _midpoint_rows(b, s):
    rows, w = b.shape
    if s >= SUBLANES:
        blocks = b.reshape(rows // s, s, w)
        return jnp.broadcast_to(blocks[:, s // 2 - 1:s // 2, :], blocks.shape).reshape(rows, w)
    groups = b.reshape(rows // SUBLANES, SUBLANES, w)
    sub = lax.broadcasted_iota(jnp.int32, groups.shape, 1)
    out = None
    for first in reversed(range(0, SUBLANES, s)):
        ref_row = jnp.broadcast_to(groups[:, first + s // 2 - 1:first + s // 2, :], groups.shape)
        out = ref_row if out is None else jnp.where(sub < first + s, ref_row, out)
    return out.reshape(rows, w)


def _mixer_ffn2_kernel(x1_ref, proj_ref, la_ref,
                       kmask_r_ref, kmask_g_ref, dret_ref, gq_ref, gk_ref, aret_ref, ltri_ref, lmask_ref,
                       rn_ref, gn_ref, wout_ref, g2_ref, wg_ref, wu_ref, wd_ref, gf_ref,
                       out_ref, o_ref, act_ref, x2_ref, sr_ref, sg_ref, *, steps_per_seq):
    rows_per_step = x1_ref.shape[0]
    step = pl.program_id(0)

    @pl.when(step == 0)
    def _():
        x2_ref[...] = jnp.zeros_like(x2_ref)

    @pl.when(step % steps_per_seq == 0)
    def _():
        sr_ref[...] = jnp.zeros_like(sr_ref)
        sg_ref[...] = jnp.zeros_like(sg_ref)

    def scores_ret(q, k):
        w = jnp.concatenate([k] * N_HEADS, axis=0).T * kmask_r_ref[...]
        return _dot(q, w)

    def scores_gla(ql, kl):
        out = []
        for col in range(QK_W // LANE):
            lanes = slice(col * LANE, (col + 1) * LANE)
            w = jnp.concatenate([kl[:, lanes]] * 2, axis=0).T * kmask_g_ref[...]
            out.append(_dot(ql[:, lanes], w))
        return jnp.concatenate(out, axis=1)

    def attend(s, q_in, v, state, row_heads):
        lhs = jnp.concatenate([s.astype(BF16), q_in], axis=1)
        rhs = jnp.concatenate([_value_block_diag(v), _state_block_diag(state, row_heads)], axis=0)
        return _dot(lhs, rhs)

    def state_update(k_out, v, row_heads):
        k_t = k_out.T
        head_rows = [[(r0, n) for r0, n, hh in row_heads if hh == h] for h in range(N_HEADS)]
        per_head = []
        for h in range(N_HEADS):
            lhs = jnp.concatenate([k_t[r0:r0 + n, :] for r0, n in head_rows[h]], axis=0)
            per_head.append(_dot(lhs, v[:, h * HEAD_V:(h + 1) * HEAD_V]))
        pieces, seen = [], [0] * N_HEADS
        for r0, n, h in row_heads:
            pieces.append(per_head[h][seen[h]:seen[h] + n, :])
            seen[h] += n
        return jnp.concatenate(pieces, axis=0)

    def mixer_stages():
        chunk_rows = [pl.ds(c * CHUNK, CHUNK) for c in range(rows_per_step // CHUNK)]

        q_r = [proj_ref[r, RQ:RQ + QK_W] for r in chunk_rows]
        k_r = [proj_ref[r, RK:RK + QK_W] for r in chunk_rows]
        v_r = [proj_ref[r, RV:RV + V_W] for r in chunk_rows]
        s_r = [scores_ret(q, k) * dret_ref[...] for q, k in zip(q_r, k_r)]
        qin_r = [(q.astype(F32) * gq_ref[...]).astype(BF16) for q in q_r]
        yield
        kv_r = [state_update((k.astype(F32) * gk_ref[...]).astype(BF16), v, _RET_ROW_HEADS)
                for k, v in zip(k_r, v_r)]
        yield

        b = []
        for r in chunk_rows:
            la = la_ref[r, :]
            la_hi = la.astype(BF16)
            la_lo = (la - la_hi.astype(F32)).astype(BF16)
            b.append(_dot(ltri_ref[...], jnp.concatenate([la_hi, la_lo], axis=0)))
        b_last = [jnp.broadcast_to(bc[CHUNK - 1:CHUNK, :], (2 * CHUNK, QK_W)) for bc in b]
        a_col = [jnp.exp(bl.T)[:, :LANE] for bl in b_last]
        yield
        q_g = [proj_ref[r, GQ:GQ + QK_W].astype(F32) for r in chunk_rows]
        k_g = [proj_ref[r, GK:GK + QK_W].astype(F32) for r in chunk_rows]
        v_g = [proj_ref[r, GV:GV + V_W] for r in chunk_rows]
        s_g = [scores_gla(q.astype(BF16), k.astype(BF16)) * lmask_ref[0] for q, k in zip(q_g, k_g)]
        for l, s in enumerate(GLA_LEVELS):
            yield
            for c in range(len(chunk_rows)):
                f = jnp.exp(-jnp.abs(b[c] - _midpoint_rows(b[c], s)))
                s_g[c] = s_g[c] + (scores_gla((q_g[c] * f).astype(BF16), (k_g[c] * f).astype(BF16))
                                   * lmask_ref[1 + l])
        yield
        qin_g = [(q * jnp.exp(bc)).astype(BF16) for q, bc in zip(q_g, b)]
        kv_g = [state_update((k * jnp.exp(bl[:CHUNK, :] - bc)).astype(BF16), v, _GLA_ROW_HEADS)
                for k, bc, bl, v in zip(k_g, b, b_last, v_g)]

        sr = sr_ref[...]
        sg = sg_ref[...]
        for c, r in enumerate(chunk_rows):
            if c % 4 == 0:
                yield
            o = attend(s_r[c], qin_r[c], v_r[c], sr, _RET_ROW_HEADS)
            sr = sr * aret_ref[...] + kv_r[c]
            gate = proj_ref[r, RG:RG + V_W].astype(F32)
            o_ref[r, 0:V_W] = _head_norm_gate(o, rn_ref[...], gate).astype(BF16)
            o = attend(s_g[c], qin_g[c], v_g[c], sg, _GLA_ROW_HEADS)
            sg = sg * a_col[c] + kv_g[c]
            gate = proj_ref[r, GG:GG + V_W].astype(F32)
            o_ref[r, V_W:2 * V_W] = _head_norm_gate(o, gn_ref[...], gate).astype(BF16)
        sr_ref[...] = sr
        sg_ref[...] = sg

    stages = mixer_stages()
    next(stages)
    x2_prev = x2_ref[...]
    y = _ffn(_rms(x2_prev, g2_ref[...]).astype(BF16), wg_ref, wu_ref, wd_ref, act_ref,
             interleave=stages)
    out_ref[...] = _rms(x2_prev + 0.5 * y, gf_ref[...])
    x2_ref[...] = x1_ref[...] + _dot(o_ref[...], wout_ref[...])


def _gla_tables():
    t = np.arange(CHUNK)[:, None]
    u = np.arange(CHUNK)[None, :]
    masks = [np.eye(CHUNK, dtype=bool)]
    for s in GLA_LEVELS:
        same_block = (t // s) == (u // s)
        masks.append(same_block & (((t % s) < s // 2) != ((u % s) < s // 2)))
    ltri = (u <= t).astype(np.float32)
    ltri2 = np.concatenate([ltri, ltri], axis=1)
    lmask = np.stack([np.tile(m, (1, N_HEADS)) for m in masks]).astype(np.float32)
    return ltri2, lmask


def _retention_tables():
    gamma = 1.0 - 2.0 ** (-5.0 - np.arange(N_HEADS, dtype=np.float64))
    lane = np.arange(QK_W)
    head_of_lane = (lane % LANE) // (HEAD_QK // 2)
    i = np.arange(CHUNK)
    dist = np.abs(i[:, None] - i[None, :])
    dret = np.concatenate([gamma[h] ** dist for h in range(N_HEADS)], axis=1)
    gq = gamma[head_of_lane][None, :] ** (i[:, None] + 1.0)
    gk = gamma[head_of_lane][None, :] ** (CHUNK - 1.0 - i[:, None])
    aret = np.broadcast_to((gamma[head_of_lane] ** CHUNK)[:, None], (QK_W, HEAD_V))
    return tuple(np.asarray(a, np.float32) for a in (dret, gq, gk, aret)), head_of_lane


def _block_masks(head_of_lane_r):
    row_head = np.repeat(np.arange(N_HEADS), CHUNK)
    lane = np.arange(QK_W)
    kmask_r = head_of_lane_r[:, None] == row_head[None, :]
    kmask_g = (lane[:LANE] // HEAD_QK)[:, None] == row_head[None, :2 * CHUNK]
    return kmask_r, kmask_g


def _const_spec(shape):
    nd = len(shape)
    return pl.BlockSpec(shape, lambda *_: (0,) * nd, pipeline_mode=pl.Buffered(1))


def _tile_rows(total_rows):
    tile = 512
    assert total_rows % tile == 0
    return tile


@jax.jit
def kernel(x, ffn1_norm_g, ffn1_w_gate, ffn1_w_up, ffn1_w_down, mix_norm_g, w_in, ret_norm_g, gla_w_a2,
           gla_b_a, gla_norm_g, w_out, ffn2_norm_g, ffn2_w_gate, ffn2_w_up, ffn2_w_down, final_norm_g):
    bsz, seq, d = x.shape
    assert d == D_MODEL and seq % CHUNK == 0 and ffn1_norm_g.shape[0] == 1
    tile = _tile_rows(seq)
    steps = seq // tile
    n_rows = bsz * seq

    row = lambda g: g.reshape(1, -1).astype(F32)
    w_in0 = w_in[0]
    half = HEAD_QK // 2
    perm = np.array([h * HEAD_QK + hf * half + c
                     for hf in range(2) for h in range(N_HEADS) for c in range(half)])
    w_main = jnp.concatenate([w_in0[:, RQ:RK][:, perm], w_in0[:, RK:RV][:, perm], w_in0[:, RV:GLOW]],
                             axis=1).astype(BF16)
    w_low = jnp.pad(w_in0[:, GLOW:], ((0, 0), (0, LANE - GATE_RANK))).astype(BF16)
    w_a2 = jnp.pad(gla_w_a2[0], ((0, LANE - GATE_RANK), (0, 0))).astype(BF16)

    pos = jnp.arange(seq, dtype=F32)
    inv = ROPE_BASE ** (-jnp.arange(half, dtype=F32) * 2.0 / HEAD_QK)
    ang = pos[:, None] * inv[None, :]
    cos = jnp.tile(jnp.cos(ang), (1, N_HEADS))
    sin = jnp.tile(jnp.sin(ang), (1, N_HEADS))

    x2d = x.reshape(n_rows, d)
    row_spec = lambda w: pl.BlockSpec((tile, w), lambda i: (i, 0))
    pos_spec = pl.BlockSpec((tile, LANE), lambda i: (i % steps, 0))
    x1, proj, la = pl.pallas_call(
        _ffn1_proj_kernel,
        grid=(n_rows // tile,),
        in_specs=[row_spec(d), pos_spec, pos_spec, _const_spec((1, d)),
                  _const_spec((d, D_FF)), _const_spec((d, D_FF)), _const_spec((D_FF, d)),
                  _const_spec((1, d)), _const_spec((d, PROJ_W)), _const_spec((d, LANE)),
                  _const_spec((LANE, QK_W)), _const_spec((1, QK_W))],
        out_specs=[row_spec(d), row_spec(PROJ_W), row_spec(QK_W)],
        out_shape=[jax.ShapeDtypeStruct((n_rows, d), F32),
                   jax.ShapeDtypeStruct((n_rows, PROJ_W), BF16),
                   jax.ShapeDtypeStruct((n_rows, QK_W), F32)],
        scratch_shapes=[pltpu.VMEM((tile, D_FF), BF16)],
        compiler_params=pltpu.CompilerParams(dimension_semantics=("arbitrary",),
                                             vmem_limit_bytes=VMEM_LIMIT),
        name="ffn1_proj",
    )(x2d, cos, sin, row(ffn1_norm_g), ffn1_w_gate[0].astype(BF16), ffn1_w_up[0].astype(BF16),
      ffn1_w_down[0].astype(BF16), row(mix_norm_g), w_main, w_low, w_a2, row(gla_b_a))

    ltri2, lmask = _gla_tables()
    (dret, gq, gk, aret), head_of_lane_r = _retention_tables()
    kmask_r, kmask_g = _block_masks(head_of_lane_r)
    consts = [jnp.asarray(kmask_r, BF16), jnp.asarray(kmask_g, BF16), jnp.asarray(dret), jnp.asarray(gq),
              jnp.asarray(gk), jnp.asarray(aret), jnp.asarray(ltri2, BF16), jnp.asarray(lmask)]

    n_tiles = n_rows // tile
    tok_spec = lambda w: pl.BlockSpec((tile, w), lambda i: (jnp.minimum(i, n_tiles - 1), 0))
    out = pl.pallas_call(
        functools.partial(_mixer_ffn2_kernel, steps_per_seq=steps),
        grid=(n_tiles + 1,),
        in_specs=[tok_spec(d), tok_spec(PROJ_W), tok_spec(QK_W)]
                 + [_const_spec(c.shape) for c in consts]
                 + [_const_spec((1, V_W)), _const_spec((1, V_W)), _const_spec((d, d)), _const_spec((1, d)),
                    _const_spec((d, D_FF)), _const_spec((d, D_FF)), _const_spec((D_FF, d)),
                    _const_spec((1, d))],
        out_specs=pl.BlockSpec((tile, d), lambda i: (jnp.maximum(i - 1, 0), 0)),
        out_shape=jax.ShapeDtypeStruct((n_rows, d), F32),
        scratch_shapes=[pltpu.VMEM((tile, 2 * V_W), BF16), pltpu.VMEM((tile, D_FF), BF16),
                        pltpu.VMEM((tile, d), F32),
                        pltpu.VMEM((QK_W, HEAD_V), F32), pltpu.VMEM((QK_W, HEAD_V), F32)],
        compiler_params=pltpu.CompilerParams(dimension_semantics=("arbitrary",),
                                             vmem_limit_bytes=VMEM_LIMIT),
        name="mixer_ffn2",
    )(x1, proj, la, *consts, row(ret_norm_g), row(gla_norm_g), w_out[0].astype(BF16), row(ffn2_norm_g),
      ffn2_w_gate[0].astype(BF16), ffn2_w_up[0].astype(BF16), ffn2_w_down[0].astype(BF16),
      row(final_norm_g))
    return out.reshape(bsz, seq, d)
```

```python
import functools

import numpy as np
import jax
import jax.numpy as jnp
from jax import lax
from jax.experimental import pallas as pl
from jax.experimental.pallas import tpu as pltpu

D_MODEL = 1024
CHUNK = 64
RMS_EPS = 1e-6
ROPE_BASE = 10000.0
N_HEADS = 4
QK_W = 256
V_W = 512
HEAD_V = V_W // N_HEADS
HEAD_QK = QK_W // N_HEADS
GATE_RANK = 16
GATE_NORM = 16.0
D_FF = 2816
PROJ_W = 2 * (2 * QK_W + 2 * V_W)
RQ, RK, RV, RG = 0, 256, 512, 1024
GQ, GK, GV, GG = 1536, 1792, 2048, 2560
GLOW = 3072

LANE = 128
MXU_N = 256
VMEM_LIMIT = 56 * 1024 * 1024

GLA_LEVELS = (64, 32, 16, 8, 4, 2)
N_EXP_BLOCKS = 2 + len(GLA_LEVELS)

BF16 = jnp.bfloat16
F32 = jnp.float32


def _dot(a, b):
    return jnp.dot(a, b, preferred_element_type=F32)


def _dot_nt(a, b):
    return lax.dot_general(a, b, (((1,), (1,)), ((), ())), preferred_element_type=F32)


def _dot_tn(a, b):
    return lax.dot_general(a, b, (((0,), (0,)), ((), ())), preferred_element_type=F32)


def _rms(x, g):
    ms = jnp.mean(x * x, axis=-1, keepdims=True)
    return x * lax.rsqrt(ms + RMS_EPS) * g


def _silu(x):
    return x * (1.0 / (1.0 + jnp.exp(-x)))


def _ffn(h, wg_ref, wu_ref, wd_ref, act_ref, interleave=None):
    for c in range(D_FF // MXU_N):
        cols = slice(c * MXU_N, (c + 1) * MXU_N)
        gate = _dot(h, wg_ref[:, cols])
        up = _dot(h, wu_ref[:, cols])
        act_ref[:, cols] = (_silu(gate) * up).astype(BF16)
        if interleave is not None:
            next(interleave, None)
    if interleave is not None:
        for _ in interleave:
            pass
    return _dot(act_ref[...], wd_ref[...])


def _ffn1_proj_kernel(x_ref, cos_ref, sin_ref, g1_ref, wg_ref, wu_ref, wd_ref, gm_ref, wqk_ref, win_ref,
                      wlow_ref, wa2_ref, ba_ref, *rest):
    later_f32, (x1_ref, proj_ref, la_ref), later_bf16, act_ref = rest[:4], rest[4:7], rest[7:11], rest[11]

    def cast_later_weights():
        yield
        for src, dst in zip(later_f32, later_bf16):
            dst[...] = src[...].astype(BF16)

    def half_tile(rows):
        x = x_ref[rows, :]
        h = _rms(x, g1_ref[...]).astype(BF16)
        yield
        for c in range(D_FF // MXU_N):
            cols = slice(c * MXU_N, (c + 1) * MXU_N)
            gate = _dot(h, wg_ref[:, cols])
            up = _dot(h, wu_ref[:, cols])
            act_ref[rows, cols] = (_silu(gate) * up).astype(BF16)
        yield
        x1 = x + 0.5 * _dot(act_ref[rows, :], wd_ref[...])
        x1_ref[rows, :] = x1
        h = _rms(x1, gm_ref[...]).astype(BF16)
        yield
        cos = cos_ref[rows, :]
        sin = sin_ref[rows, :]
        for off, scale in ((RQ, 1.0), (RK, HEAD_QK ** -0.5)):
            t = _dot(h, wqk_ref[:, off:off + QK_W])
            t1, t2 = t[:, :LANE], t[:, LANE:]
            proj_ref[rows, off:off + LANE] = ((t1 * cos - t2 * sin) * scale).astype(BF16)
            proj_ref[rows, off + LANE:off + QK_W] = ((t1 * sin + t2 * cos) * scale).astype(BF16)
        proj_ref[rows, RV:GQ] = _dot(h, win_ref[:, 0:GQ - RV]).astype(BF16)
        proj_ref[rows, GQ:GK] = (_dot(h, win_ref[:, GQ - RV:GK - RV]) * (HEAD_QK ** -0.5)).astype(BF16)
        proj_ref[rows, GK:PROJ_W] = _dot(h, win_ref[:, GK - RV:PROJ_W - RV]).astype(BF16)
        low = _dot(h, wlow_ref[...]).astype(BF16)
        yield
        z = _dot(low, wa2_ref[...]) + ba_ref[...]
        la_ref[rows, :] = (jnp.minimum(z, 0.0) - jnp.log(1.0 + jnp.exp(-jnp.abs(z)))) * (1.0 / GATE_NORM)

    n_half = x_ref.shape[0] // 2
    streams = [half_tile(pl.ds(i * n_half, n_half)) for i in range(2)] + [cast_later_weights()]
    while streams:
        streams = [g for g in streams if next(g, StopIteration) is not StopIteration]


def _head_norm_gate(o, norm_g, gate):
    outs = []
    for h in range(N_HEADS):
        cols = slice(h * HEAD_V, (h + 1) * HEAD_V)
        oh = o[:, cols]
        ms = jnp.mean(oh * oh, axis=-1, keepdims=True)
        outs.append(oh * lax.rsqrt(ms + RMS_EPS) * norm_g[:, cols] * _silu(gate[:, cols]))
    return jnp.concatenate(outs, axis=-1)


_GLA_ROW_HEADS = tuple((h * HEAD_QK, HEAD_QK, h) for h in range(N_HEADS))
_RET_ROW_HEADS = tuple((g * (HEAD_QK // 2), HEAD_QK // 2, g % N_HEADS) for g in range(2 * N_HEADS))


def _state_block_diag(state, row_heads):
    zero = None
    out = []
    for r0, n, h in row_heads:
        blk = state[r0:r0 + n, :].astype(BF16)
        zero = jnp.zeros_like(blk) if zero is None else zero
        out.append(jnp.concatenate([blk if j == h else zero for j in range(N_HEADS)], axis=1))
    return jnp.concatenate(out, axis=0)


def _state_compact(kv, row_heads):
    return jnp.concatenate([kv[r0:r0 + n, h * HEAD_V:(h + 1) * HEAD_V] for r0, n, h in row_heads], axis=0)


def _value_block_diag(v):
    zero = jnp.zeros((CHUNK, HEAD_V), BF16)
    return jnp.concatenate(
        [jnp.concatenate([v[:, h * HEAD_V:(h + 1) * HEAD_V] if j == h else zero for j in range(N_HEADS)], axis=1)
         for h in range(N_HEADS)], axis=0)


SUBLANES = 8


def _midpoint_rows(b, s):
    rows, w = b.shape
    if s >= SUBLANES:
        blocks = b.reshape(rows // s, s, w)
        return jnp.broadcast_to(blocks[:, s // 2 - 1:s // 2, :], blocks.shape).reshape(rows, w)
    groups = b.reshape(rows // SUBLANES, SUBLANES, w)
    sub = lax.broadcasted_iota(jnp.int32, groups.shape, 1)
    out = None
    for first in reversed(range(0, SUBLANES, s)):
        ref_row = jnp.broadcast_to(groups[:, first + s // 2 - 1:first + s // 2, :], groups.shape)
        out = ref_row if out is None else jnp.where(sub < first + s, ref_row, out)
    return out.reshape(rows, w)


def _mixer_ffn2_kernel(x1_ref, proj_ref, la_ref,
                       kmask_r_ref, kmask_g_ref, dret_ref, gq_ref, gk_ref, aret_ref, ltri_ref, lmask_ref,
                       rn_ref, gn_ref, wout_ref, g2_ref, wg_ref, wu_ref, wd_ref, gf_ref,
                       out_ref, o_ref, act_ref, x2_ref, sr_ref, sg_ref, *, steps_per_seq):
    rows_per_step = x1_ref.shape[0]
    step = pl.program_id(0)

    @pl.when(step == 0)
    def _():
        x2_ref[...] = jnp.zeros_like(x2_ref)

    @pl.when(step % steps_per_seq == 0)
    def _():
        sr_ref[...] = jnp.zeros_like(sr_ref)
        sg_ref[...] = jnp.zeros_like(sg_ref)

    def scores_ret(q, k):
        w = jnp.concatenate([k] * N_HEADS, axis=0).T * kmask_r_ref[...]
        return _dot(q, w)

    def scores_gla(ql, kl):
        out = []
        for col in range(QK_W // LANE):
            lanes = slice(col * LANE, (col + 1) * LANE)
            w = jnp.concatenate([kl[:, lanes]] * 2, axis=0).T * kmask_g_ref[...]
            out.append(_dot(ql[:, lanes], w))
        return jnp.concatenate(out, axis=1)

    def attend(s, q_in, v, state, row_heads):
        lhs = jnp.concatenate([s.astype(BF16), q_in], axis=1)
        rhs = jnp.concatenate([_value_block_diag(v), _state_block_diag(state, row_heads)], axis=0)
        return _dot(lhs, rhs)

    def state_update(k_out, v, row_heads):
        k_t = k_out.T
        head_rows = [[(r0, n) for r0, n, hh in row_heads if hh == h] for h in range(N_HEADS)]
        per_head = []
        for h in range(N_HEADS):
            lhs = jnp.concatenate([k_t[r0:r0 + n, :] for r0, n in head_rows[h]], axis=0)
            per_head.append(_dot(lhs, v[:, h * HEAD_V:(h + 1) * HEAD_V]))
        pieces, seen = [], [0] * N_HEADS
        for r0, n, h in row_heads:
            pieces.append(per_head[h][seen[h]:seen[h] + n, :])
            seen[h] += n
        return jnp.concatenate(pieces, axis=0)

    def mixer_stages():
        chunk_rows = [pl.ds(c * CHUNK, CHUNK) for c in range(rows_per_step // CHUNK)]

        q_r = [proj_ref[r, RQ:RQ + QK_W] for r in chunk_rows]
        k_r = [proj_ref[r, RK:RK + QK_W] for r in chunk_rows]
        v_r = [proj_ref[r, RV:RV + V_W] for r in chunk_rows]
        s_r = [scores_ret(q, k) * dret_ref[...] for q, k in zip(q_r, k_r)]
        qin_r = [(q.astype(F32) * gq_ref[...]).astype(BF16) for q in q_r]
        yield
        kv_r = [state_update((k.astype(F32) * gk_ref[...]).astype(BF16), v, _RET_ROW_HEADS)
                for k, v in zip(k_r, v_r)]
        yield

        b = []
        for r in chunk_rows:
            la = la_ref[r, :]
            la_hi = la.astype(BF16)
            la_lo = (la - la_hi.astype(F32)).astype(BF16)
            b.append(_dot(ltri_ref[...], jnp.concatenate([la_hi, la_lo], axis=0)))
        b_last = [jnp.broadcast_to(bc[CHUNK - 1:CHUNK, :], (2 * CHUNK, QK_W)) for bc in b]
        a_col = [jnp.exp(bl.T)[:, :LANE] for bl in b_last]
        yield
        q_g = [proj_ref[r, GQ:GQ + QK_W].astype(F32) for r in chunk_rows]
        k_g = [proj_ref[r, GK:GK + QK_W].astype(F32) for r in chunk_rows]
        v_g = [proj_ref[r, GV:GV + V_W] for r in chunk_rows]
        s_g = [scores_gla(q.astype(BF16), k.astype(BF16)) * lmask_ref[0] for q, k in zip(q_g, k_g)]
        for l, s in enumerate(GLA_LEVELS):
            yield
            for c in range(len(chunk_rows)):
                f = jnp.exp(-jnp.abs(b[c] - _midpoint_rows(b[c], s)))
                s_g[c] = s_g[c] + (scores_gla((q_g[c] * f).astype(BF16), (k_g[c] * f).astype(BF16))
                                   * lmask_ref[1 + l])
        yield
        qin_g = [(q * jnp.exp(bc)).astype(BF16) for q, bc in zip(q_g, b)]
        kv_g = [state_update((k * jnp.exp(bl[:CHUNK, :] - bc)).astype(BF16), v, _GLA_ROW_HEADS)
                for k, bc, bl, v in zip(k_g, b, b_last, v_g)]

        sr = sr_ref[...]
        sg = sg_ref[...]
        for c, r in enumerate(chunk_rows):
            if c % 4 == 0:
                yield
            o = attend(s_r[c], qin_r[c], v_r[c], sr, _RET_ROW_HEADS)
            sr = sr * aret_ref[...] + kv_r[c]
            gate = proj_ref[r, RG:RG + V_W].astype(F32)
            o_ref[r, 0:V_W] = _head_norm_gate(o, rn_ref[...], gate).astype(BF16)
            o = attend(s_g[c], qin_g[c], v_g[c], sg, _GLA_ROW_HEADS)
            sg = sg * a_col[c] + kv_g[c]
            gate = proj_ref[r, GG:GG + V_W].astype(F32)
            o_ref[r, V_W:2 * V_W] = _head_norm_gate(o, gn_ref[...], gate).astype(BF16)
        sr_ref[...] = sr
        sg_ref[...] = sg

    stages = mixer_stages()
    next(stages)
    x2_prev = x2_ref[...]
    y = _ffn(_rms(x2_prev, g2_ref[...]).astype(BF16), wg_ref, wu_ref, wd_ref, act_ref,
             interleave=stages)
    out_ref[...] = _rms(x2_prev + 0.5 * y, gf_ref[...])
    x2_ref[...] = x1_ref[...] + _dot(o_ref[...], wout_ref[...])


def _gla_tables():
    t = np.arange(CHUNK)[:, None]
    u = np.arange(CHUNK)[None, :]
    masks = [np.eye(CHUNK, dtype=bool)]
    for s in GLA_LEVELS:
        same_block = (t // s) == (u // s)
        masks.append(same_block & (((t % s) < s // 2) != ((u % s) < s // 2)))
    ltri = (u <= t).astype(np.float32)
    ltri2 = np.concatenate([ltri, ltri], axis=1)
    lmask = np.stack([np.tile(m, (1, N_HEADS)) for m in masks]).astype(np.float32)
    return ltri2, lmask


def _retention_tables():
    gamma = 1.0 - 2.0 ** (-5.0 - np.arange(N_HEADS, dtype=np.float64))
    lane = np.arange(QK_W)
    head_of_lane = (lane % LANE) // (HEAD_QK // 2)
    i = np.arange(CHUNK)
    dist = np.abs(i[:, None] - i[None, :])
    dret = np.concatenate([gamma[h] ** dist for h in range(N_HEADS)], axis=1)
    gq = gamma[head_of_lane][None, :] ** (i[:, None] + 1.0)
    gk = gamma[head_of_lane][None, :] ** (CHUNK - 1.0 - i[:, None])
    aret = np.broadcast_to((gamma[head_of_lane] ** CHUNK)[:, None], (QK_W, HEAD_V))
    return tuple(np.asarray(a, np.float32) for a in (dret, gq, gk, aret)), head_of_lane


def _block_masks(head_of_lane_r):
    row_head = np.repeat(np.arange(N_HEADS), CHUNK)
    lane = np.arange(QK_W)
    kmask_r = head_of_lane_r[:, None] == row_head[None, :]
    kmask_g = (lane[:LANE] // HEAD_QK)[:, None] == row_head[None, :2 * CHUNK]
    return kmask_r, kmask_g


def _const_spec(shape):
    nd = len(shape)
    return pl.BlockSpec(shape, lambda *_: (0,) * nd, pipeline_mode=pl.Buffered(1))


BF16_SUBLANES = 16


def _slab_spec(shape, n_steps):
    rows, cols = shape
    slab = next(s for s in range(BF16_SUBLANES, rows + 1, BF16_SUBLANES)
                if rows % s == 0 and n_steps % (rows // s) == 0 and rows // s <= n_steps)
    repeat = n_steps // (rows // slab)
    return pl.BlockSpec((slab, cols), lambda i: (i // repeat, 0))


def _tile_rows(total_rows):
    tile = 512
    assert total_rows % tile == 0
    return tile


@jax.jit
def kernel(x, ffn1_norm_g, ffn1_w_gate, ffn1_w_up, ffn1_w_down, mix_norm_g, w_in, ret_norm_g, gla_w_a2,
           gla_b_a, gla_norm_g, w_out, ffn2_norm_g, ffn2_w_gate, ffn2_w_up, ffn2_w_down, final_norm_g):
    bsz, seq, d = x.shape
    assert d == D_MODEL and seq % CHUNK == 0 and ffn1_norm_g.shape[0] == 1
    tile = _tile_rows(seq)
    steps = seq // tile
    n_rows = bsz * seq

    row = lambda g: g.reshape(1, -1).astype(F32)
    w_in0 = w_in[0]
    half = HEAD_QK // 2
    perm = np.array([h * HEAD_QK + hf * half + c
                     for hf in range(2) for h in range(N_HEADS) for c in range(half)])
    w_qk = w_in0[:, np.concatenate([RQ + perm, RK + perm])].astype(BF16)
    w_rest = w_in0[:, RV:GLOW].astype(BF16)
    w_low = jnp.pad(w_in0[:, GLOW:], ((0, 0), (0, LANE - GATE_RANK))).astype(BF16)
    w_a2 = jnp.pad(gla_w_a2[0], ((0, LANE - GATE_RANK), (0, 0))).astype(BF16)

    pos = jnp.arange(seq, dtype=F32)
    inv = ROPE_BASE ** (-jnp.arange(half, dtype=F32) * 2.0 / HEAD_QK)
    ang = pos[:, None] * inv[None, :]
    cos = jnp.tile(jnp.cos(ang), (1, N_HEADS))
    sin = jnp.tile(jnp.sin(ang), (1, N_HEADS))

    x2d = x.reshape(n_rows, d)
    n_tiles = n_rows // tile
    row_spec = lambda w: pl.BlockSpec((tile, w), lambda i: (i, 0))
    pos_spec = pl.BlockSpec((tile, LANE), lambda i: (i % steps, 0))
    later = [w_out[0], ffn2_w_gate[0], ffn2_w_up[0], ffn2_w_down[0]]
    later_specs = [_slab_spec(w.shape, n_tiles) for w in later]
    x1, proj, la, w_out_b, wg2_b, wu2_b, wd2_b = pl.pallas_call(
        _ffn1_proj_kernel,
        grid=(n_tiles,),
        in_specs=[row_spec(d), pos_spec, pos_spec, _const_spec((1, d)),
                  _const_spec((d, D_FF)), _const_spec((d, D_FF)), _const_spec((D_FF, d)),
                  _const_spec((1, d)), _const_spec((d, RV)), _const_spec((d, PROJ_W - RV)),
                  _const_spec((d, LANE)), _const_spec((LANE, QK_W)), _const_spec((1, QK_W))] + later_specs,
        out_specs=[row_spec(d), row_spec(PROJ_W), row_spec(QK_W)] + later_specs,
        out_shape=[jax.ShapeDtypeStruct((n_rows, d), F32),
                   jax.ShapeDtypeStruct((n_rows, PROJ_W), BF16),
                   jax.ShapeDtypeStruct((n_rows, QK_W), F32)]
                  + [jax.ShapeDtypeStruct(w.shape, BF16) for w in later],
        scratch_shapes=[pltpu.VMEM((tile, D_FF), BF16)],
        compiler_params=pltpu.CompilerParams(dimension_semantics=("arbitrary",),
                                             vmem_limit_bytes=VMEM_LIMIT),
        name="ffn1_proj",
    )(x2d, cos, sin, row(ffn1_norm_g), ffn1_w_gate[0].astype(BF16), ffn1_w_up[0].astype(BF16),
      ffn1_w_down[0].astype(BF16), row(mix_norm_g), w_qk, w_rest, w_low, w_a2, row(gla_b_a), *later)

    ltri2, lmask = _gla_tables()
    (dret, gq, gk, aret), head_of_lane_r = _retention_tables()
    kmask_r, kmask_g = _block_masks(head_of_lane_r)
    consts = [jnp.asarray(kmask_r, BF16), jnp.asarray(kmask_g, BF16), jnp.asarray(dret), jnp.asarray(gq),
              jnp.asarray(gk), jnp.asarray(aret), jnp.asarray(ltri2, BF16), jnp.asarray(lmask)]

    tok_spec = lambda w: pl.BlockSpec((tile, w), lambda i: (jnp.minimum(i, n_tiles - 1), 0))
    out = pl.pallas_call(
        functools.partial(_mixer_ffn2_kernel, steps_per_seq=steps),
        grid=(n_tiles + 1,),
        in_specs=[tok_spec(d), tok_spec(PROJ_W), tok_spec(QK_W)]
                 + [_const_spec(c.shape) for c in consts]
                 + [_const_spec((1, V_W)), _const_spec((1, V_W)), _const_spec((d, d)), _const_spec((1, d)),
                    _const_spec((d, D_FF)), _const_spec((d, D_FF)), _const_spec((D_FF, d)),
                    _const_spec((1, d))],
        out_specs=pl.BlockSpec((tile, d), lambda i: (jnp.maximum(i - 1, 0), 0)),
        out_shape=jax.ShapeDtypeStruct((n_rows, d), F32),
        scratch_shapes=[pltpu.VMEM((tile, 2 * V_W), BF16), pltpu.VMEM((tile, D_FF), BF16),
                        pltpu.VMEM((tile, d), F32),
                        pltpu.VMEM((QK_W, HEAD_V), F32), pltpu.VMEM((QK_W, HEAD_V), F32)],
        compiler_params=pltpu.CompilerParams(dimension_semantics=("arbitrary",),
                                             vmem_limit_bytes=VMEM_LIMIT),
        name="mixer_ffn2",
    )(x1, proj, la, *consts, row(ret_norm_g), row(gla_norm_g), w_out_b, row(ffn2_norm_g),
      wg2_b, wu2_b, wd2_b, row(final_norm_g))
    return out.reshape(bsz, seq, d)
```

```python
import functools

import numpy as np
import jax
import jax.numpy as jnp
from jax import lax
from jax.experimental import pallas as pl
from jax.experimental.pallas import tpu as pltpu

D_MODEL = 1024
CHUNK = 64
RMS_EPS = 1e-6
ROPE_BASE = 10000.0
N_HEADS = 4
QK_W = 256
V_W = 512
HEAD_V = V_W // N_HEADS
HEAD_QK = QK_W // N_HEADS
GATE_RANK = 16
GATE_NORM = 16.0
D_FF = 2816
PROJ_W = 2 * (2 * QK_W + 2 * V_W)
RQ, RK, RV, RG = 0, 256, 512, 1024
GQ, GK, GV, GG = 1536, 1792, 2048, 2560
GLOW = 3072

LANE = 128
MXU_N = 256
VMEM_LIMIT = 56 * 1024 * 1024

GLA_LEVELS = (64, 32, 16, 8, 4, 2)
N_EXP_BLOCKS = 2 + len(GLA_LEVELS)

BF16 = jnp.bfloat16
F32 = jnp.float32


def _dot(a, b):
    return jnp.dot(a, b, preferred_element_type=F32)


def _dot_nt(a, b):
    return lax.dot_general(a, b, (((1,), (1,)), ((), ())), preferred_element_type=F32)


def _dot_tn(a, b):
    return lax.dot_general(a, b, (((0,), (0,)), ((), ())), preferred_element_type=F32)


def _rms(x, g):
    ms = jnp.mean(x * x, axis=-1, keepdims=True)
    return x * lax.rsqrt(ms + RMS_EPS) * g


def _silu(x):
    return x * (1.0 / (1.0 + jnp.exp(-x)))


def _ffn(h, wg_ref, wu_ref, wd_ref, act_ref, interleave=None):
    for c in range(D_FF // MXU_N):
        cols = slice(c * MXU_N, (c + 1) * MXU_N)
        gate = _dot(h, wg_ref[:, cols])
        up = _dot(h, wu_ref[:, cols])
        act_ref[:, cols] = (_silu(gate) * up).astype(BF16)
        if interleave is not None:
            next(interleave, None)
    if interleave is not None:
        for _ in interleave:
            pass
    return _dot(act_ref[...], wd_ref[...])


def _ffn1_proj_kernel(x_ref, cos_ref, sin_ref, g1_ref, wg_ref, wu_ref, wd_ref, gm_ref, win_ref,
                      wlow_ref, wa2_ref, ba_ref, *rest):
    later_f32, (x1_ref, proj_ref, la_ref), later_bf16, act_ref = rest[:4], rest[4:7], rest[7:11], rest[11]

    def cast_later_weights():
        yield
        for src, dst in zip(later_f32, later_bf16):
            dst[...] = src[...].astype(BF16)

    def half_tile(rows):
        x = x_ref[rows, :]
        h = _rms(x, g1_ref[...]).astype(BF16)
        yield
        for c in range(D_FF // MXU_N):
            cols = slice(c * MXU_N, (c + 1) * MXU_N)
            gate = _dot(h, wg_ref[:, cols])
            up = _dot(h, wu_ref[:, cols])
            act_ref[rows, cols] = (_silu(gate) * up).astype(BF16)
        yield
        x1 = x + 0.5 * _dot(act_ref[rows, :], wd_ref[...])
        x1_ref[rows, :] = x1
        h = _rms(x1, gm_ref[...]).astype(BF16)
        yield
        cos = cos_ref[rows, :]
        sin = sin_ref[rows, :]
        first_half = (lax.broadcasted_iota(jnp.int32, cos.shape, 1) % HEAD_QK) < HEAD_QK // 2
        for off, scale in ((RQ, 1.0), (RK, HEAD_QK ** -0.5)):
            t = _dot(h, win_ref[:, off:off + QK_W])
            for col in range(QK_W // LANE):
                tc = t[:, col * LANE:(col + 1) * LANE]
                partner = jnp.where(first_half, pltpu.roll(tc, LANE - HEAD_QK // 2, 1),
                                    pltpu.roll(tc, HEAD_QK // 2, 1))
                proj_ref[rows, off + col * LANE:off + (col + 1) * LANE] = (
                    (tc * cos + partner * sin) * scale).astype(BF16)
        proj_ref[rows, RV:GQ] = _dot(h, win_ref[:, RV:GQ]).astype(BF16)
        proj_ref[rows, GQ:GK] = (_dot(h, win_ref[:, GQ:GK]) * (HEAD_QK ** -0.5)).astype(BF16)
        proj_ref[rows, GK:PROJ_W] = _dot(h, win_ref[:, GK:PROJ_W]).astype(BF16)
        low = _dot(h, wlow_ref[...]).astype(BF16)
        yield
        z = _dot(low, wa2_ref[...]) + ba_ref[...]
        la_ref[rows, :] = (jnp.minimum(z, 0.0) - jnp.log(1.0 + jnp.exp(-jnp.abs(z)))) * (1.0 / GATE_NORM)

    n_half = x_ref.shape[0] // 2
    streams = [half_tile(pl.ds(i * n_half, n_half)) for i in range(2)] + [cast_later_weights()]
    while streams:
        streams = [g for g in streams if next(g, StopIteration) is not StopIteration]


def _head_norm_gate(o, norm_g, gate):
    outs = []
    for h in range(N_HEADS):
        cols = slice(h * HEAD_V, (h + 1) * HEAD_V)
        oh = o[:, cols]
        ms = jnp.mean(oh * oh, axis=-1, keepdims=True)
        outs.append(oh * lax.rsqrt(ms + RMS_EPS) * norm_g[:, cols] * _silu(gate[:, cols]))
    return jnp.concatenate(outs, axis=-1)


def _head_block_diag(blocks):
    zero = jnp.zeros_like(blocks[0])
    return jnp.concatenate(
        [jnp.concatenate([blk if j == h else zero for j in range(N_HEADS)], axis=1)
         for h, blk in enumerate(blocks)], axis=0)


def _state_block_diag(state):
    return _head_block_diag([state[h * HEAD_QK:(h + 1) * HEAD_QK, :].astype(BF16) for h in range(N_HEADS)])


def _value_block_diag(v):
    return _head_block_diag([v[:, h * HEAD_V:(h + 1) * HEAD_V] for h in range(N_HEADS)])


SUBLANES = 8


def _midpoint_rows(b, s):
    rows, w = b.shape
    if s >= SUBLANES:
        blocks = b.reshape(rows // s, s, w)
        return jnp.broadcast_to(blocks[:, s // 2 - 1:s // 2, :], blocks.shape).reshape(rows, w)
    groups = b.reshape(rows // SUBLANES, SUBLANES, w)
    sub = lax.broadcasted_iota(jnp.int32, groups.shape, 1)
    out = None
    for first in reversed(range(0, SUBLANES, s)):
        ref_row = jnp.broadcast_to(groups[:, first + s // 2 - 1:first + s // 2, :], groups.shape)
        out = ref_row if out is None else jnp.where(sub < first + s, ref_row, out)
    return out.reshape(rows, w)


def _mixer_ffn2_kernel(x1_ref, proj_ref, la_ref,
                       kmask_ref, dret_ref, gq_ref, gk_ref, aret_ref, ltri_ref, lmask_ref,
                       rn_ref, gn_ref, wout_ref, g2_ref, wg_ref, wu_ref, wd_ref, gf_ref,
                       out_ref, o_ref, act_ref, x2_ref, sr_ref, sg_ref, *, steps_per_seq):
    rows_per_step = x1_ref.shape[0]
    step = pl.program_id(0)

    @pl.when(step == 0)
    def _():
        x2_ref[...] = jnp.zeros_like(x2_ref)

    @pl.when(step % steps_per_seq == 0)
    def _():
        sr_ref[...] = jnp.zeros_like(sr_ref)
        sg_ref[...] = jnp.zeros_like(sg_ref)

    def scores(ql, kl):
        out = []
        for col in range(QK_W // LANE):
            lanes = slice(col * LANE, (col + 1) * LANE)
            w = jnp.concatenate([kl[:, lanes]] * 2, axis=0).T * kmask_ref[...]
            out.append(_dot(ql[:, lanes], w))
        return jnp.concatenate(out, axis=1)

    def attend(s, q_in, v, state):
        lhs = jnp.concatenate([s.astype(BF16), q_in], axis=1)
        rhs = jnp.concatenate([_value_block_diag(v), _state_block_diag(state)], axis=0)
        return _dot(lhs, rhs)

    def state_update(k_out, v):
        k_t = k_out.T
        return jnp.concatenate([_dot(k_t[h * HEAD_QK:(h + 1) * HEAD_QK, :], v[:, h * HEAD_V:(h + 1) * HEAD_V])
                                for h in range(N_HEADS)], axis=0)

    def mixer_stages():
        chunk_rows = [pl.ds(c * CHUNK, CHUNK) for c in range(rows_per_step // CHUNK)]

        q_r = [proj_ref[r, RQ:RQ + QK_W] for r in chunk_rows]
        k_r = [proj_ref[r, RK:RK + QK_W] for r in chunk_rows]
        v_r = [proj_ref[r, RV:RV + V_W] for r in chunk_rows]
        s_r = [scores(q, k) * dret_ref[...] for q, k in zip(q_r, k_r)]
        qin_r = [(q.astype(F32) * gq_ref[...]).astype(BF16) for q in q_r]
        yield
        kv_r = [state_update((k.astype(F32) * gk_ref[...]).astype(BF16), v) for k, v in zip(k_r, v_r)]
        yield

        b = []
        for r in chunk_rows:
            la = la_ref[r, :]
            la_hi = la.astype(BF16)
            la_lo = (la - la_hi.astype(F32)).astype(BF16)
            b.append(_dot(ltri_ref[...], jnp.concatenate([la_hi, la_lo], axis=0)))
        b_last = [jnp.broadcast_to(bc[CHUNK - 1:CHUNK, :], (2 * CHUNK, QK_W)) for bc in b]
        a_col = [jnp.exp(bl.T)[:, :LANE] for bl in b_last]
        yield
        q_g = [proj_ref[r, GQ:GQ + QK_W].astype(F32) for r in chunk_rows]
        k_g = [proj_ref[r, GK:GK + QK_W].astype(F32) for r in chunk_rows]
        v_g = [proj_ref[r, GV:GV + V_W] for r in chunk_rows]
        s_g = [scores(q.astype(BF16), k.astype(BF16)) * lmask_ref[0] for q, k in zip(q_g, k_g)]
        for l, s in enumerate(GLA_LEVELS):
            yield
            for c in range(len(chunk_rows)):
                f = jnp.exp(-jnp.abs(b[c] - _midpoint_rows(b[c], s)))
                s_g[c] = s_g[c] + (scores((q_g[c] * f).astype(BF16), (k_g[c] * f).astype(BF16))
                                   * lmask_ref[1 + l])
        yield
        qin_g = [(q * jnp.exp(bc)).astype(BF16) for q, bc in zip(q_g, b)]
        kv_g = [state_update((k * jnp.exp(bl[:CHUNK, :] - bc)).astype(BF16), v)
                for k, bc, bl, v in zip(k_g, b, b_last, v_g)]

        sr = sr_ref[...]
        sg = sg_ref[...]
        for c, r in enumerate(chunk_rows):
            if c % 4 == 0:
                yield
            o = attend(s_r[c], qin_r[c], v_r[c], sr)
            sr = sr * aret_ref[...] + kv_r[c]
            gate = proj_ref[r, RG:RG + V_W].astype(F32)
            o_ref[r, 0:V_W] = _head_norm_gate(o, rn_ref[...], gate).astype(BF16)
            o = attend(s_g[c], qin_g[c], v_g[c], sg)
            sg = sg * a_col[c] + kv_g[c]
            gate = proj_ref[r, GG:GG + V_W].astype(F32)
            o_ref[r, V_W:2 * V_W] = _head_norm_gate(o, gn_ref[...], gate).astype(BF16)
        sr_ref[...] = sr
        sg_ref[...] = sg

    stages = mixer_stages()
    next(stages)
    x2_prev = x2_ref[...]
    y = _ffn(_rms(x2_prev, g2_ref[...]).astype(BF16), wg_ref, wu_ref, wd_ref, act_ref,
             interleave=stages)
    out_ref[...] = _rms(x2_prev + 0.5 * y, gf_ref[...])
    x2_ref[...] = x1_ref[...] + _dot(o_ref[...], wout_ref[...])


def _gla_tables():
    t = np.arange(CHUNK)[:, None]
    u = np.arange(CHUNK)[None, :]
    masks = [np.eye(CHUNK, dtype=bool)]
    for s in GLA_LEVELS:
        same_block = (t // s) == (u // s)
        masks.append(same_block & (((t % s) < s // 2) != ((u % s) < s // 2)))
    ltri = (u <= t).astype(np.float32)
    ltri2 = np.concatenate([ltri, ltri], axis=1)
    lmask = np.stack([np.tile(m, (1, N_HEADS)) for m in masks]).astype(np.float32)
    return ltri2, lmask


def _retention_tables():
    gamma = 1.0 - 2.0 ** (-5.0 - np.arange(N_HEADS, dtype=np.float64))
    head_of_lane = np.arange(QK_W) // HEAD_QK
    i = np.arange(CHUNK)
    dist = np.abs(i[:, None] - i[None, :])
    dret = np.concatenate([gamma[h] ** dist for h in range(N_HEADS)], axis=1)
    gq = gamma[head_of_lane][None, :] ** (i[:, None] + 1.0)
    gk = gamma[head_of_lane][None, :] ** (CHUNK - 1.0 - i[:, None])
    aret = np.broadcast_to((gamma[head_of_lane] ** CHUNK)[:, None], (QK_W, HEAD_V))
    return tuple(np.asarray(a, np.float32) for a in (dret, gq, gk, aret))


def _key_mask():
    idx = np.arange(LANE) // HEAD_QK
    return idx[:, None] == idx[None, :]


def _const_spec(shape):
    nd = len(shape)
    return pl.BlockSpec(shape, lambda *_: (0,) * nd, pipeline_mode=pl.Buffered(1))


BF16_SUBLANES = 16


def _slab_spec(shape, n_steps):
    rows, cols = shape
    slab = next(s for s in range(BF16_SUBLANES, rows + 1, BF16_SUBLANES)
                if rows % s == 0 and n_steps % (rows // s) == 0 and rows // s <= n_steps)
    repeat = n_steps // (rows // slab)
    return pl.BlockSpec((slab, cols), lambda i: (i // repeat, 0))


def _tile_rows(total_rows):
    tile = 512
    assert total_rows % tile == 0
    return tile


@jax.jit
def kernel(x, ffn1_norm_g, ffn1_w_gate, ffn1_w_up, ffn1_w_down, mix_norm_g, w_in, ret_norm_g, gla_w_a2,
           gla_b_a, gla_norm_g, w_out, ffn2_norm_g, ffn2_w_gate, ffn2_w_up, ffn2_w_down, final_norm_g):
    bsz, seq, d = x.shape
    assert d == D_MODEL and seq % CHUNK == 0 and ffn1_norm_g.shape[0] == 1
    tile = _tile_rows(seq)
    steps = seq // tile
    n_rows = bsz * seq

    row = lambda g: g.reshape(1, -1).astype(F32)
    w_in_b = w_in[0].astype(BF16)
    w_low = jnp.pad(w_in_b[:, GLOW:], ((0, 0), (0, LANE - GATE_RANK)))
    w_a2 = jnp.pad(gla_w_a2[0], ((0, LANE - GATE_RANK), (0, 0))).astype(BF16)

    half = HEAD_QK // 2
    pos = jnp.arange(seq, dtype=F32)
    inv = ROPE_BASE ** (-jnp.arange(half, dtype=F32) * 2.0 / HEAD_QK)
    ang = pos[:, None] * inv[None, :]
    sign = np.where((np.arange(LANE) % HEAD_QK) < half, -1.0, 1.0).astype(np.float32)
    cos = jnp.tile(jnp.cos(ang), (1, LANE // half))
    sin = jnp.tile(jnp.sin(ang), (1, LANE // half)) * sign[None, :]

    x2d = x.reshape(n_rows, d)
    n_tiles = n_rows // tile
    row_spec = lambda w: pl.BlockSpec((tile, w), lambda i: (i, 0))
    pos_spec = pl.BlockSpec((tile, LANE), lambda i: (i % steps, 0))
    later = [w_out[0], ffn2_w_gate[0], ffn2_w_up[0], ffn2_w_down[0]]
    later_specs = [_slab_spec(w.shape, n_tiles) for w in later]
    x1, proj, la, w_out_b, wg2_b, wu2_b, wd2_b = pl.pallas_call(
        _ffn1_proj_kernel,
        grid=(n_tiles,),
        in_specs=[row_spec(d), pos_spec, pos_spec, _const_spec((1, d)),
                  _const_spec((d, D_FF)), _const_spec((d, D_FF)), _const_spec((D_FF, d)),
                  _const_spec((1, d)), _const_spec(w_in_b.shape),
                  _const_spec((d, LANE)), _const_spec((LANE, QK_W)), _const_spec((1, QK_W))] + later_specs,
        out_specs=[row_spec(d), row_spec(PROJ_W), row_spec(QK_W)] + later_specs,
        out_shape=[jax.ShapeDtypeStruct((n_rows, d), F32),
                   jax.ShapeDtypeStruct((n_rows, PROJ_W), BF16),
                   jax.ShapeDtypeStruct((n_rows, QK_W), F32)]
                  + [jax.ShapeDtypeStruct(w.shape, BF16) for w in later],
        scratch_shapes=[pltpu.VMEM((tile, D_FF), BF16)],
        compiler_params=pltpu.CompilerParams(dimension_semantics=("arbitrary",),
                                             vmem_limit_bytes=VMEM_LIMIT),
        name="ffn1_proj",
    )(x2d, cos, sin, row(ffn1_norm_g), ffn1_w_gate[0].astype(BF16), ffn1_w_up[0].astype(BF16),
      ffn1_w_down[0].astype(BF16), row(mix_norm_g), w_in_b, w_low, w_a2, row(gla_b_a), *later)

    ltri2, lmask = _gla_tables()
    dret, gq, gk, aret = _retention_tables()
    consts = [jnp.asarray(_key_mask(), BF16), jnp.asarray(dret), jnp.asarray(gq),
              jnp.asarray(gk), jnp.asarray(aret), jnp.asarray(ltri2, BF16), jnp.asarray(lmask)]

    tok_spec = lambda w: pl.BlockSpec((tile, w), lambda i: (jnp.minimum(i, n_tiles - 1), 0))
    out = pl.pallas_call(
        functools.partial(_mixer_ffn2_kernel, steps_per_seq=steps),
        grid=(n_tiles + 1,),
        in_specs=[tok_spec(d), tok_spec(PROJ_W), tok_spec(QK_W)]
                 + [_const_spec(c.shape) for c in consts]
                 + [_const_spec((1, V_W)), _const_spec((1, V_W)), _const_spec((d, d)), _const_spec((1, d)),
                    _const_spec((d, D_FF)), _const_spec((d, D_FF)), _const_spec((D_FF, d)),
                    _const_spec((1, d))],
        out_specs=pl.BlockSpec((tile, d), lambda i: (jnp.maximum(i - 1, 0), 0)),
        out_shape=jax.ShapeDtypeStruct((n_rows, d), F32),
        scratch_shapes=[pltpu.VMEM((tile, 2 * V_W), BF16), pltpu.VMEM((tile, D_FF), BF16),
                        pltpu.VMEM((tile, d), F32),
                        pltpu.VMEM((QK_W, HEAD_V), F32), pltpu.VMEM((QK_W, HEAD_V), F32)],
        compiler_params=pltpu.CompilerParams(dimension_semantics=("arbitrary",),
                                             vmem_limit_bytes=VMEM_LIMIT),
        name="mixer_ffn2",
    )(x1, proj, la, *consts, row(ret_norm_g), row(gla_norm_g), w_out_b, row(ffn2_norm_g),
      wg2_b, wu2_b, wd2_b, row(final_norm_g))
    return out.reshape(bsz, seq, d)
```

```python
import functools

import numpy as np
import jax
import jax.numpy as jnp
from jax import lax
from jax.experimental import pallas as pl
from jax.experimental.pallas import tpu as pltpu

D_MODEL = 1024
CHUNK = 64
RMS_EPS = 1e-6
ROPE_BASE = 10000.0
N_HEADS = 4
QK_W = 256
V_W = 512
HEAD_V = V_W // N_HEADS
HEAD_QK = QK_W // N_HEADS
GATE_RANK = 16
GATE_NORM = 16.0
D_FF = 2816
PROJ_W = 2 * (2 * QK_W + 2 * V_W)
RQ, RK, RV, RG = 0, 256, 512, 1024
GQ, GK, GV, GG = 1536, 1792, 2048, 2560
GLOW = 3072

LANE = 128
SUBLANES = 8
BF16_SUBLANES = 16
MXU_N = 256
VMEM_LIMIT = 56 * 1024 * 1024

GLA_LEVELS = (64, 32, 16, 8, 4, 2)

BF16 = jnp.bfloat16
F32 = jnp.float32


def _dot(a, b):
    return jnp.dot(a, b, preferred_element_type=F32)


def _rms(x, g):
    ms = jnp.mean(x * x, axis=-1, keepdims=True)
    return x * lax.rsqrt(ms + RMS_EPS) * g


def _silu(x):
    h = 0.5 * x
    return h * jnp.tanh(h) + h


def _ffn(h, wg_ref, wu_ref, wd_ref, act_ref, interleave=None):
    for c in range(D_FF // MXU_N):
        cols = slice(c * MXU_N, (c + 1) * MXU_N)
        gate = _dot(h, wg_ref[:, cols])
        up = _dot(h, wu_ref[:, cols])
        act_ref[:, cols] = (_silu(gate) * up).astype(BF16)
        if interleave is not None:
            next(interleave, None)
    if interleave is not None:
        for _ in interleave:
            pass
    return _dot(act_ref[...], wd_ref[...])


def _ffn1_proj_kernel(x_ref, cos_ref, sin_ref, g1_ref, wg_ref, wu_ref, wd_ref, gm_ref, win_ref,
                      wlow_ref, wa2_ref, ba_ref, *rest):
    later_f32, (x1_ref, proj_ref, la_ref), later_bf16, act_ref = rest[:4], rest[4:7], rest[7:11], rest[11]

    def cast_later_weights():
        yield
        for src, dst in zip(later_f32, later_bf16):
            dst[...] = src[...].astype(BF16)

    def half_tile(rows):
        x = x_ref[rows, :]
        h = _rms(x, g1_ref[...]).astype(BF16)
        yield
        for c in range(D_FF // MXU_N):
            cols = slice(c * MXU_N, (c + 1) * MXU_N)
            gate = _dot(h, wg_ref[:, cols])
            up = _dot(h, wu_ref[:, cols])
            act_ref[rows, cols] = (_silu(gate) * up).astype(BF16)
        yield
        x1 = x + 0.5 * _dot(act_ref[rows, :], wd_ref[...])
        x1_ref[rows, :] = x1
        h = _rms(x1, gm_ref[...]).astype(BF16)
        yield
        cos = cos_ref[rows, :]
        sin = sin_ref[rows, :]
        first_half = (lax.broadcasted_iota(jnp.int32, cos.shape, 1) % HEAD_QK) < HEAD_QK // 2
        for off, scale in ((RQ, 1.0), (RK, HEAD_QK ** -0.5)):
            t = _dot(h, win_ref[:, off:off + QK_W])
            for col in range(QK_W // LANE):
                tc = t[:, col * LANE:(col + 1) * LANE]
                partner = jnp.where(first_half, pltpu.roll(tc, LANE - HEAD_QK // 2, 1),
                                    pltpu.roll(tc, HEAD_QK // 2, 1))
                proj_ref[rows, off + col * LANE:off + (col + 1) * LANE] = (
                    (tc * cos + partner * sin) * scale).astype(BF16)
        proj_ref[rows, RV:GQ] = _dot(h, win_ref[:, RV:GQ]).astype(BF16)
        proj_ref[rows, GQ:GK] = (_dot(h, win_ref[:, GQ:GK]) * (HEAD_QK ** -0.5)).astype(BF16)
        proj_ref[rows, GK:PROJ_W] = _dot(h, win_ref[:, GK:PROJ_W]).astype(BF16)
        low = _dot(h, wlow_ref[...]).astype(BF16)
        yield
        z = _dot(low, wa2_ref[...]) + ba_ref[...]
        la_ref[rows, :] = (jnp.minimum(z, 0.0) - jnp.log(1.0 + jnp.exp(-jnp.abs(z)))) * (1.0 / GATE_NORM)

    n_half = x_ref.shape[0] // 2
    streams = [half_tile(pl.ds(i * n_half, n_half)) for i in range(2)] + [cast_later_weights()]
    while streams:
        streams = [g for g in streams if next(g, StopIteration) is not StopIteration]


def _head_norm_gate(o, norm_g, gate):
    outs = []
    for h in range(N_HEADS):
        cols = slice(h * HEAD_V, (h + 1) * HEAD_V)
        oh = o[:, cols]
        ms = jnp.mean(oh * oh, axis=-1, keepdims=True)
        outs.append(oh * lax.rsqrt(ms + RMS_EPS) * norm_g[:, cols] * _silu(gate[:, cols]))
    return jnp.concatenate(outs, axis=-1)


def _head_block_diag(blocks):
    zero = jnp.zeros_like(blocks[0])
    return jnp.concatenate(
        [jnp.concatenate([blk if j == h else zero for j in range(N_HEADS)], axis=1)
         for h, blk in enumerate(blocks)], axis=0)


def _state_block_diag(state):
    return _head_block_diag([state[h * HEAD_QK:(h + 1) * HEAD_QK, :].astype(BF16) for h in range(N_HEADS)])


def _value_block_diag(v):
    return _head_block_diag([v[:, h * HEAD_V:(h + 1) * HEAD_V] for h in range(N_HEADS)])


def _midpoint_rows(b, s):
    rows, w = b.shape
    if s >= SUBLANES:
        blocks = b.reshape(rows // s, s, w)
        return jnp.broadcast_to(blocks[:, s // 2 - 1:s // 2, :], blocks.shape).reshape(rows, w)
    groups = b.reshape(rows // SUBLANES, SUBLANES, w)
    sub = lax.broadcasted_iota(jnp.int32, groups.shape, 1)
    out = None
    for first in reversed(range(0, SUBLANES, s)):
        ref_row = jnp.broadcast_to(groups[:, first + s // 2 - 1:first + s // 2, :], groups.shape)
        out = ref_row if out is None else jnp.where(sub < first + s, ref_row, out)
    return out.reshape(rows, w)


def _mixer_ffn2_kernel(x1_ref, proj_ref, la_ref,
                       kmask_ref, dret_ref, gq_ref, gk_ref, aret_ref, ltri_ref, lmask_ref,
                       rn_ref, gn_ref, wout_ref, g2_ref, wg_ref, wu_ref, wd_ref, gf_ref,
                       out_ref, o_ref, act_ref, x2_ref, sr_ref, sg_ref, *, steps_per_seq):
    rows_per_step = x1_ref.shape[0]
    step = pl.program_id(0)

    @pl.when(step % steps_per_seq == 0)
    def _():
        sr_ref[...] = jnp.zeros_like(sr_ref)
        sg_ref[...] = jnp.zeros_like(sg_ref)

    def scores(ql, kl):
        out = []
        for col in range(QK_W // LANE):
            lanes = slice(col * LANE, (col + 1) * LANE)
            w = jnp.concatenate([kl[:, lanes]] * 2, axis=0).T * kmask_ref[...]
            out.append(_dot(ql[:, lanes], w))
        return jnp.concatenate(out, axis=1)

    def attend(s, q_in, v, state):
        lhs = jnp.concatenate([s.astype(BF16), q_in], axis=1)
        rhs = jnp.concatenate([_value_block_diag(v), _state_block_diag(state)], axis=0)
        return _dot(lhs, rhs)

    def state_update(k_out, v):
        k_t = k_out.T
        return jnp.concatenate([_dot(k_t[h * HEAD_QK:(h + 1) * HEAD_QK, :], v[:, h * HEAD_V:(h + 1) * HEAD_V])
                                for h in range(N_HEADS)], axis=0)

    def mixer_stages():
        chunk_rows = [pl.ds(c * CHUNK, CHUNK) for c in range(rows_per_step // CHUNK)]

        q_r = [proj_ref[r, RQ:RQ + QK_W] for r in chunk_rows]
        k_r = [proj_ref[r, RK:RK + QK_W] for r in chunk_rows]
        v_r = [proj_ref[r, RV:RV + V_W] for r in chunk_rows]
        s_r = [scores(q, k) * dret_ref[...] for q, k in zip(q_r, k_r)]
        qin_r = [(q.astype(F32) * gq_ref[...]).astype(BF16) for q in q_r]
        yield
        kv_r = [state_update((k.astype(F32) * gk_ref[...]).astype(BF16), v) for k, v in zip(k_r, v_r)]
        yield

        b = []
        for r in chunk_rows:
            la = la_ref[r, :]
            la_hi = la.astype(BF16)
            la_lo = (la - la_hi.astype(F32)).astype(BF16)
            b.append(_dot(ltri_ref[...], jnp.concatenate([la_hi, la_lo], axis=0)))
        b_last = [jnp.broadcast_to(bc[CHUNK - 1:CHUNK, :], (2 * CHUNK, QK_W)) for bc in b]
        a_col = [jnp.exp(bl.T)[:, :LANE] for bl in b_last]
        yield
        q_g = [proj_ref[r, GQ:GQ + QK_W].astype(F32) for r in chunk_rows]
        k_g = [proj_ref[r, GK:GK + QK_W].astype(F32) for r in chunk_rows]
        v_g = [proj_ref[r, GV:GV + V_W] for r in chunk_rows]
        s_g = [scores(q.astype(BF16), k.astype(BF16)) * lmask_ref[0] for q, k in zip(q_g, k_g)]
        for l, s in enumerate(GLA_LEVELS):
            yield
            for c in range(len(chunk_rows)):
                f = jnp.exp(-jnp.abs(b[c] - _midpoint_rows(b[c], s)))
                s_g[c] = s_g[c] + (scores((q_g[c] * f).astype(BF16), (k_g[c] * f).astype(BF16))
                                   * lmask_ref[1 + l])
        yield
        qin_g = [(q * jnp.exp(bc)).astype(BF16) for q, bc in zip(q_g, b)]
        kv_g = [state_update((k * jnp.exp(bl[:CHUNK, :] - bc)).astype(BF16), v)
                for k, bc, bl, v in zip(k_g, b, b_last, v_g)]

        sr = sr_ref[...]
        sg = sg_ref[...]
        for c, r in enumerate(chunk_rows):
            if c % 4 == 0:
                yield
            o = attend(s_r[c], qin_r[c], v_r[c], sr)
            sr = sr * aret_ref[...] + kv_r[c]
            gate = proj_ref[r, RG:RG + V_W].astype(F32)
            o_ref[r, 0:V_W] = _head_norm_gate(o, rn_ref[...], gate).astype(BF16)
            o = attend(s_g[c], qin_g[c], v_g[c], sg)
            sg = sg * a_col[c] + kv_g[c]
            gate = proj_ref[r, GG:GG + V_W].astype(F32)
            o_ref[r, V_W:2 * V_W] = _head_norm_gate(o, gn_ref[...], gate).astype(BF16)
        sr_ref[...] = sr
        sg_ref[...] = sg

    def run(mix, finish_previous):
        stages = mixer_stages() if mix else iter(())
        next(stages, None)
        if finish_previous:
            x2_prev = x2_ref[...]
            y = _ffn(_rms(x2_prev, g2_ref[...]).astype(BF16), wg_ref, wu_ref, wd_ref, act_ref,
                     interleave=stages)
            out_ref[...] = _rms(x2_prev + 0.5 * y, gf_ref[...])
        for _ in stages:
            pass
        if mix:
            x2_ref[...] = x1_ref[...] + _dot(o_ref[...], wout_ref[...])

    last = pl.num_programs(0) - 1
    pl.when(step == 0)(functools.partial(run, True, False))
    pl.when(jnp.logical_and(step > 0, step < last))(functools.partial(run, True, True))
    pl.when(step == last)(functools.partial(run, False, True))


def _gla_tables():
    t = np.arange(CHUNK)[:, None]
    u = np.arange(CHUNK)[None, :]
    masks = [np.eye(CHUNK, dtype=bool)]
    for s in GLA_LEVELS:
        same_block = (t // s) == (u // s)
        masks.append(same_block & (((t % s) < s // 2) != ((u % s) < s // 2)))
    ltri = (u <= t).astype(np.float32)
    ltri2 = np.concatenate([ltri, ltri], axis=1)
    lmask = np.stack([np.tile(m, (1, N_HEADS)) for m in masks]).astype(np.float32)
    return ltri2, lmask


def _retention_tables():
    gamma = 1.0 - 2.0 ** (-5.0 - np.arange(N_HEADS, dtype=np.float64))
    head_of_lane = np.arange(QK_W) // HEAD_QK
    i = np.arange(CHUNK)
    dist = np.abs(i[:, None] - i[None, :])
    dret = np.concatenate([gamma[h] ** dist for h in range(N_HEADS)], axis=1)
    gq = gamma[head_of_lane][None, :] ** (i[:, None] + 1.0)
    gk = gamma[head_of_lane][None, :] ** (CHUNK - 1.0 - i[:, None])
    aret = np.broadcast_to((gamma[head_of_lane] ** CHUNK)[:, None], (QK_W, HEAD_V))
    return tuple(np.asarray(a, np.float32) for a in (dret, gq, gk, aret))


def _key_mask():
    idx = np.arange(LANE) // HEAD_QK
    return idx[:, None] == idx[None, :]


def _const_spec(shape):
    nd = len(shape)
    return pl.BlockSpec(shape, lambda *_: (0,) * nd, pipeline_mode=pl.Buffered(1))


def _slab_spec(shape, n_steps):
    rows, cols = shape
    slab = next(s for s in range(BF16_SUBLANES, rows + 1, BF16_SUBLANES)
                if rows % s == 0 and n_steps % (rows // s) == 0 and rows // s <= n_steps)
    repeat = n_steps // (rows // slab)
    return pl.BlockSpec((slab, cols), lambda i: (i // repeat, 0))


def _tile_rows(total_rows):
    tile = 512
    assert total_rows % tile == 0
    return tile


@jax.jit
def kernel(x, ffn1_norm_g, ffn1_w_gate, ffn1_w_up, ffn1_w_down, mix_norm_g, w_in, ret_norm_g, gla_w_a2,
           gla_b_a, gla_norm_g, w_out, ffn2_norm_g, ffn2_w_gate, ffn2_w_up, ffn2_w_down, final_norm_g):
    bsz, seq, d = x.shape
    assert d == D_MODEL and seq % CHUNK == 0 and ffn1_norm_g.shape[0] == 1
    tile = _tile_rows(seq)
    steps = seq // tile
    n_rows = bsz * seq

    row = lambda g: g.reshape(1, -1).astype(F32)
    w_in_b = w_in[0].astype(BF16)
    w_low = jnp.pad(w_in_b[:, GLOW:], ((0, 0), (0, LANE - GATE_RANK)))
    w_a2 = jnp.pad(gla_w_a2[0], ((0, LANE - GATE_RANK), (0, 0))).astype(BF16)

    half = HEAD_QK // 2
    pos = jnp.arange(seq, dtype=F32)
    inv = ROPE_BASE ** (-jnp.arange(half, dtype=F32) * 2.0 / HEAD_QK)
    ang = pos[:, None] * inv[None, :]
    sign = np.where((np.arange(LANE) % HEAD_QK) < half, -1.0, 1.0).astype(np.float32)
    cos = jnp.tile(jnp.cos(ang), (1, LANE // half))
    sin = jnp.tile(jnp.sin(ang), (1, LANE // half)) * sign[None, :]

    x2d = x.reshape(n_rows, d)
    n_tiles = n_rows // tile
    row_spec = lambda w: pl.BlockSpec((tile, w), lambda i: (i, 0))
    pos_spec = pl.BlockSpec((tile, LANE), lambda i: (i % steps, 0))
    later = [w_out[0], ffn2_w_gate[0], ffn2_w_up[0], ffn2_w_down[0]]
    later_specs = [_slab_spec(w.shape, n_tiles) for w in later]
    x1, proj, la, w_out_b, wg2_b, wu2_b, wd2_b = pl.pallas_call(
        _ffn1_proj_kernel,
        grid=(n_tiles,),
        in_specs=[row_spec(d), pos_spec, pos_spec, _const_spec((1, d)),
                  _const_spec((d, D_FF)), _const_spec((d, D_FF)), _const_spec((D_FF, d)),
                  _const_spec((1, d)), _const_spec(w_in_b.shape),
                  _const_spec((d, LANE)), _const_spec((LANE, QK_W)), _const_spec((1, QK_W))] + later_specs,
        out_specs=[row_spec(d), row_spec(PROJ_W), row_spec(QK_W)] + later_specs,
        out_shape=[jax.ShapeDtypeStruct((n_rows, d), F32),
                   jax.ShapeDtypeStruct((n_rows, PROJ_W), BF16),
                   jax.ShapeDtypeStruct((n_rows, QK_W), F32)]
                  + [jax.ShapeDtypeStruct(w.shape, BF16) for w in later],
        scratch_shapes=[pltpu.VMEM((tile, D_FF), BF16)],
        compiler_params=pltpu.CompilerParams(dimension_semantics=("arbitrary",),
                                             vmem_limit_bytes=VMEM_LIMIT),
        name="ffn1_proj",
    )(x2d, cos, sin, row(ffn1_norm_g), ffn1_w_gate[0].astype(BF16), ffn1_w_up[0].astype(BF16),
      ffn1_w_down[0].astype(BF16), row(mix_norm_g), w_in_b, w_low, w_a2, row(gla_b_a), *later)

    ltri2, lmask = _gla_tables()
    dret, gq, gk, aret = _retention_tables()
    consts = [jnp.asarray(_key_mask(), BF16), jnp.asarray(dret), jnp.asarray(gq),
              jnp.asarray(gk), jnp.asarray(aret), jnp.asarray(ltri2, BF16), jnp.asarray(lmask)]

    tok_spec = lambda w: pl.BlockSpec((tile, w), lambda i: (jnp.minimum(i, n_tiles - 1), 0))
    out = pl.pallas_call(
        functools.partial(_mixer_ffn2_kernel, steps_per_seq=steps),
        grid=(n_tiles + 1,),
        in_specs=[tok_spec(d), tok_spec(PROJ_W), tok_spec(QK_W)]
                 + [_const_spec(c.shape) for c in consts]
                 + [_const_spec((1, V_W)), _const_spec((1, V_W)), _const_spec((d, d)), _const_spec((1, d)),
                    _const_spec((d, D_FF)), _const_spec((d, D_FF)), _const_spec((D_FF, d)),
                    _const_spec((1, d))],
        out_specs=pl.BlockSpec((tile, d), lambda i: (jnp.maximum(i - 1, 0), 0)),
        out_shape=jax.ShapeDtypeStruct((n_rows, d), F32),
        scratch_shapes=[pltpu.VMEM((tile, 2 * V_W), BF16), pltpu.VMEM((tile, D_FF), BF16),
                        pltpu.VMEM((tile, d), F32),
                        pltpu.VMEM((QK_W, HEAD_V), F32), pltpu.VMEM((QK_W, HEAD_V), F32)],
        compiler_params=pltpu.CompilerParams(dimension_semantics=("arbitrary",),
                                             vmem_limit_bytes=VMEM_LIMIT),
        name="mixer_ffn2",
    )(x1, proj, la, *consts, row(ret_norm_g), row(gla_norm_g), w_out_b, row(ffn2_norm_g),
      wg2_b, wu2_b, wd2_b, row(final_norm_g))
    return out.reshape(bsz, seq, d)
```

```python
import functools

import numpy as np
import jax
import jax.numpy as jnp
from jax import lax
from jax.experimental import pallas as pl
from jax.experimental.pallas import tpu as pltpu

D_MODEL = 1024
CHUNK = 64
RMS_EPS = 1e-6
ROPE_BASE = 10000.0
N_HEADS = 4
QK_W = 256
V_W = 512
HEAD_V = V_W // N_HEADS
HEAD_QK = QK_W // N_HEADS
GATE_RANK = 16
GATE_NORM = 16.0
D_FF = 2816
PROJ_W = 2 * (2 * QK_W + 2 * V_W)
RQ, RK, RV, RG = 0, 256, 512, 1024
GQ, GK, GV, GG = 1536, 1792, 2048, 2560
GLOW = 3072

LANE = 128
SUBLANES = 8
BF16_SUBLANES = 16
MXU_N = 256
VMEM_LIMIT = 56 * 1024 * 1024

GLA_LEVELS = (64, 32, 16, 8, 4, 2)

BF16 = jnp.bfloat16
F32 = jnp.float32


def _dot(a, b):
    return jnp.dot(a, b, preferred_element_type=F32)


def _rms(x, g):
    ms = jnp.mean(x * x, axis=-1, keepdims=True)
    return x * lax.rsqrt(ms + RMS_EPS) * g


def _silu(x):
    h = 0.5 * x
    return h * jnp.tanh(h) + h


def _ffn(h, wg_ref, wu_ref, wd_ref, act_ref, interleave=None):
    for c in range(D_FF // MXU_N):
        cols = slice(c * MXU_N, (c + 1) * MXU_N)
        gate = _dot(h, wg_ref[:, cols])
        up = _dot(h, wu_ref[:, cols])
        act_ref[:, cols] = (_silu(gate) * up).astype(BF16)
        if interleave is not None:
            next(interleave, None)
    if interleave is not None:
        for _ in interleave:
            pass
    return _dot(act_ref[...], wd_ref[...])


def _ffn1_proj_kernel(x_ref, cos_ref, sin_ref, g1_ref, wg_ref, wu_ref, wd_ref, gm_ref, win_ref,
                      wlow_ref, wa2_ref, ba_ref, *rest):
    later_f32, (x1_ref, proj_ref, la_ref), later_bf16, act_ref = rest[:4], rest[4:7], rest[7:11], rest[11]

    def cast_later_weights():
        yield
        for src, dst in zip(later_f32, later_bf16):
            dst[...] = src[...].astype(BF16)

    def half_tile(rows):
        x = x_ref[rows, :]
        h = _rms(x, g1_ref[...]).astype(BF16)
        yield
        for c in range(D_FF // MXU_N):
            cols = slice(c * MXU_N, (c + 1) * MXU_N)
            gate = _dot(h, wg_ref[:, cols])
            up = _dot(h, wu_ref[:, cols])
            act_ref[rows, cols] = (_silu(gate) * up).astype(BF16)
        yield
        x1 = x + 0.5 * _dot(act_ref[rows, :], wd_ref[...])
        x1_ref[rows, :] = x1
        h = _rms(x1, gm_ref[...]).astype(BF16)
        yield
        cos = cos_ref[rows, :]
        sin = sin_ref[rows, :]
        first_half = (lax.broadcasted_iota(jnp.int32, cos.shape, 1) % HEAD_QK) < HEAD_QK // 2
        for off, scale in ((RQ, 1.0), (RK, HEAD_QK ** -0.5)):
            t = _dot(h, win_ref[:, off:off + QK_W])
            for col in range(QK_W // LANE):
                tc = t[:, col * LANE:(col + 1) * LANE]
                partner = jnp.where(first_half, pltpu.roll(tc, LANE - HEAD_QK // 2, 1),
                                    pltpu.roll(tc, HEAD_QK // 2, 1))
                proj_ref[rows, off + col * LANE:off + (col + 1) * LANE] = (
                    (tc * cos + partner * sin) * scale).astype(BF16)
        proj_ref[rows, RV:GQ] = _dot(h, win_ref[:, RV:GQ]).astype(BF16)
        proj_ref[rows, GQ:GK] = (_dot(h, win_ref[:, GQ:GK]) * (HEAD_QK ** -0.5)).astype(BF16)
        proj_ref[rows, GK:PROJ_W] = _dot(h, win_ref[:, GK:PROJ_W]).astype(BF16)
        low = _dot(h, wlow_ref[...]).astype(BF16)
        yield
        z = _dot(low, wa2_ref[...]) + ba_ref[...]
        la_ref[rows, :] = (jnp.minimum(z, 0.0) - jnp.log(1.0 + jnp.exp(-jnp.abs(z)))) * (1.0 / GATE_NORM)

    n_half = x_ref.shape[0] // 2
    streams = [half_tile(pl.ds(i * n_half, n_half)) for i in range(2)] + [cast_later_weights()]
    while streams:
        streams = [g for g in streams if next(g, StopIteration) is not StopIteration]


def _head_norm_gate(o, norm_g, gate):
    outs = []
    for h in range(N_HEADS):
        cols = slice(h * HEAD_V, (h + 1) * HEAD_V)
        oh = o[:, cols]
        ms = jnp.mean(oh * oh, axis=-1, keepdims=True)
        outs.append(oh * lax.rsqrt(ms + RMS_EPS) * norm_g[:, cols] * _silu(gate[:, cols]))
    return jnp.concatenate(outs, axis=-1)


def _head_block_diag(blocks):
    zero = jnp.zeros_like(blocks[0])
    return jnp.concatenate(
        [jnp.concatenate([blk if j == h else zero for j in range(N_HEADS)], axis=1)
         for h, blk in enumerate(blocks)], axis=0)


def _state_block_diag(state):
    return _head_block_diag([state[h * HEAD_QK:(h + 1) * HEAD_QK, :].astype(BF16) for h in range(N_HEADS)])


def _value_block_diag(v):
    return _head_block_diag([v[:, h * HEAD_V:(h + 1) * HEAD_V] for h in range(N_HEADS)])


def _midpoint_rows(b, s):
    rows, w = b.shape
    if s >= SUBLANES:
        blocks = b.reshape(rows // s, s, w)
        return jnp.broadcast_to(blocks[:, s // 2 - 1:s // 2, :], blocks.shape).reshape(rows, w)
    groups = b.reshape(rows // SUBLANES, SUBLANES, w)
    sub = lax.broadcasted_iota(jnp.int32, groups.shape, 1)
    out = None
    for first in reversed(range(0, SUBLANES, s)):
        ref_row = jnp.broadcast_to(groups[:, first + s // 2 - 1:first + s // 2, :], groups.shape)
        out = ref_row if out is None else jnp.where(sub < first + s, ref_row, out)
    return out.reshape(rows, w)


def _mixer_ffn2_kernel(x1_ref, proj_ref, la_ref,
                       kmask_ref, dret_ref, gq_ref, gk_ref, aret_ref, ltri_ref, lmask_ref,
                       rn_ref, gn_ref, wout_ref, g2_ref, wg_ref, wu_ref, wd_ref, gf_ref,
                       out_ref, o_ref, act_ref, x2_ref, sr_ref, sg_ref, *, steps_per_seq):
    rows_per_step = x1_ref.shape[0]
    step = pl.program_id(0)

    @pl.when(step == 0)
    def _():
        x2_ref[...] = jnp.zeros_like(x2_ref)

    @pl.when(step % steps_per_seq == 0)
    def _():
        sr_ref[...] = jnp.zeros_like(sr_ref)
        sg_ref[...] = jnp.zeros_like(sg_ref)

    def scores(ql, kl):
        out = []
        for col in range(QK_W // LANE):
            lanes = slice(col * LANE, (col + 1) * LANE)
            w = jnp.concatenate([kl[:, lanes]] * 2, axis=0).T * kmask_ref[...]
            out.append(_dot(ql[:, lanes], w))
        return jnp.concatenate(out, axis=1)

    def attend(s, q_in, v, state):
        lhs = jnp.concatenate([s.astype(BF16), q_in], axis=1)
        rhs = jnp.concatenate([_value_block_diag(v), _state_block_diag(state)], axis=0)
        return _dot(lhs, rhs)

    def state_update(k_out, v):
        k_t = k_out.T
        return jnp.concatenate([_dot(k_t[h * HEAD_QK:(h + 1) * HEAD_QK, :], v[:, h * HEAD_V:(h + 1) * HEAD_V])
                                for h in range(N_HEADS)], axis=0)

    def mixer_stages():
        chunk_rows = [pl.ds(c * CHUNK, CHUNK) for c in range(rows_per_step // CHUNK)]

        q_r = [proj_ref[r, RQ:RQ + QK_W] for r in chunk_rows]
        k_r = [proj_ref[r, RK:RK + QK_W] for r in chunk_rows]
        v_r = [proj_ref[r, RV:RV + V_W] for r in chunk_rows]
        s_r = [scores(q, k) * dret_ref[...] for q, k in zip(q_r, k_r)]
        qin_r = [(q.astype(F32) * gq_ref[...]).astype(BF16) for q in q_r]
        yield
        kv_r = [state_update((k.astype(F32) * gk_ref[...]).astype(BF16), v) for k, v in zip(k_r, v_r)]
        yield

        b = []
        for r in chunk_rows:
            la = la_ref[r, :]
            la_hi = la.astype(BF16)
            la_lo = (la - la_hi.astype(F32)).astype(BF16)
            b.append(_dot(ltri_ref[...], jnp.concatenate([la_hi, la_lo], axis=0)))
        b_last = [jnp.broadcast_to(bc[CHUNK - 1:CHUNK, :], (2 * CHUNK, QK_W)) for bc in b]
        a_col = [jnp.exp(bl.T)[:, :LANE] for bl in b_last]
        yield
        q_g = [proj_ref[r, GQ:GQ + QK_W].astype(F32) for r in chunk_rows]
        k_g = [proj_ref[r, GK:GK + QK_W].astype(F32) for r in chunk_rows]
        v_g = [proj_ref[r, GV:GV + V_W] for r in chunk_rows]
        s_g = [scores(q.astype(BF16), k.astype(BF16)) * lmask_ref[0] for q, k in zip(q_g, k_g)]
        for l, s in enumerate(GLA_LEVELS):
            yield
            for c in range(len(chunk_rows)):
                f = jnp.exp(-jnp.abs(b[c] - _midpoint_rows(b[c], s)))
                s_g[c] = s_g[c] + (scores((q_g[c] * f).astype(BF16), (k_g[c] * f).astype(BF16))
                                   * lmask_ref[1 + l])
        yield
        qin_g = [(q * jnp.exp(bc)).astype(BF16) for q, bc in zip(q_g, b)]
        kv_g = [state_update((k * jnp.exp(bl[:CHUNK, :] - bc)).astype(BF16), v)
                for k, bc, bl, v in zip(k_g, b, b_last, v_g)]

        sr = sr_ref[...]
        sg = sg_ref[...]
        for c, r in enumerate(chunk_rows):
            if c % 4 == 0:
                yield
            o = attend(s_r[c], qin_r[c], v_r[c], sr)
            sr = sr * aret_ref[...] + kv_r[c]
            gate = proj_ref[r, RG:RG + V_W].astype(F32)
            o_ref[r, 0:V_W] = _head_norm_gate(o, rn_ref[...], gate).astype(BF16)
            o = attend(s_g[c], qin_g[c], v_g[c], sg)
            sg = sg * a_col[c] + kv_g[c]
            gate = proj_ref[r, GG:GG + V_W].astype(F32)
            o_ref[r, V_W:2 * V_W] = _head_norm_gate(o, gn_ref[...], gate).astype(BF16)
        sr_ref[...] = sr
        sg_ref[...] = sg

    stages = mixer_stages()
    next(stages)
    x2_prev = x2_ref[...]
    y = _ffn(_rms(x2_prev, g2_ref[...]).astype(BF16), wg_ref, wu_ref, wd_ref, act_ref,
             interleave=stages)
    out_ref[...] = _rms(x2_prev + 0.5 * y, gf_ref[...])
    x2_ref[...] = x1_ref[...] + _dot(o_ref[...], wout_ref[...])


def _gla_tables():
    t = np.arange(CHUNK)[:, None]
    u = np.arange(CHUNK)[None, :]
    masks = [np.eye(CHUNK, dtype=bool)]
    for s in GLA_LEVELS:
        same_block = (t // s) == (u // s)
        masks.append(same_block & (((t % s) < s // 2) != ((u % s) < s // 2)))
    ltri = (u <= t).astype(np.float32)
    ltri2 = np.concatenate([ltri, ltri], axis=1)
    lmask = np.stack([np.tile(m, (1, N_HEADS)) for m in masks]).astype(np.float32)
    return ltri2, lmask


def _retention_tables():
    gamma = 1.0 - 2.0 ** (-5.0 - np.arange(N_HEADS, dtype=np.float64))
    head_of_lane = np.arange(QK_W) // HEAD_QK
    i = np.arange(CHUNK)
    dist = np.abs(i[:, None] - i[None, :])
    dret = np.concatenate([gamma[h] ** dist for h in range(N_HEADS)], axis=1)
    gq = gamma[head_of_lane][None, :] ** (i[:, None] + 1.0)
    gk = gamma[head_of_lane][None, :] ** (CHUNK - 1.0 - i[:, None])
    aret = np.broadcast_to((gamma[head_of_lane] ** CHUNK)[:, None], (QK_W, HEAD_V))
    return tuple(np.asarray(a, np.float32) for a in (dret, gq, gk, aret))


def _key_mask():
    idx = np.arange(LANE) // HEAD_QK
    return idx[:, None] == idx[None, :]


def _const_spec(shape):
    nd = len(shape)
    return pl.BlockSpec(shape, lambda *_: (0,) * nd, pipeline_mode=pl.Buffered(1))


def _slab_spec(shape, n_steps):
    rows, cols = shape
    slab = next(s for s in range(BF16_SUBLANES, rows + 1, BF16_SUBLANES)
                if rows % s == 0 and n_steps % (rows // s) == 0 and rows // s <= n_steps)
    repeat = n_steps // (rows // slab)
    return pl.BlockSpec((slab, cols), lambda i: (i // repeat, 0))


def _tile_rows(total_rows):
    tile = 512
    assert total_rows % tile == 0
    return tile


@jax.jit
def kernel(x, ffn1_norm_g, ffn1_w_gate, ffn1_w_up, ffn1_w_down, mix_norm_g, w_in, ret_norm_g, gla_w_a2,
           gla_b_a, gla_norm_g, w_out, ffn2_norm_g, ffn2_w_gate, ffn2_w_up, ffn2_w_down, final_norm_g):
    bsz, seq, d = x.shape
    assert d == D_MODEL and seq % CHUNK == 0 and ffn1_norm_g.shape[0] == 1
    tile = _tile_rows(seq)
    steps = seq // tile
    n_rows = bsz * seq

    row = lambda g: g.reshape(1, -1).astype(F32)
    w_in_b = w_in[0].astype(BF16)
    w_low = jnp.pad(w_in_b[:, GLOW:], ((0, 0), (0, LANE - GATE_RANK)))
    w_a2 = jnp.pad(gla_w_a2[0], ((0, LANE - GATE_RANK), (0, 0))).astype(BF16)

    half = HEAD_QK // 2
    pos = jnp.arange(seq, dtype=F32)
    inv = ROPE_BASE ** (-jnp.arange(half, dtype=F32) * 2.0 / HEAD_QK)
    ang = pos[:, None] * inv[None, :]
    sign = np.where((np.arange(LANE) % HEAD_QK) < half, -1.0, 1.0).astype(np.float32)
    cos = jnp.tile(jnp.cos(ang), (1, LANE // half))
    sin = jnp.tile(jnp.sin(ang), (1, LANE // half)) * sign[None, :]

    x2d = x.reshape(n_rows, d)
    n_tiles = n_rows // tile
    row_spec = lambda w: pl.BlockSpec((tile, w), lambda i: (i, 0))
    pos_spec = pl.BlockSpec((tile, LANE), lambda i: (i % steps, 0))
    later = [w_out[0], ffn2_w_gate[0], ffn2_w_up[0], ffn2_w_down[0]]
    later_specs = [_slab_spec(w.shape, n_tiles) for w in later]
    x1, proj, la, w_out_b, wg2_b, wu2_b, wd2_b = pl.pallas_call(
        _ffn1_proj_kernel,
        grid=(n_tiles,),
        in_specs=[row_spec(d), pos_spec, pos_spec, _const_spec((1, d)),
                  _const_spec((d, D_FF)), _const_spec((d, D_FF)), _const_spec((D_FF, d)),
                  _const_spec((1, d)), _const_spec(w_in_b.shape),
                  _const_spec((d, LANE)), _const_spec((LANE, QK_W)), _const_spec((1, QK_W))] + later_specs,
        out_specs=[row_spec(d), row_spec(PROJ_W), row_spec(QK_W)] + later_specs,
        out_shape=[jax.ShapeDtypeStruct((n_rows, d), F32),
                   jax.ShapeDtypeStruct((n_rows, PROJ_W), BF16),
                   jax.ShapeDtypeStruct((n_rows, QK_W), F32)]
                  + [jax.ShapeDtypeStruct(w.shape, BF16) for w in later],
        scratch_shapes=[pltpu.VMEM((tile, D_FF), BF16)],
        compiler_params=pltpu.CompilerParams(dimension_semantics=("arbitrary",),
                                             vmem_limit_bytes=VMEM_LIMIT),
        name="ffn1_proj",
    )(x2d, cos, sin, row(ffn1_norm_g), ffn1_w_gate[0].astype(BF16), ffn1_w_up[0].astype(BF16),
      ffn1_w_down[0].astype(BF16), row(mix_norm_g), w_in_b, w_low, w_a2, row(gla_b_a), *later)

    ltri2, lmask = _gla_tables()
    dret, gq, gk, aret = _retention_tables()
    consts = [jnp.asarray(_key_mask(), BF16), jnp.asarray(dret), jnp.asarray(gq),
              jnp.asarray(gk), jnp.asarray(aret), jnp.asarray(ltri2, BF16), jnp.asarray(lmask)]

    tok_spec = lambda w: pl.BlockSpec((tile, w), lambda i: (jnp.minimum(i, n_tiles - 1), 0))
    out = pl.pallas_call(
        functools.partial(_mixer_ffn2_kernel, steps_per_seq=steps),
        grid=(n_tiles + 1,),
        in_specs=[tok_spec(d), tok_spec(PROJ_W), tok_spec(QK_W)]
                 + [_const_spec(c.shape) for c in consts]
                 + [_const_spec((1, V_W)), _const_spec((1, V_W)), _const_spec((d, d)), _const_spec((1, d)),
                    _const_spec((d, D_FF)), _const_spec((d, D_FF)), _const_spec((D_FF, d)),
                    _const_spec((1, d))],
        out_specs=pl.BlockSpec((tile, d), lambda i: (jnp.maximum(i - 1, 0), 0)),
        out_shape=jax.ShapeDtypeStruct((n_rows, d), F32),
        scratch_shapes=[pltpu.VMEM((tile, 2 * V_W), BF16), pltpu.VMEM((tile, D_FF), BF16),
                        pltpu.VMEM((tile, d), F32),
                        pltpu.VMEM((QK_W, HEAD_V), F32), pltpu.VMEM((QK_W, HEAD_V), F32)],
        compiler_params=pltpu.CompilerParams(dimension_semantics=("arbitrary",),
                                             vmem_limit_bytes=VMEM_LIMIT),
        name="mixer_ffn2",
    )(x1, proj, la, *consts, row(ret_norm_g), row(gla_norm_g), w_out_b, row(ffn2_norm_g),
      wg2_b, wu2_b, wd2_b, row(final_norm_g))
    return out.reshape(bsz, seq, d)
```

```python
import functools

import numpy as np
import jax
import jax.numpy as jnp
from jax import lax
from jax.experimental import pallas as pl
from jax.experimental.pallas import tpu as pltpu

D_MODEL = 1024
CHUNK = 64
RMS_EPS = 1e-6
ROPE_BASE = 10000.0
N_HEADS = 4
QK_W = 256
V_W = 512
HEAD_V = V_W // N_HEADS
HEAD_QK = QK_W // N_HEADS
GATE_RANK = 16
GATE_NORM = 16.0
D_FF = 2816
PROJ_W = 2 * (2 * QK_W + 2 * V_W)
RQ, RK, RV, RG = 0, 256, 512, 1024
GQ, GK, GV, GG = 1536, 1792, 2048, 2560
GLOW = 3072

LANE = 128
SUBLANES = 8
BF16_SUBLANES = 16
MXU_N = 256
VMEM_LIMIT = 56 * 1024 * 1024

GLA_LEVELS = (64, 32, 16, 8, 4, 2)
MIXER_STAGES_PER_ROUND = (2,) * 11
LOG2_E = 1.4426950408889634

BF16 = jnp.bfloat16
F32 = jnp.float32


def _dot(a, b):
    return jnp.dot(a, b, preferred_element_type=F32)


def _rms(x, g):
    ms = jnp.mean(x * x, axis=-1, keepdims=True)
    return x * lax.rsqrt(ms + RMS_EPS) * g


def _silu(x):
    h = 0.5 * x
    return h * jnp.tanh(h) + h


def _ffn(h, wg_ref, wu_ref, wd_ref, act_ref, interleave=None):
    for c in range(D_FF // MXU_N):
        cols = slice(c * MXU_N, (c + 1) * MXU_N)
        gate = _dot(h, wg_ref[:, cols])
        up = _dot(h, wu_ref[:, cols])
        act_ref[:, cols] = (_silu(gate) * up).astype(BF16)
        if interleave is not None:
            next(interleave, None)
    if interleave is not None:
        for _ in interleave:
            pass
    return _dot(act_ref[...], wd_ref[...])


def _ffn1_proj_kernel(x_ref, cos_ref, sin_ref, g1_ref, wg_ref, wu_ref, wd_ref, gm_ref, win_ref,
                      wlow_ref, wa2_ref, ba_ref, *rest):
    later_f32, (x1_ref, proj_ref, la_ref), later_bf16, act_ref = rest[:4], rest[4:7], rest[7:11], rest[11]

    def cast_later_weights():
        yield
        for src, dst in zip(later_f32, later_bf16):
            dst[...] = src[...].astype(BF16)

    def half_tile(rows):
        x = x_ref[rows, :]
        h = _rms(x, g1_ref[...]).astype(BF16)
        yield
        for c in range(D_FF // MXU_N):
            cols = slice(c * MXU_N, (c + 1) * MXU_N)
            gate = _dot(h, wg_ref[:, cols])
            up = _dot(h, wu_ref[:, cols])
            act_ref[rows, cols] = (_silu(gate) * up).astype(BF16)
        yield
        x1 = x + 0.5 * _dot(act_ref[rows, :], wd_ref[...])
        x1_ref[rows, :] = x1
        h = _rms(x1, gm_ref[...]).astype(BF16)
        yield
        cos = cos_ref[rows, :]
        sin = sin_ref[rows, :]
        first_half = (lax.broadcasted_iota(jnp.int32, cos.shape, 1) % HEAD_QK) < HEAD_QK // 2
        for off, scale in ((RQ, 1.0), (RK, HEAD_QK ** -0.5)):
            t = _dot(h, win_ref[:, off:off + QK_W])
            for col in range(QK_W // LANE):
                tc = t[:, col * LANE:(col + 1) * LANE]
                partner = jnp.where(first_half, pltpu.roll(tc, LANE - HEAD_QK // 2, 1),
                                    pltpu.roll(tc, HEAD_QK // 2, 1))
                proj_ref[rows, off + col * LANE:off + (col + 1) * LANE] = (
                    (tc * cos + partner * sin) * scale).astype(BF16)
        proj_ref[rows, RV:GQ] = _dot(h, win_ref[:, RV:GQ]).astype(BF16)
        proj_ref[rows, GQ:GK] = (_dot(h, win_ref[:, GQ:GK]) * (HEAD_QK ** -0.5)).astype(BF16)
        proj_ref[rows, GK:PROJ_W] = _dot(h, win_ref[:, GK:PROJ_W]).astype(BF16)
        low = _dot(h, wlow_ref[...]).astype(BF16)
        yield
        z = _dot(low, wa2_ref[...]) + ba_ref[...]
        la_ref[rows, :] = (jnp.minimum(z, 0.0) - jnp.log(1.0 + jnp.exp(-jnp.abs(z)))) * (1.0 / GATE_NORM)

    n_half = x_ref.shape[0] // 2
    streams = [half_tile(pl.ds(i * n_half, n_half)) for i in range(2)] + [cast_later_weights()]
    while streams:
        streams = [g for g in streams if next(g, StopIteration) is not StopIteration]


def _head_norm_gate(o, norm_g, gate):
    outs = []
    for h in range(N_HEADS):
        cols = slice(h * HEAD_V, (h + 1) * HEAD_V)
        oh = o[:, cols]
        ms = jnp.mean(oh * oh, axis=-1, keepdims=True)
        outs.append(oh * lax.rsqrt(ms + RMS_EPS) * norm_g[:, cols] * _silu(gate[:, cols]))
    return jnp.concatenate(outs, axis=-1)


def _head_block_diag(blocks):
    zero = jnp.zeros_like(blocks[0])
    return jnp.concatenate(
        [jnp.concatenate([blk if j == h else zero for j in range(N_HEADS)], axis=1)
         for h, blk in enumerate(blocks)], axis=0)


def _state_block_diag(state):
    return _head_block_diag([state[h * HEAD_QK:(h + 1) * HEAD_QK, :].astype(BF16) for h in range(N_HEADS)])


def _value_block_diag(v):
    return _head_block_diag([v[:, h * HEAD_V:(h + 1) * HEAD_V] for h in range(N_HEADS)])


def _midpoint_rows(b, s):
    rows, w = b.shape
    if s >= SUBLANES:
        blocks = b.reshape(rows // s, s, w)
        return jnp.broadcast_to(blocks[:, s // 2 - 1:s // 2, :], blocks.shape).reshape(rows, w)
    groups = b.reshape(rows // SUBLANES, SUBLANES, w)
    sub = lax.broadcasted_iota(jnp.int32, groups.shape, 1)
    out = None
    for first in reversed(range(0, SUBLANES, s)):
        ref_row = jnp.broadcast_to(groups[:, first + s // 2 - 1:first + s // 2, :], groups.shape)
        out = ref_row if out is None else jnp.where(sub < first + s, ref_row, out)
    return out.reshape(rows, w)


def _mixer_ffn2_kernel(x1_ref, proj_ref, la_ref,
                       kmask_ref, dret_ref, gq_ref, gk_ref, aret_ref, ltri_ref, lmask_ref,
                       rn_ref, gn_ref, wout_ref, g2_ref, wg_ref, wu_ref, wd_ref, gf_ref,
                       out_ref, o_ref, act_ref, x2_ref, h2_ref, sr_ref, sg_ref, *, steps_per_seq):
    rows_per_step = x1_ref.shape[0]
    step = pl.program_id(0)

    @pl.when(step == 0)
    def _():
        x2_ref[...] = jnp.zeros_like(x2_ref)
        h2_ref[...] = jnp.zeros_like(h2_ref)

    @pl.when(step % steps_per_seq == 0)
    def _():
        sr_ref[...] = jnp.zeros_like(sr_ref)
        sg_ref[...] = jnp.zeros_like(sg_ref)

    def scores(ql, kl):
        out = []
        for col in range(QK_W // LANE):
            lanes = slice(col * LANE, (col + 1) * LANE)
            w = jnp.concatenate([kl[:, lanes]] * 2, axis=0).T * kmask_ref[...]
            out.append(_dot(ql[:, lanes], w))
        return jnp.concatenate(out, axis=1)

    def attend(s, q_in, v, state):
        s = s.astype(BF16)
        zero = jnp.zeros((CHUNK, HEAD_V), BF16)
        out = []
        for col in range(QK_W // LANE):
            lanes = slice(col * LANE, (col + 1) * LANE)
            rhs = []
            for h in (2 * col, 2 * col + 1):
                vh = v[:, h * HEAD_V:(h + 1) * HEAD_V]
                rhs.append([vh, zero] if h % 2 == 0 else [zero, vh])
            for h in (2 * col, 2 * col + 1):
                sh = state[h * HEAD_QK:(h + 1) * HEAD_QK, :].astype(BF16)
                rhs.append([sh, zero] if h % 2 == 0 else [zero, sh])
            rhs = jnp.concatenate([jnp.concatenate(r, axis=1) for r in rhs], axis=0)
            out.append(_dot(jnp.concatenate([s[:, lanes], q_in[:, lanes]], axis=1), rhs))
        return jnp.concatenate(out, axis=1)

    def state_update(k_out, v):
        k_t = k_out.T
        return jnp.concatenate([_dot(k_t[h * HEAD_QK:(h + 1) * HEAD_QK, :], v[:, h * HEAD_V:(h + 1) * HEAD_V])
                                for h in range(N_HEADS)], axis=0)

    def mixer_stages():
        chunk_rows = [pl.ds(c * CHUNK, CHUNK) for c in range(rows_per_step // CHUNK)]

        q_r = [proj_ref[r, RQ:RQ + QK_W] for r in chunk_rows]
        k_r = [proj_ref[r, RK:RK + QK_W] for r in chunk_rows]
        v_r = [proj_ref[r, RV:RV + V_W] for r in chunk_rows]
        s_r = [scores(q, k) * dret_ref[...] for q, k in zip(q_r, k_r)]
        qin_r = [(q.astype(F32) * gq_ref[...]).astype(BF16) for q in q_r]
        yield
        kv_r = [state_update((k.astype(F32) * gk_ref[...]).astype(BF16), v) for k, v in zip(k_r, v_r)]
        yield

        b = []
        for r in chunk_rows:
            la = la_ref[r, :]
            la_hi = la.astype(BF16)
            la_lo = (la - la_hi.astype(F32)).astype(BF16)
            b.append(_dot(ltri_ref[...], jnp.concatenate([la_hi, la_lo], axis=0)) * LOG2_E)
        b_last = [jnp.broadcast_to(bc[CHUNK - 1:CHUNK, :], (2 * CHUNK, QK_W)) for bc in b]
        a_col = [jnp.exp2(bl.T)[:, :LANE] for bl in b_last]
        yield
        q_g = [proj_ref[r, GQ:GQ + QK_W] for r in chunk_rows]
        k_g = [proj_ref[r, GK:GK + QK_W] for r in chunk_rows]
        v_g = [proj_ref[r, GV:GV + V_W] for r in chunk_rows]
        s_g = [scores(q, k) * lmask_ref[0] for q, k in zip(q_g, k_g)]
        for l, s in enumerate(GLA_LEVELS):
            for c in range(len(chunk_rows)):
                if c % 4 == 0:
                    yield
                f = jnp.exp2(-jnp.abs(b[c] - _midpoint_rows(b[c], s))).astype(BF16)
                s_g[c] = s_g[c] + scores(q_g[c] * f, k_g[c] * f) * lmask_ref[1 + l]
        yield
        qin_g = [q * jnp.exp2(bc).astype(BF16) for q, bc in zip(q_g, b)]
        kv_g = [state_update(k * jnp.exp2(bl[:CHUNK, :] - bc).astype(BF16), v)
                for k, bc, bl, v in zip(k_g, b, b_last, v_g)]

        sr = sr_ref[...]
        sg = sg_ref[...]
        for c, r in enumerate(chunk_rows):
            if c % 2 == 0:
                yield
            o = attend(s_r[c], qin_r[c], v_r[c], sr)
            sr = sr * aret_ref[...] + kv_r[c]
            gate = proj_ref[r, RG:RG + V_W].astype(F32)
            o_ref[r, 0:V_W] = _head_norm_gate(o, rn_ref[...], gate).astype(BF16)
            o = attend(s_g[c], qin_g[c], v_g[c], sg)
            sg = sg * a_col[c] + kv_g[c]
            gate = proj_ref[r, GG:GG + V_W].astype(F32)
            o_ref[r, V_W:2 * V_W] = _head_norm_gate(o, gn_ref[...], gate).astype(BF16)
        sr_ref[...] = sr
        sg_ref[...] = sg

    def finish_previous(rows):
        h = h2_ref[rows, :]
        for c in range(D_FF // MXU_N):
            cols = slice(c * MXU_N, (c + 1) * MXU_N)
            gate = _dot(h, wg_ref[:, cols])
            up = _dot(h, wu_ref[:, cols])
            act_ref[rows, cols] = (_silu(gate) * up).astype(BF16)
            yield
        y = _dot(act_ref[rows, :], wd_ref[...])
        out_ref[rows, :] = _rms(x2_ref[rows, :] + 0.5 * y, gf_ref[...])

    n_half = rows_per_step // 2
    mixer = mixer_stages()
    halves = [finish_previous(pl.ds(0, n_half)), finish_previous(pl.ds(n_half, n_half))]
    for n_mixer in MIXER_STAGES_PER_ROUND:
        for g in halves:
            next(g, None)
        for _ in range(n_mixer):
            next(mixer, None)
    for g in halves + [mixer]:
        for _ in g:
            pass
    for i in range(2):
        rows = pl.ds(i * n_half, n_half)
        x2 = x1_ref[rows, :] + _dot(o_ref[rows, :], wout_ref[...])
        x2_ref[rows, :] = x2
        h2_ref[rows, :] = _rms(x2, g2_ref[...]).astype(BF16)


def _gla_tables():
    t = np.arange(CHUNK)[:, None]
    u = np.arange(CHUNK)[None, :]
    masks = [np.eye(CHUNK, dtype=bool)]
    for s in GLA_LEVELS:
        same_block = (t // s) == (u // s)
        masks.append(same_block & (((t % s) < s // 2) != ((u % s) < s // 2)))
    ltri = (u <= t).astype(np.float32)
    ltri2 = np.concatenate([ltri, ltri], axis=1)
    lmask = np.stack([np.tile(m, (1, N_HEADS)) for m in masks]).astype(np.float32)
    return ltri2, lmask


def _retention_tables():
    gamma = 1.0 - 2.0 ** (-5.0 - np.arange(N_HEADS, dtype=np.float64))
    head_of_lane = np.arange(QK_W) // HEAD_QK
    i = np.arange(CHUNK)
    dist = np.abs(i[:, None] - i[None, :])
    dret = np.concatenate([gamma[h] ** dist for h in range(N_HEADS)], axis=1)
    gq = gamma[head_of_lane][None, :] ** (i[:, None] + 1.0)
    gk = gamma[head_of_lane][None, :] ** (CHUNK - 1.0 - i[:, None])
    aret = np.broadcast_to((gamma[head_of_lane] ** CHUNK)[:, None], (QK_W, HEAD_V))
    return tuple(np.asarray(a, np.float32) for a in (dret, gq, gk, aret))


def _key_mask():
    idx = np.arange(LANE) // HEAD_QK
    return idx[:, None] == idx[None, :]


def _const_spec(shape):
    nd = len(shape)
    return pl.BlockSpec(shape, lambda *_: (0,) * nd, pipeline_mode=pl.Buffered(1))


def _slab_spec(shape, n_steps):
    rows, cols = shape
    slab = next(s for s in range(BF16_SUBLANES, rows + 1, BF16_SUBLANES)
                if rows % s == 0 and n_steps % (rows // s) == 0 and rows // s <= n_steps)
    repeat = n_steps // (rows // slab)
    return pl.BlockSpec((slab, cols), lambda i: (i // repeat, 0))


def _tile_rows(total_rows):
    tile = 512
    assert total_rows % tile == 0
    return tile


@jax.jit
def kernel(x, ffn1_norm_g, ffn1_w_gate, ffn1_w_up, ffn1_w_down, mix_norm_g, w_in, ret_norm_g, gla_w_a2,
           gla_b_a, gla_norm_g, w_out, ffn2_norm_g, ffn2_w_gate, ffn2_w_up, ffn2_w_down, final_norm_g):
    bsz, seq, d = x.shape
    assert d == D_MODEL and seq % CHUNK == 0 and ffn1_norm_g.shape[0] == 1
    tile = _tile_rows(seq)
    steps = seq // tile
    n_rows = bsz * seq

    row = lambda g: g.reshape(1, -1).astype(F32)
    w_in_b = w_in[0].astype(BF16)
    w_low = jnp.pad(w_in_b[:, GLOW:], ((0, 0), (0, LANE - GATE_RANK)))
    w_a2 = jnp.pad(gla_w_a2[0], ((0, LANE - GATE_RANK), (0, 0))).astype(BF16)

    half = HEAD_QK // 2
    pos = jnp.arange(seq, dtype=F32)
    inv = ROPE_BASE ** (-jnp.arange(half, dtype=F32) * 2.0 / HEAD_QK)
    ang = pos[:, None] * inv[None, :]
    sign = np.where((np.arange(LANE) % HEAD_QK) < half, -1.0, 1.0).astype(np.float32)
    cos = jnp.tile(jnp.cos(ang), (1, LANE // half))
    sin = jnp.tile(jnp.sin(ang), (1, LANE // half)) * sign[None, :]

    x2d = x.reshape(n_rows, d)
    n_tiles = n_rows // tile
    row_spec = lambda w: pl.BlockSpec((tile, w), lambda i: (i, 0))
    pos_spec = pl.BlockSpec((tile, LANE), lambda i: (i % steps, 0))
    later = [w_out[0], ffn2_w_gate[0], ffn2_w_up[0], ffn2_w_down[0]]
    later_specs = [_slab_spec(w.shape, n_tiles) for w in later]
    x1, proj, la, w_out_b, wg2_b, wu2_b, wd2_b = pl.pallas_call(
        _ffn1_proj_kernel,
        grid=(n_tiles,),
        in_specs=[row_spec(d), pos_spec, pos_spec, _const_spec((1, d)),
                  _const_spec((d, D_FF)), _const_spec((d, D_FF)), _const_spec((D_FF, d)),
                  _const_spec((1, d)), _const_spec(w_in_b.shape),
                  _const_spec((d, LANE)), _const_spec((LANE, QK_W)), _const_spec((1, QK_W))] + later_specs,
        out_specs=[row_spec(d), row_spec(PROJ_W), row_spec(QK_W)] + later_specs,
        out_shape=[jax.ShapeDtypeStruct((n_rows, d), F32),
                   jax.ShapeDtypeStruct((n_rows, PROJ_W), BF16),
                   jax.ShapeDtypeStruct((n_rows, QK_W), F32)]
                  + [jax.ShapeDtypeStruct(w.shape, BF16) for w in later],
        scratch_shapes=[pltpu.VMEM((tile, D_FF), BF16)],
        compiler_params=pltpu.CompilerParams(dimension_semantics=("arbitrary",),
                                             vmem_limit_bytes=VMEM_LIMIT),
        name="ffn1_proj",
    )(x2d, cos, sin, row(ffn1_norm_g), ffn1_w_gate[0].astype(BF16), ffn1_w_up[0].astype(BF16),
      ffn1_w_down[0].astype(BF16), row(mix_norm_g), w_in_b, w_low, w_a2, row(gla_b_a), *later)

    ltri2, lmask = _gla_tables()
    dret, gq, gk, aret = _retention_tables()
    consts = [jnp.asarray(_key_mask(), BF16), jnp.asarray(dret), jnp.asarray(gq),
              jnp.asarray(gk), jnp.asarray(aret), jnp.asarray(ltri2, BF16), jnp.asarray(lmask)]

    tok_spec = lambda w: pl.BlockSpec((tile, w), lambda i: (jnp.minimum(i, n_tiles - 1), 0))
    out = pl.pallas_call(
        functools.partial(_mixer_ffn2_kernel, steps_per_seq=steps),
        grid=(n_tiles + 1,),
        in_specs=[tok_spec(d), tok_spec(PROJ_W), tok_spec(QK_W)]
                 + [_const_spec(c.shape) for c in consts]
                 + [_const_spec((1, V_W)), _const_spec((1, V_W)), _const_spec((d, d)), _const_spec((1, d)),
                    _const_spec((d, D_FF)), _const_spec((d, D_FF)), _const_spec((D_FF, d)),
                    _const_spec((1, d))],
        out_specs=pl.BlockSpec((tile, d), lambda i: (jnp.maximum(i - 1, 0), 0)),
        out_shape=jax.ShapeDtypeStruct((n_rows, d), F32),
        scratch_shapes=[pltpu.VMEM((tile, 2 * V_W), BF16), pltpu.VMEM((tile, D_FF), BF16),
                        pltpu.VMEM((tile, d), F32), pltpu.VMEM((tile, d), BF16),
                        pltpu.VMEM((QK_W, HEAD_V), F32), pltpu.VMEM((QK_W, HEAD_V), F32)],
        compiler_params=pltpu.CompilerParams(dimension_semantics=("arbitrary",),
                                             vmem_limit_bytes=VMEM_LIMIT),
        name="mixer_ffn2",
    )(x1, proj, la, *consts, row(ret_norm_g), row(gla_norm_g), w_out_b, row(ffn2_norm_g),
      wg2_b, wu2_b, wd2_b, row(final_norm_g))
    return out.reshape(bsz, seq, d)
```

```python
import functools

import numpy as np
import jax
import jax.numpy as jnp
from jax import lax
from jax.experimental import pallas as pl
from jax.experimental.pallas import tpu as pltpu

D_MODEL = 1024
CHUNK = 64
RMS_EPS = 1e-6
ROPE_BASE = 10000.0
N_HEADS = 4
QK_W = 256
V_W = 512
HEAD_V = V_W // N_HEADS
HEAD_QK = QK_W // N_HEADS
GATE_RANK = 16
GATE_NORM = 16.0
D_FF = 2816
PROJ_W = 2 * (2 * QK_W + 2 * V_W)
RQ, RK, RV, RG = 0, 256, 512, 1024
GQ, GK, GV, GG = 1536, 1792, 2048, 2560
GLOW = 3072

LANE = 128
SUBLANES = 8
BF16_SUBLANES = 16
MXU_N = 256
WEIGHT_CHUNK_ROWS = 128
VMEM_LIMIT = 56 * 1024 * 1024

GLA_LEVELS = (64, 32, 16, 8, 4, 2)
MIXER_STAGES_PER_ROUND = (2,) * 11
LOG2_E = 1.4426950408889634

BF16 = jnp.bfloat16
F32 = jnp.float32


def _dot(a, b):
    return jnp.dot(a, b, preferred_element_type=F32)


def _rms(x, g):
    ms = jnp.mean(x * x, axis=-1, keepdims=True)
    return x * lax.rsqrt(ms + RMS_EPS) * g


def _silu(x):
    h = 0.5 * x
    return h * jnp.tanh(h) + h


def _ffn(h, wg_ref, wu_ref, wd_ref, act_ref, interleave=None):
    for c in range(D_FF // MXU_N):
        cols = slice(c * MXU_N, (c + 1) * MXU_N)
        gate = _dot(h, wg_ref[:, cols])
        up = _dot(h, wu_ref[:, cols])
        act_ref[:, cols] = (_silu(gate) * up).astype(BF16)
        if interleave is not None:
            next(interleave, None)
    if interleave is not None:
        for _ in interleave:
            pass
    return _dot(act_ref[...], wd_ref[...])


def _load_weight_as_bf16(src_hbm, dst_ref, stage_ref, sem):
    n_rows, n_cols = dst_ref.shape
    n_chunks = n_rows // WEIGHT_CHUNK_ROWS

    def chunk_copy(k, slot):
        return pltpu.make_async_copy(
            src_hbm.at[pl.ds(k * WEIGHT_CHUNK_ROWS, WEIGHT_CHUNK_ROWS), :],
            stage_ref.at[slot, :, pl.ds(0, n_cols)], sem.at[slot])

    chunk_copy(0, 0).start()

    def body(k, carry):
        slot = k % 2

        @pl.when(k + 1 < n_chunks)
        def _():
            chunk_copy(k + 1, 1 - slot).start()

        chunk_copy(k, slot).wait()
        rows = pl.ds(pl.multiple_of(k * WEIGHT_CHUNK_ROWS, WEIGHT_CHUNK_ROWS), WEIGHT_CHUNK_ROWS)
        dst_ref[rows, :] = stage_ref[slot, :, pl.ds(0, n_cols)].astype(BF16)
        return carry

    lax.fori_loop(0, n_chunks, body, 0)


def _ffn1_proj_kernel(x_ref, cos_ref, sin_ref, g1_ref, wg_hbm, wu_hbm, wd_hbm, gm_ref, win_hbm,
                      wlow_ref, wa2_ref, ba_ref, *rest):
    later_f32, (x1_ref, proj_ref, la_ref), later_bf16 = rest[:4], rest[4:7], rest[7:11]
    act_ref, wg_ref, wu_ref, wd_ref, win_ref, stage_ref, sem = rest[11:]

    @pl.when(pl.program_id(0) == 0)
    def _():
        for src, dst in ((wg_hbm, wg_ref), (wu_hbm, wu_ref), (wd_hbm, wd_ref), (win_hbm, win_ref)):
            _load_weight_as_bf16(src, dst, stage_ref, sem)

    def cast_later_weights():
        yield
        for src, dst in zip(later_f32, later_bf16):
            dst[...] = src[...].astype(BF16)

    def half_tile(rows):
        x = x_ref[rows, :]
        h = _rms(x, g1_ref[...]).astype(BF16)
        yield
        for c in range(D_FF // MXU_N):
            cols = slice(c * MXU_N, (c + 1) * MXU_N)
            gate = _dot(h, wg_ref[:, cols])
            up = _dot(h, wu_ref[:, cols])
            act_ref[rows, cols] = (_silu(gate) * up).astype(BF16)
        yield
        x1 = x + 0.5 * _dot(act_ref[rows, :], wd_ref[...])
        x1_ref[rows, :] = x1
        h = _rms(x1, gm_ref[...]).astype(BF16)
        yield
        cos = cos_ref[rows, :]
        sin = sin_ref[rows, :]
        first_half = (lax.broadcasted_iota(jnp.int32, cos.shape, 1) % HEAD_QK) < HEAD_QK // 2
        for off, scale in ((RQ, 1.0), (RK, HEAD_QK ** -0.5)):
            t = _dot(h, win_ref[:, off:off + QK_W])
            for col in range(QK_W // LANE):
                tc = t[:, col * LANE:(col + 1) * LANE]
                partner = jnp.where(first_half, pltpu.roll(tc, LANE - HEAD_QK // 2, 1),
                                    pltpu.roll(tc, HEAD_QK // 2, 1))
                proj_ref[rows, off + col * LANE:off + (col + 1) * LANE] = (
                    (tc * cos + partner * sin) * scale).astype(BF16)
        proj_ref[rows, RV:GQ] = _dot(h, win_ref[:, RV:GQ]).astype(BF16)
        proj_ref[rows, GQ:GK] = (_dot(h, win_ref[:, GQ:GK]) * (HEAD_QK ** -0.5)).astype(BF16)
        proj_ref[rows, GK:PROJ_W] = _dot(h, win_ref[:, GK:PROJ_W]).astype(BF16)
        low = _dot(h, wlow_ref[...]).astype(BF16)
        yield
        z = _dot(low, wa2_ref[...]) + ba_ref[...]
        la_ref[rows, :] = (jnp.minimum(z, 0.0) - jnp.log(1.0 + jnp.exp(-jnp.abs(z)))) * (1.0 / GATE_NORM)

    n_half = x_ref.shape[0] // 2
    streams = [half_tile(pl.ds(i * n_half, n_half)) for i in range(2)] + [cast_later_weights()]
    while streams:
        streams = [g for g in streams if next(g, StopIteration) is not StopIteration]


def _head_norm_gate(o, norm_g, gate):
    outs = []
    for h in range(N_HEADS):
        cols = slice(h * HEAD_V, (h + 1) * HEAD_V)
        oh = o[:, cols]
        ms = jnp.mean(oh * oh, axis=-1, keepdims=True)
        outs.append(oh * lax.rsqrt(ms + RMS_EPS) * norm_g[:, cols] * _silu(gate[:, cols]))
    return jnp.concatenate(outs, axis=-1)


def _head_block_diag(blocks):
    zero = jnp.zeros_like(blocks[0])
    return jnp.concatenate(
        [jnp.concatenate([blk if j == h else zero for j in range(N_HEADS)], axis=1)
         for h, blk in enumerate(blocks)], axis=0)


def _state_block_diag(state):
    return _head_block_diag([state[h * HEAD_QK:(h + 1) * HEAD_QK, :].astype(BF16) for h in range(N_HEADS)])


def _value_block_diag(v):
    return _head_block_diag([v[:, h * HEAD_V:(h + 1) * HEAD_V] for h in range(N_HEADS)])


def _midpoint_rows(b, s):
    rows, w = b.shape
    if s >= SUBLANES:
        blocks = b.reshape(rows // s, s, w)
        return jnp.broadcast_to(blocks[:, s // 2 - 1:s // 2, :], blocks.shape).reshape(rows, w)
    groups = b.reshape(rows // SUBLANES, SUBLANES, w)
    sub = lax.broadcasted_iota(jnp.int32, groups.shape, 1)
    out = None
    for first in reversed(range(0, SUBLANES, s)):
        ref_row = jnp.broadcast_to(groups[:, first + s // 2 - 1:first + s // 2, :], groups.shape)
        out = ref_row if out is None else jnp.where(sub < first + s, ref_row, out)
    return out.reshape(rows, w)


def _mixer_ffn2_kernel(x1_ref, proj_ref, la_ref,
                       kmask_ref, dret_ref, gq_ref, gk_ref, aret_ref, ltri_ref, lmask_ref,
                       rn_ref, gn_ref, wout_ref, g2_ref, wg_ref, wu_ref, wd_ref, gf_ref,
                       out_ref, o_ref, act_ref, x2_ref, h2_ref, sr_ref, sg_ref, *, steps_per_seq):
    rows_per_step = x1_ref.shape[0]
    step = pl.program_id(0)

    @pl.when(step == 0)
    def _():
        x2_ref[...] = jnp.zeros_like(x2_ref)
        h2_ref[...] = jnp.zeros_like(h2_ref)

    @pl.when(step % steps_per_seq == 0)
    def _():
        sr_ref[...] = jnp.zeros_like(sr_ref)
        sg_ref[...] = jnp.zeros_like(sg_ref)

    def scores(ql, kl):
        out = []
        for col in range(QK_W // LANE):
            lanes = slice(col * LANE, (col + 1) * LANE)
            w = jnp.concatenate([kl[:, lanes]] * 2, axis=0).T * kmask_ref[...]
            out.append(_dot(ql[:, lanes], w))
        return jnp.concatenate(out, axis=1)

    def attend(s, q_in, v, state):
        s = s.astype(BF16)
        zero = jnp.zeros((CHUNK, HEAD_V), BF16)
        out = []
        for col in range(QK_W // LANE):
            lanes = slice(col * LANE, (col + 1) * LANE)
            rhs = []
            for h in (2 * col, 2 * col + 1):
                vh = v[:, h * HEAD_V:(h + 1) * HEAD_V]
                rhs.append([vh, zero] if h % 2 == 0 else [zero, vh])
            for h in (2 * col, 2 * col + 1):
                sh = state[h * HEAD_QK:(h + 1) * HEAD_QK, :].astype(BF16)
                rhs.append([sh, zero] if h % 2 == 0 else [zero, sh])
            rhs = jnp.concatenate([jnp.concatenate(r, axis=1) for r in rhs], axis=0)
            out.append(_dot(jnp.concatenate([s[:, lanes], q_in[:, lanes]], axis=1), rhs))
        return jnp.concatenate(out, axis=1)

    def state_update(k_out, v):
        k_t = k_out.T
        return jnp.concatenate([_dot(k_t[h * HEAD_QK:(h + 1) * HEAD_QK, :], v[:, h * HEAD_V:(h + 1) * HEAD_V])
                                for h in range(N_HEADS)], axis=0)

    def mixer_stages():
        chunk_rows = [pl.ds(c * CHUNK, CHUNK) for c in range(rows_per_step // CHUNK)]

        q_r = [proj_ref[r, RQ:RQ + QK_W] for r in chunk_rows]
        k_r = [proj_ref[r, RK:RK + QK_W] for r in chunk_rows]
        v_r = [proj_ref[r, RV:RV + V_W] for r in chunk_rows]
        s_r = [scores(q, k) * dret_ref[...] for q, k in zip(q_r, k_r)]
        qin_r = [(q.astype(F32) * gq_ref[...]).astype(BF16) for q in q_r]
        yield
        kv_r = [state_update((k.astype(F32) * gk_ref[...]).astype(BF16), v) for k, v in zip(k_r, v_r)]
        yield

        b = []
        for r in chunk_rows:
            la = la_ref[r, :]
            la_hi = la.astype(BF16)
            la_lo = (la - la_hi.astype(F32)).astype(BF16)
            b.append(_dot(ltri_ref[...], jnp.concatenate([la_hi, la_lo], axis=0)) * LOG2_E)
        b_last = [jnp.broadcast_to(bc[CHUNK - 1:CHUNK, :], (2 * CHUNK, QK_W)) for bc in b]
        a_col = [jnp.exp2(bl.T)[:, :LANE] for bl in b_last]
        yield
        q_g = [proj_ref[r, GQ:GQ + QK_W] for r in chunk_rows]
        k_g = [proj_ref[r, GK:GK + QK_W] for r in chunk_rows]
        v_g = [proj_ref[r, GV:GV + V_W] for r in chunk_rows]
        s_g = [scores(q, k) * lmask_ref[0] for q, k in zip(q_g, k_g)]
        for l, s in enumerate(GLA_LEVELS):
            for c in range(len(chunk_rows)):
                if c % 4 == 0:
                    yield
                f = jnp.exp2(-jnp.abs(b[c] - _midpoint_rows(b[c], s))).astype(BF16)
                s_g[c] = s_g[c] + scores(q_g[c] * f, k_g[c] * f) * lmask_ref[1 + l]
        yield
        qin_g = [q * jnp.exp2(bc).astype(BF16) for q, bc in zip(q_g, b)]
        kv_g = [state_update(k * jnp.exp2(bl[:CHUNK, :] - bc).astype(BF16), v)
                for k, bc, bl, v in zip(k_g, b, b_last, v_g)]

        sr = sr_ref[...]
        sg = sg_ref[...]
        for c, r in enumerate(chunk_rows):
            if c % 2 == 0:
                yield
            o = attend(s_r[c], qin_r[c], v_r[c], sr)
            sr = sr * aret_ref[...] + kv_r[c]
            gate = proj_ref[r, RG:RG + V_W].astype(F32)
            o_ref[r, 0:V_W] = _head_norm_gate(o, rn_ref[...], gate).astype(BF16)
            o = attend(s_g[c], qin_g[c], v_g[c], sg)
            sg = sg * a_col[c] + kv_g[c]
            gate = proj_ref[r, GG:GG + V_W].astype(F32)
            o_ref[r, V_W:2 * V_W] = _head_norm_gate(o, gn_ref[...], gate).astype(BF16)
        sr_ref[...] = sr
        sg_ref[...] = sg

    def finish_previous(rows):
        h = h2_ref[rows, :]
        for c in range(D_FF // MXU_N):
            cols = slice(c * MXU_N, (c + 1) * MXU_N)
            gate = _dot(h, wg_ref[:, cols])
            up = _dot(h, wu_ref[:, cols])
            act_ref[rows, cols] = (_silu(gate) * up).astype(BF16)
            yield
        y = _dot(act_ref[rows, :], wd_ref[...])
        out_ref[rows, :] = _rms(x2_ref[rows, :] + 0.5 * y, gf_ref[...])

    n_half = rows_per_step // 2
    mixer = mixer_stages()
    halves = [finish_previous(pl.ds(0, n_half)), finish_previous(pl.ds(n_half, n_half))]
    for n_mixer in MIXER_STAGES_PER_ROUND:
        for g in halves:
            next(g, None)
        for _ in range(n_mixer):
            next(mixer, None)
    for g in halves + [mixer]:
        for _ in g:
            pass
    for i in range(2):
        rows = pl.ds(i * n_half, n_half)
        x2 = x1_ref[rows, :] + _dot(o_ref[rows, :], wout_ref[...])
        x2_ref[rows, :] = x2
        h2_ref[rows, :] = _rms(x2, g2_ref[...]).astype(BF16)


def _gla_tables():
    t = np.arange(CHUNK)[:, None]
    u = np.arange(CHUNK)[None, :]
    masks = [np.eye(CHUNK, dtype=bool)]
    for s in GLA_LEVELS:
        same_block = (t // s) == (u // s)
        masks.append(same_block & (((t % s) < s // 2) != ((u % s) < s // 2)))
    ltri = (u <= t).astype(np.float32)
    ltri2 = np.concatenate([ltri, ltri], axis=1)
    lmask = np.stack([np.tile(m, (1, N_HEADS)) for m in masks]).astype(np.float32)
    return ltri2, lmask


def _retention_tables():
    gamma = 1.0 - 2.0 ** (-5.0 - np.arange(N_HEADS, dtype=np.float64))
    head_of_lane = np.arange(QK_W) // HEAD_QK
    i = np.arange(CHUNK)
    dist = np.abs(i[:, None] - i[None, :])
    dret = np.concatenate([gamma[h] ** dist for h in range(N_HEADS)], axis=1)
    gq = gamma[head_of_lane][None, :] ** (i[:, None] + 1.0)
    gk = gamma[head_of_lane][None, :] ** (CHUNK - 1.0 - i[:, None])
    aret = np.broadcast_to((gamma[head_of_lane] ** CHUNK)[:, None], (QK_W, HEAD_V))
    return tuple(np.asarray(a, np.float32) for a in (dret, gq, gk, aret))


def _key_mask():
    idx = np.arange(LANE) // HEAD_QK
    return idx[:, None] == idx[None, :]


def _const_spec(shape):
    nd = len(shape)
    return pl.BlockSpec(shape, lambda *_: (0,) * nd, pipeline_mode=pl.Buffered(1))


def _slab_spec(shape, n_steps):
    rows, cols = shape
    slab = next(s for s in range(BF16_SUBLANES, rows + 1, BF16_SUBLANES)
                if rows % s == 0 and n_steps % (rows // s) == 0 and rows // s <= n_steps)
    repeat = n_steps // (rows // slab)
    return pl.BlockSpec((slab, cols), lambda i: (i // repeat, 0))


def _tile_rows(total_rows):
    tile = 512
    assert total_rows % tile == 0
    return tile


@jax.jit
def kernel(x, ffn1_norm_g, ffn1_w_gate, ffn1_w_up, ffn1_w_down, mix_norm_g, w_in, ret_norm_g, gla_w_a2,
           gla_b_a, gla_norm_g, w_out, ffn2_norm_g, ffn2_w_gate, ffn2_w_up, ffn2_w_down, final_norm_g):
    bsz, seq, d = x.shape
    assert d == D_MODEL and seq % CHUNK == 0 and ffn1_norm_g.shape[0] == 1
    tile = _tile_rows(seq)
    steps = seq // tile
    n_rows = bsz * seq

    row = lambda g: g.reshape(1, -1).astype(F32)
    in_w = w_in.shape[-1]
    w_low = jnp.pad(w_in[0][:, GLOW:], ((0, 0), (0, LANE - GATE_RANK))).astype(BF16)
    w_a2 = jnp.pad(gla_w_a2[0], ((0, LANE - GATE_RANK), (0, 0))).astype(BF16)

    half = HEAD_QK // 2
    pos = jnp.arange(seq, dtype=F32)
    inv = ROPE_BASE ** (-jnp.arange(half, dtype=F32) * 2.0 / HEAD_QK)
    ang = pos[:, None] * inv[None, :]
    sign = np.where((np.arange(LANE) % HEAD_QK) < half, -1.0, 1.0).astype(np.float32)
    cos = jnp.tile(jnp.cos(ang), (1, LANE // half))
    sin = jnp.tile(jnp.sin(ang), (1, LANE // half)) * sign[None, :]

    x2d = x.reshape(n_rows, d)
    n_tiles = n_rows // tile
    row_spec = lambda w: pl.BlockSpec((tile, w), lambda i: (i, 0))
    pos_spec = pl.BlockSpec((tile, LANE), lambda i: (i % steps, 0))
    hbm_spec = pl.BlockSpec(memory_space=pl.ANY)
    later = [w_out[0], ffn2_w_gate[0], ffn2_w_up[0], ffn2_w_down[0]]
    later_specs = [_slab_spec(w.shape, n_tiles) for w in later]
    x1, proj, la, w_out_b, wg2_b, wu2_b, wd2_b = pl.pallas_call(
        _ffn1_proj_kernel,
        grid=(n_tiles,),
        in_specs=[row_spec(d), pos_spec, pos_spec, _const_spec((1, d)),
                  hbm_spec, hbm_spec, hbm_spec, _const_spec((1, d)), hbm_spec,
                  _const_spec((d, LANE)), _const_spec((LANE, QK_W)), _const_spec((1, QK_W))] + later_specs,
        out_specs=[row_spec(d), row_spec(PROJ_W), row_spec(QK_W)] + later_specs,
        out_shape=[jax.ShapeDtypeStruct((n_rows, d), F32),
                   jax.ShapeDtypeStruct((n_rows, PROJ_W), BF16),
                   jax.ShapeDtypeStruct((n_rows, QK_W), F32)]
                  + [jax.ShapeDtypeStruct(w.shape, BF16) for w in later],
        scratch_shapes=[pltpu.VMEM((tile, D_FF), BF16),
                        pltpu.VMEM((d, D_FF), BF16), pltpu.VMEM((d, D_FF), BF16), pltpu.VMEM((D_FF, d), BF16),
                        pltpu.VMEM((d, in_w), BF16),
                        pltpu.VMEM((2, WEIGHT_CHUNK_ROWS, in_w), F32), pltpu.SemaphoreType.DMA((2,))],
        compiler_params=pltpu.CompilerParams(dimension_semantics=("arbitrary",),
                                             vmem_limit_bytes=VMEM_LIMIT),
        name="ffn1_proj",
    )(x2d, cos, sin, row(ffn1_norm_g), ffn1_w_gate[0], ffn1_w_up[0], ffn1_w_down[0], row(mix_norm_g),
      w_in[0], w_low, w_a2, row(gla_b_a), *later)

    ltri2, lmask = _gla_tables()
    dret, gq, gk, aret = _retention_tables()
    consts = [jnp.asarray(_key_mask(), BF16), jnp.asarray(dret), jnp.asarray(gq),
              jnp.asarray(gk), jnp.asarray(aret), jnp.asarray(ltri2, BF16), jnp.asarray(lmask)]

    tok_spec = lambda w: pl.BlockSpec((tile, w), lambda i: (jnp.minimum(i, n_tiles - 1), 0))
    out = pl.pallas_call(
        functools.partial(_mixer_ffn2_kernel, steps_per_seq=steps),
        grid=(n_tiles + 1,),
        in_specs=[tok_spec(d), tok_spec(PROJ_W), tok_spec(QK_W)]
                 + [_const_spec(c.shape) for c in consts]
                 + [_const_spec((1, V_W)), _const_spec((1, V_W)), _const_spec((d, d)), _const_spec((1, d)),
                    _const_spec((d, D_FF)), _const_spec((d, D_FF)), _const_spec((D_FF, d)),
                    _const_spec((1, d))],
        out_specs=pl.BlockSpec((tile, d), lambda i: (jnp.maximum(i - 1, 0), 0)),
        out_shape=jax.ShapeDtypeStruct((n_rows, d), F32),
        scratch_shapes=[pltpu.VMEM((tile, 2 * V_W), BF16), pltpu.VMEM((tile, D_FF), BF16),
                        pltpu.VMEM((tile, d), F32), pltpu.VMEM((tile, d), BF16),
                        pltpu.VMEM((QK_W, HEAD_V), F32), pltpu.VMEM((QK_W, HEAD_V), F32)],
        compiler_params=pltpu.CompilerParams(dimension_semantics=("arbitrary",),
                                             vmem_limit_bytes=VMEM_LIMIT),
        name="mixer_ffn2",
    )(x1, proj, la, *consts, row(ret_norm_g), row(gla_norm_g), w_out_b, row(ffn2_norm_g),
      wg2_b, wu2_b, wd2_b, row(final_norm_g))
    return out.reshape(bsz, seq, d)
```

```python
import functools

import numpy as np
import jax
import jax.numpy as jnp
from jax import lax
from jax.experimental import pallas as pl
from jax.experimental.pallas import tpu as pltpu

D_MODEL = 1024
CHUNK = 64
RMS_EPS = 1e-6
ROPE_BASE = 10000.0
N_HEADS = 4
QK_W = 256
V_W = 512
HEAD_V = V_W // N_HEADS
HEAD_QK = QK_W // N_HEADS
GATE_RANK = 16
GATE_NORM = 16.0
D_FF = 2816
PROJ_W = 2 * (2 * QK_W + 2 * V_W)
RQ, RK, RV, RG = 0, 256, 512, 1024
GQ, GK, GV, GG = 1536, 1792, 2048, 2560
GLOW = 3072

LANE = 128
SUBLANES = 8
BF16_SUBLANES = 16
MXU_N = 256
WEIGHT_CHUNK_ROWS = 128
WEIGHT_STAGE_SLOTS = 4
VMEM_LIMIT = 56 * 1024 * 1024

GLA_LEVELS = (64, 32, 16, 8, 4, 2)
MIXER_STAGES_PER_ROUND = (2,) * 11
LOG2_E = 1.4426950408889634

BF16 = jnp.bfloat16
F32 = jnp.float32


def _dot(a, b):
    return jnp.dot(a, b, preferred_element_type=F32)


def _dot_nt(a, b):
    return lax.dot_general(a, b, (((1,), (1,)), ((), ())), preferred_element_type=F32)


def _rms(x, g):
    ms = jnp.mean(x * x, axis=-1, keepdims=True)
    return x * lax.rsqrt(ms + RMS_EPS) * g


def _silu(x):
    h = 0.5 * x
    return h * jnp.tanh(h) + h


def _ffn(h, wg_ref, wu_ref, wd_ref, act_ref, interleave=None):
    for c in range(D_FF // MXU_N):
        cols = slice(c * MXU_N, (c + 1) * MXU_N)
        gate = _dot(h, wg_ref[:, cols])
        up = _dot(h, wu_ref[:, cols])
        act_ref[:, cols] = (_silu(gate) * up).astype(BF16)
        if interleave is not None:
            next(interleave, None)
    if interleave is not None:
        for _ in interleave:
            pass
    return _dot(act_ref[...], wd_ref[...])


def _load_weight_as_bf16(src_hbm, dst_ref, stage_ref, sem):
    n_rows, n_cols = dst_ref.shape
    n_chunks = n_rows // WEIGHT_CHUNK_ROWS
    n_slots = stage_ref.shape[0]

    def chunk_copy(k):
        slot = k % n_slots
        return pltpu.make_async_copy(
            src_hbm.at[pl.ds(k * WEIGHT_CHUNK_ROWS, WEIGHT_CHUNK_ROWS), :],
            stage_ref.at[slot, :, pl.ds(0, n_cols)], sem.at[slot])

    for k in range(min(n_slots - 1, n_chunks)):
        chunk_copy(k).start()

    def body(k, carry):
        @pl.when(k + n_slots - 1 < n_chunks)
        def _():
            chunk_copy(k + n_slots - 1).start()

        chunk_copy(k).wait()
        rows = pl.ds(pl.multiple_of(k * WEIGHT_CHUNK_ROWS, WEIGHT_CHUNK_ROWS), WEIGHT_CHUNK_ROWS)
        dst_ref[rows, :] = stage_ref[k % n_slots, :, pl.ds(0, n_cols)].astype(BF16)
        return carry

    lax.fori_loop(0, n_chunks, body, 0)


def _ffn1_proj_kernel(x_ref, cos_ref, sin_ref, g1_ref, wg_hbm, wu_hbm, wd_hbm, gm_ref, win_hbm,
                      wlow_ref, wa2_ref, ba_ref, *rest):
    later_f32, (x1_ref, proj_ref, la_ref), later_bf16 = rest[:4], rest[4:7], rest[7:11]
    act_ref, wg_ref, wu_ref, wd_ref, win_ref, stage_ref, sem = rest[11:]

    @pl.when(pl.program_id(0) == 0)
    def _():
        for src, dst in ((wg_hbm, wg_ref), (wu_hbm, wu_ref), (wd_hbm, wd_ref), (win_hbm, win_ref)):
            _load_weight_as_bf16(src, dst, stage_ref, sem)

    def cast_later_weights():
        yield
        for src, dst in zip(later_f32, later_bf16):
            dst[...] = src[...].astype(BF16)

    def half_tile(rows):
        x = x_ref[rows, :]
        h = _rms(x, g1_ref[...]).astype(BF16)
        yield
        for c in range(D_FF // MXU_N):
            cols = slice(c * MXU_N, (c + 1) * MXU_N)
            gate = _dot(h, wg_ref[:, cols])
            up = _dot(h, wu_ref[:, cols])
            act_ref[rows, cols] = (_silu(gate) * up).astype(BF16)
        yield
        x1 = x + 0.5 * _dot(act_ref[rows, :], wd_ref[...])
        x1_ref[rows, :] = x1
        h = _rms(x1, gm_ref[...]).astype(BF16)
        yield
        cos = cos_ref[rows, :]
        sin = sin_ref[rows, :]
        first_half = (lax.broadcasted_iota(jnp.int32, cos.shape, 1) % HEAD_QK) < HEAD_QK // 2
        for off, scale in ((RQ, 1.0), (RK, HEAD_QK ** -0.5)):
            t = _dot_nt(h, win_ref[off:off + QK_W, :])
            for col in range(QK_W // LANE):
                tc = t[:, col * LANE:(col + 1) * LANE]
                partner = jnp.where(first_half, pltpu.roll(tc, LANE - HEAD_QK // 2, 1),
                                    pltpu.roll(tc, HEAD_QK // 2, 1))
                proj_ref[rows, off + col * LANE:off + (col + 1) * LANE] = (
                    (tc * cos + partner * sin) * scale).astype(BF16)
        proj_ref[rows, RV:GQ] = _dot_nt(h, win_ref[RV:GQ, :]).astype(BF16)
        proj_ref[rows, GQ:GK] = (_dot_nt(h, win_ref[GQ:GK, :]) * (HEAD_QK ** -0.5)).astype(BF16)
        proj_ref[rows, GK:PROJ_W] = _dot_nt(h, win_ref[GK:PROJ_W, :]).astype(BF16)
        low = _dot(h, wlow_ref[...]).astype(BF16)
        yield
        z = _dot(low, wa2_ref[...]) + ba_ref[...]
        la_ref[rows, :] = (jnp.minimum(z, 0.0) - jnp.log(1.0 + jnp.exp(-jnp.abs(z)))) * (1.0 / GATE_NORM)

    n_half = x_ref.shape[0] // 2
    streams = [half_tile(pl.ds(i * n_half, n_half)) for i in range(2)] + [cast_later_weights()]
    while streams:
        streams = [g for g in streams if next(g, StopIteration) is not StopIteration]


def _head_norm_gate(o, norm_g, gate):
    outs = []
    for h in range(N_HEADS):
        cols = slice(h * HEAD_V, (h + 1) * HEAD_V)
        oh = o[:, cols]
        ms = jnp.mean(oh * oh, axis=-1, keepdims=True)
        outs.append(oh * lax.rsqrt(ms + RMS_EPS) * norm_g[:, cols] * _silu(gate[:, cols]))
    return jnp.concatenate(outs, axis=-1)


def _head_block_diag(blocks):
    zero = jnp.zeros_like(blocks[0])
    return jnp.concatenate(
        [jnp.concatenate([blk if j == h else zero for j in range(N_HEADS)], axis=1)
         for h, blk in enumerate(blocks)], axis=0)


def _state_block_diag(state):
    return _head_block_diag([state[h * HEAD_QK:(h + 1) * HEAD_QK, :].astype(BF16) for h in range(N_HEADS)])


def _value_block_diag(v):
    return _head_block_diag([v[:, h * HEAD_V:(h + 1) * HEAD_V] for h in range(N_HEADS)])


def _midpoint_rows(b, s):
    rows, w = b.shape
    if s >= SUBLANES:
        blocks = b.reshape(rows // s, s, w)
        return jnp.broadcast_to(blocks[:, s // 2 - 1:s // 2, :], blocks.shape).reshape(rows, w)
    groups = b.reshape(rows // SUBLANES, SUBLANES, w)
    sub = lax.broadcasted_iota(jnp.int32, groups.shape, 1)
    out = None
    for first in reversed(range(0, SUBLANES, s)):
        ref_row = jnp.broadcast_to(groups[:, first + s // 2 - 1:first + s // 2, :], groups.shape)
        out = ref_row if out is None else jnp.where(sub < first + s, ref_row, out)
    return out.reshape(rows, w)


def _mixer_ffn2_kernel(x1_ref, proj_ref, la_ref,
                       kmask_ref, dret_ref, gq_ref, gk_ref, aret_ref, ltri_ref, lmask_ref,
                       rn_ref, gn_ref, wout_ref, g2_ref, wg_ref, wu_ref, wd_ref, gf_ref,
                       out_ref, o_ref, act_ref, x2_ref, h2_ref, sr_ref, sg_ref, *, steps_per_seq):
    rows_per_step = x1_ref.shape[0]
    step = pl.program_id(0)

    @pl.when(step == 0)
    def _():
        x2_ref[...] = jnp.zeros_like(x2_ref)
        h2_ref[...] = jnp.zeros_like(h2_ref)

    @pl.when(step % steps_per_seq == 0)
    def _():
        sr_ref[...] = jnp.zeros_like(sr_ref)
        sg_ref[...] = jnp.zeros_like(sg_ref)

    def scores(ql, kl):
        out = []
        for col in range(QK_W // LANE):
            lanes = slice(col * LANE, (col + 1) * LANE)
            w = jnp.concatenate([kl[:, lanes]] * 2, axis=0).T * kmask_ref[...]
            out.append(_dot(ql[:, lanes], w))
        return jnp.concatenate(out, axis=1)

    def attend(s, q_in, v, state):
        s = s.astype(BF16)
        zero = jnp.zeros((CHUNK, HEAD_V), BF16)
        out = []
        for col in range(QK_W // LANE):
            lanes = slice(col * LANE, (col + 1) * LANE)
            rhs = []
            for h in (2 * col, 2 * col + 1):
                vh = v[:, h * HEAD_V:(h + 1) * HEAD_V]
                rhs.append([vh, zero] if h % 2 == 0 else [zero, vh])
            for h in (2 * col, 2 * col + 1):
                sh = state[h * HEAD_QK:(h + 1) * HEAD_QK, :].astype(BF16)
                rhs.append([sh, zero] if h % 2 == 0 else [zero, sh])
            rhs = jnp.concatenate([jnp.concatenate(r, axis=1) for r in rhs], axis=0)
            out.append(_dot(jnp.concatenate([s[:, lanes], q_in[:, lanes]], axis=1), rhs))
        return jnp.concatenate(out, axis=1)

    def state_update(k_out, v):
        k_t = k_out.T
        return jnp.concatenate([_dot(k_t[h * HEAD_QK:(h + 1) * HEAD_QK, :], v[:, h * HEAD_V:(h + 1) * HEAD_V])
                                for h in range(N_HEADS)], axis=0)

    def mixer_stages():
        chunk_rows = [pl.ds(c * CHUNK, CHUNK) for c in range(rows_per_step // CHUNK)]

        q_r = [proj_ref[r, RQ:RQ + QK_W] for r in chunk_rows]
        k_r = [proj_ref[r, RK:RK + QK_W] for r in chunk_rows]
        v_r = [proj_ref[r, RV:RV + V_W] for r in chunk_rows]
        s_r = [scores(q, k) * dret_ref[...] for q, k in zip(q_r, k_r)]
        qin_r = [(q.astype(F32) * gq_ref[...]).astype(BF16) for q in q_r]
        yield
        kv_r = [state_update((k.astype(F32) * gk_ref[...]).astype(BF16), v) for k, v in zip(k_r, v_r)]
        yield

        b = []
        for r in chunk_rows:
            la = la_ref[r, :]
            la_hi = la.astype(BF16)
            la_lo = (la - la_hi.astype(F32)).astype(BF16)
            b.append(_dot(ltri_ref[...], jnp.concatenate([la_hi, la_lo], axis=0)) * LOG2_E)
        b_last = [jnp.broadcast_to(bc[CHUNK - 1:CHUNK, :], (2 * CHUNK, QK_W)) for bc in b]
        a_col = [jnp.exp2(bl.T)[:, :LANE] for bl in b_last]
        yield
        q_g = [proj_ref[r, GQ:GQ + QK_W] for r in chunk_rows]
        k_g = [proj_ref[r, GK:GK + QK_W] for r in chunk_rows]
        v_g = [proj_ref[r, GV:GV + V_W] for r in chunk_rows]
        s_g = [scores(q, k) * lmask_ref[0] for q, k in zip(q_g, k_g)]
        for l, s in enumerate(GLA_LEVELS):
            for c in range(len(chunk_rows)):
                if c % 4 == 0:
                    yield
                f = jnp.exp2(-jnp.abs(b[c] - _midpoint_rows(b[c], s))).astype(BF16)
                s_g[c] = s_g[c] + scores(q_g[c] * f, k_g[c] * f) * lmask_ref[1 + l]
        yield
        qin_g = [q * jnp.exp2(bc).astype(BF16) for q, bc in zip(q_g, b)]
        kv_g = [state_update(k * jnp.exp2(bl[:CHUNK, :] - bc).astype(BF16), v)
                for k, bc, bl, v in zip(k_g, b, b_last, v_g)]

        sr = sr_ref[...]
        sg = sg_ref[...]
        for c, r in enumerate(chunk_rows):
            if c % 2 == 0:
                yield
            o = attend(s_r[c], qin_r[c], v_r[c], sr)
            sr = sr * aret_ref[...] + kv_r[c]
            gate = proj_ref[r, RG:RG + V_W].astype(F32)
            o_ref[r, 0:V_W] = _head_norm_gate(o, rn_ref[...], gate).astype(BF16)
            o = attend(s_g[c], qin_g[c], v_g[c], sg)
            sg = sg * a_col[c] + kv_g[c]
            gate = proj_ref[r, GG:GG + V_W].astype(F32)
            o_ref[r, V_W:2 * V_W] = _head_norm_gate(o, gn_ref[...], gate).astype(BF16)
        sr_ref[...] = sr
        sg_ref[...] = sg

    def finish_previous(rows):
        h = h2_ref[rows, :]
        for c in range(D_FF // MXU_N):
            cols = slice(c * MXU_N, (c + 1) * MXU_N)
            gate = _dot(h, wg_ref[:, cols])
            up = _dot(h, wu_ref[:, cols])
            act_ref[rows, cols] = (_silu(gate) * up).astype(BF16)
            yield
        y = _dot(act_ref[rows, :], wd_ref[...])
        out_ref[rows, :] = _rms(x2_ref[rows, :] + 0.5 * y, gf_ref[...])

    n_half = rows_per_step // 2
    mixer = mixer_stages()
    halves = [finish_previous(pl.ds(0, n_half)), finish_previous(pl.ds(n_half, n_half))]
    for n_mixer in MIXER_STAGES_PER_ROUND:
        for g in halves:
            next(g, None)
        for _ in range(n_mixer):
            next(mixer, None)
    for g in halves + [mixer]:
        for _ in g:
            pass
    for i in range(2):
        rows = pl.ds(i * n_half, n_half)
        x2 = x1_ref[rows, :] + _dot(o_ref[rows, :], wout_ref[...])
        x2_ref[rows, :] = x2
        h2_ref[rows, :] = _rms(x2, g2_ref[...]).astype(BF16)


def _gla_tables():
    t = np.arange(CHUNK)[:, None]
    u = np.arange(CHUNK)[None, :]
    masks = [np.eye(CHUNK, dtype=bool)]
    for s in GLA_LEVELS:
        same_block = (t // s) == (u // s)
        masks.append(same_block & (((t % s) < s // 2) != ((u % s) < s // 2)))
    ltri = (u <= t).astype(np.float32)
    ltri2 = np.concatenate([ltri, ltri], axis=1)
    lmask = np.stack([np.tile(m, (1, N_HEADS)) for m in masks]).astype(np.float32)
    return ltri2, lmask


def _retention_tables():
    gamma = 1.0 - 2.0 ** (-5.0 - np.arange(N_HEADS, dtype=np.float64))
    head_of_lane = np.arange(QK_W) // HEAD_QK
    i = np.arange(CHUNK)
    dist = np.abs(i[:, None] - i[None, :])
    dret = np.concatenate([gamma[h] ** dist for h in range(N_HEADS)], axis=1)
    gq = gamma[head_of_lane][None, :] ** (i[:, None] + 1.0)
    gk = gamma[head_of_lane][None, :] ** (CHUNK - 1.0 - i[:, None])
    aret = np.broadcast_to((gamma[head_of_lane] ** CHUNK)[:, None], (QK_W, HEAD_V))
    return tuple(np.asarray(a, np.float32) for a in (dret, gq, gk, aret))


def _key_mask():
    idx = np.arange(LANE) // HEAD_QK
    return idx[:, None] == idx[None, :]


def _const_spec(shape):
    nd = len(shape)
    return pl.BlockSpec(shape, lambda *_: (0,) * nd, pipeline_mode=pl.Buffered(1))


def _slab_spec(shape, n_steps):
    rows, cols = shape
    slab = next(s for s in range(BF16_SUBLANES, rows + 1, BF16_SUBLANES)
                if rows % s == 0 and n_steps % (rows // s) == 0 and rows // s <= n_steps)
    repeat = n_steps // (rows // slab)
    return pl.BlockSpec((slab, cols), lambda i: (i // repeat, 0))


def _tile_rows(total_rows):
    tile = 512
    assert total_rows % tile == 0
    return tile


@jax.jit
def kernel(x, ffn1_norm_g, ffn1_w_gate, ffn1_w_up, ffn1_w_down, mix_norm_g, w_in, ret_norm_g, gla_w_a2,
           gla_b_a, gla_norm_g, w_out, ffn2_norm_g, ffn2_w_gate, ffn2_w_up, ffn2_w_down, final_norm_g):
    bsz, seq, d = x.shape
    assert d == D_MODEL and seq % CHUNK == 0 and ffn1_norm_g.shape[0] == 1
    tile = _tile_rows(seq)
    steps = seq // tile
    n_rows = bsz * seq

    row = lambda g: g.reshape(1, -1).astype(F32)
    w_in_t = jnp.swapaxes(w_in, 1, 2)[0]
    w_low = jnp.pad(w_in_t[GLOW:, :].T, ((0, 0), (0, LANE - GATE_RANK))).astype(BF16)
    w_a2 = jnp.pad(gla_w_a2[0], ((0, LANE - GATE_RANK), (0, 0))).astype(BF16)

    half = HEAD_QK // 2
    pos = jnp.arange(seq, dtype=F32)
    inv = ROPE_BASE ** (-jnp.arange(half, dtype=F32) * 2.0 / HEAD_QK)
    ang = pos[:, None] * inv[None, :]
    sign = np.where((np.arange(LANE) % HEAD_QK) < half, -1.0, 1.0).astype(np.float32)
    cos = jnp.tile(jnp.cos(ang), (1, LANE // half))
    sin = jnp.tile(jnp.sin(ang), (1, LANE // half)) * sign[None, :]

    x2d = x.reshape(n_rows, d)
    n_tiles = n_rows // tile
    row_spec = lambda w: pl.BlockSpec((tile, w), lambda i: (i, 0))
    pos_spec = pl.BlockSpec((tile, LANE), lambda i: (i % steps, 0))
    hbm_spec = pl.BlockSpec(memory_space=pl.ANY)
    later = [w_out[0], ffn2_w_gate[0], ffn2_w_up[0], ffn2_w_down[0]]
    later_specs = [_slab_spec(w.shape, n_tiles) for w in later]
    x1, proj, la, w_out_b, wg2_b, wu2_b, wd2_b = pl.pallas_call(
        _ffn1_proj_kernel,
        grid=(n_tiles,),
        in_specs=[row_spec(d), pos_spec, pos_spec, _const_spec((1, d)),
                  hbm_spec, hbm_spec, hbm_spec, _const_spec((1, d)), hbm_spec,
                  _const_spec((d, LANE)), _const_spec((LANE, QK_W)), _const_spec((1, QK_W))] + later_specs,
        out_specs=[row_spec(d), row_spec(PROJ_W), row_spec(QK_W)] + later_specs,
        out_shape=[jax.ShapeDtypeStruct((n_rows, d), F32),
                   jax.ShapeDtypeStruct((n_rows, PROJ_W), BF16),
                   jax.ShapeDtypeStruct((n_rows, QK_W), F32)]
                  + [jax.ShapeDtypeStruct(w.shape, BF16) for w in later],
        scratch_shapes=[pltpu.VMEM((tile, D_FF), BF16),
                        pltpu.VMEM((d, D_FF), BF16), pltpu.VMEM((d, D_FF), BF16), pltpu.VMEM((D_FF, d), BF16),
                        pltpu.VMEM((PROJ_W, d), BF16),
                        pltpu.VMEM((WEIGHT_STAGE_SLOTS, WEIGHT_CHUNK_ROWS, D_FF), F32),
                        pltpu.SemaphoreType.DMA((WEIGHT_STAGE_SLOTS,))],
        compiler_params=pltpu.CompilerParams(dimension_semantics=("arbitrary",),
                                             vmem_limit_bytes=VMEM_LIMIT),
        name="ffn1_proj",
    )(x2d, cos, sin, row(ffn1_norm_g), ffn1_w_gate[0], ffn1_w_up[0], ffn1_w_down[0], row(mix_norm_g),
      w_in_t, w_low, w_a2, row(gla_b_a), *later)

    ltri2, lmask = _gla_tables()
    dret, gq, gk, aret = _retention_tables()
    consts = [jnp.asarray(_key_mask(), BF16), jnp.asarray(dret), jnp.asarray(gq),
              jnp.asarray(gk), jnp.asarray(aret), jnp.asarray(ltri2, BF16), jnp.asarray(lmask)]

    tok_spec = lambda w: pl.BlockSpec((tile, w), lambda i: (jnp.minimum(i, n_tiles - 1), 0))
    out = pl.pallas_call(
        functools.partial(_mixer_ffn2_kernel, steps_per_seq=steps),
        grid=(n_tiles + 1,),
        in_specs=[tok_spec(d), tok_spec(PROJ_W), tok_spec(QK_W)]
                 + [_const_spec(c.shape) for c in consts]
                 + [_const_spec((1, V_W)), _const_spec((1, V_W)), _const_spec((d, d)), _const_spec((1, d)),
                    _const_spec((d, D_FF)), _const_spec((d, D_FF)), _const_spec((D_FF, d)),
                    _const_spec((1, d))],
        out_specs=pl.BlockSpec((tile, d), lambda i: (jnp.maximum(i - 1, 0), 0)),
        out_shape=jax.ShapeDtypeStruct((n_rows, d), F32),
        scratch_shapes=[pltpu.VMEM((tile, 2 * V_W), BF16), pltpu.VMEM((tile, D_FF), BF16),
                        pltpu.VMEM((tile, d), F32), pltpu.VMEM((tile, d), BF16),
                        pltpu.VMEM((QK_W, HEAD_V), F32), pltpu.VMEM((QK_W, HEAD_V), F32)],
        compiler_params=pltpu.CompilerParams(dimension_semantics=("arbitrary",),
                                             vmem_limit_bytes=VMEM_LIMIT),
        name="mixer_ffn2",
    )(x1, proj, la, *consts, row(ret_norm_g), row(gla_norm_g), w_out_b, row(ffn2_norm_g),
      wg2_b, wu2_b, wd2_b, row(final_norm_g))
    return out.reshape(bsz, seq, d)
```

```python
import functools

import numpy as np
import jax
import jax.numpy as jnp
from jax import lax
from jax.experimental import pallas as pl
from jax.experimental.pallas import tpu as pltpu

D_MODEL = 1024
CHUNK = 64
RMS_EPS = 1e-6
ROPE_BASE = 10000.0
N_HEADS = 4
QK_W = 256
V_W = 512
HEAD_V = V_W // N_HEADS
HEAD_QK = QK_W // N_HEADS
GATE_RANK = 16
GATE_NORM = 16.0
D_FF = 2816
PROJ_W = 2 * (2 * QK_W + 2 * V_W)
RQ, RK, RV, RG = 0, 256, 512, 1024
GQ, GK, GV, GG = 1536, 1792, 2048, 2560
GLOW = 3072

LANE = 128
SUBLANES = 8
BF16_SUBLANES = 16
MXU_N = 256
WEIGHT_CHUNK_ROWS = 128
WEIGHT_STAGE_SLOTS = 4
VMEM_LIMIT = 56 * 1024 * 1024

GLA_LEVELS = (64, 32, 16, 8, 4, 2)
MIXER_STAGES_PER_ROUND = (2,) * 11
LOG2_E = 1.4426950408889634

BF16 = jnp.bfloat16
F32 = jnp.float32


def _dot(a, b):
    return jnp.dot(a, b, preferred_element_type=F32)


def _dot_nt(a, b):
    return lax.dot_general(a, b, (((1,), (1,)), ((), ())), preferred_element_type=F32)


def _rms(x, g):
    ms = jnp.mean(x * x, axis=-1, keepdims=True)
    return x * lax.rsqrt(ms + RMS_EPS) * g


def _silu(x):
    h = 0.5 * x
    return h * jnp.tanh(h) + h


def _ffn(h, wg_ref, wu_ref, wd_ref, act_ref, interleave=None):
    for c in range(D_FF // MXU_N):
        cols = slice(c * MXU_N, (c + 1) * MXU_N)
        gate = _dot(h, wg_ref[:, cols])
        up = _dot(h, wu_ref[:, cols])
        act_ref[:, cols] = (_silu(gate) * up).astype(BF16)
        if interleave is not None:
            next(interleave, None)
    if interleave is not None:
        for _ in interleave:
            pass
    return _dot(act_ref[...], wd_ref[...])


def _load_weight_as_bf16(src_hbm, dst_ref, stage_ref, sem):
    n_rows, n_cols = dst_ref.shape
    n_chunks = n_rows // WEIGHT_CHUNK_ROWS
    n_slots = stage_ref.shape[0]

    def chunk_copy(k):
        slot = k % n_slots
        return pltpu.make_async_copy(
            src_hbm.at[pl.ds(k * WEIGHT_CHUNK_ROWS, WEIGHT_CHUNK_ROWS), :],
            stage_ref.at[slot, :, pl.ds(0, n_cols)], sem.at[slot])

    for k in range(min(n_slots - 1, n_chunks)):
        chunk_copy(k).start()

    def body(k, carry):
        @pl.when(k + n_slots - 1 < n_chunks)
        def _():
            chunk_copy(k + n_slots - 1).start()

        chunk_copy(k).wait()
        rows = pl.ds(pl.multiple_of(k * WEIGHT_CHUNK_ROWS, WEIGHT_CHUNK_ROWS), WEIGHT_CHUNK_ROWS)
        dst_ref[rows, :] = stage_ref[k % n_slots, :, pl.ds(0, n_cols)].astype(BF16)
        return carry

    lax.fori_loop(0, n_chunks, body, 0)


def _ffn1_proj_kernel(x_ref, cos_ref, sin_ref, g1_ref, wg_hbm, wu_hbm, wd_hbm, gm_ref, win_hbm,
                      wa2_ref, ba_ref, *rest):
    later_f32, (x1_ref, proj_ref, la_ref), later_bf16 = rest[:4], rest[4:7], rest[7:11]
    act_ref, wg_ref, wu_ref, wd_ref, win_ref, wlow_ref, stage_ref, sem = rest[11:]

    @pl.when(pl.program_id(0) == 0)
    def _():
        for src, dst in ((wg_hbm, wg_ref), (wu_hbm, wu_ref), (wd_hbm, wd_ref), (win_hbm, win_ref)):
            _load_weight_as_bf16(src, dst, stage_ref, sem)
        n_cols = wlow_ref.shape[1]
        tail = pltpu.make_async_copy(win_hbm.at[pl.ds(GLOW, GATE_RANK), :],
                                     stage_ref.at[0, pl.ds(0, GATE_RANK), pl.ds(0, n_cols)], sem.at[0])
        tail.start()
        wlow_ref[...] = jnp.zeros_like(wlow_ref)
        tail.wait()
        wlow_ref[0:GATE_RANK, :] = stage_ref[0, 0:GATE_RANK, 0:n_cols].astype(BF16)

    def cast_later_weights():
        yield
        for src, dst in zip(later_f32, later_bf16):
            dst[...] = src[...].astype(BF16)

    def half_tile(rows):
        x = x_ref[rows, :]
        h = _rms(x, g1_ref[...]).astype(BF16)
        yield
        for c in range(D_FF // MXU_N):
            cols = slice(c * MXU_N, (c + 1) * MXU_N)
            gate = _dot(h, wg_ref[:, cols])
            up = _dot(h, wu_ref[:, cols])
            act_ref[rows, cols] = (_silu(gate) * up).astype(BF16)
        yield
        x1 = x + 0.5 * _dot(act_ref[rows, :], wd_ref[...])
        x1_ref[rows, :] = x1
        h = _rms(x1, gm_ref[...]).astype(BF16)
        yield
        cos = cos_ref[rows, :]
        sin = sin_ref[rows, :]
        first_half = (lax.broadcasted_iota(jnp.int32, cos.shape, 1) % HEAD_QK) < HEAD_QK // 2
        for off, scale in ((RQ, 1.0), (RK, HEAD_QK ** -0.5)):
            t = _dot_nt(h, win_ref[off:off + QK_W, :])
            for col in range(QK_W // LANE):
                tc = t[:, col * LANE:(col + 1) * LANE]
                partner = jnp.where(first_half, pltpu.roll(tc, LANE - HEAD_QK // 2, 1),
                                    pltpu.roll(tc, HEAD_QK // 2, 1))
                proj_ref[rows, off + col * LANE:off + (col + 1) * LANE] = (
                    (tc * cos + partner * sin) * scale).astype(BF16)
        proj_ref[rows, RV:GQ] = _dot_nt(h, win_ref[RV:GQ, :]).astype(BF16)
        proj_ref[rows, GQ:GK] = (_dot_nt(h, win_ref[GQ:GK, :]) * (HEAD_QK ** -0.5)).astype(BF16)
        proj_ref[rows, GK:PROJ_W] = _dot_nt(h, win_ref[GK:PROJ_W, :]).astype(BF16)
        low = _dot_nt(h, wlow_ref[...]).astype(BF16)
        yield
        z = _dot(low, wa2_ref[...]) + ba_ref[...]
        la_ref[rows, :] = (jnp.minimum(z, 0.0) - jnp.log(1.0 + jnp.exp(-jnp.abs(z)))) * (1.0 / GATE_NORM)

    n_half = x_ref.shape[0] // 2
    streams = [half_tile(pl.ds(i * n_half, n_half)) for i in range(2)] + [cast_later_weights()]
    while streams:
        streams = [g for g in streams if next(g, StopIteration) is not StopIteration]


def _head_norm_gate(o, norm_g, gate):
    outs = []
    for h in range(N_HEADS):
        cols = slice(h * HEAD_V, (h + 1) * HEAD_V)
        oh = o[:, cols]
        ms = jnp.mean(oh * oh, axis=-1, keepdims=True)
        outs.append(oh * lax.rsqrt(ms + RMS_EPS) * norm_g[:, cols] * _silu(gate[:, cols]))
    return jnp.concatenate(outs, axis=-1)


def _head_block_diag(blocks):
    zero = jnp.zeros_like(blocks[0])
    return jnp.concatenate(
        [jnp.concatenate([blk if j == h else zero for j in range(N_HEADS)], axis=1)
         for h, blk in enumerate(blocks)], axis=0)


def _state_block_diag(state):
    return _head_block_diag([state[h * HEAD_QK:(h + 1) * HEAD_QK, :].astype(BF16) for h in range(N_HEADS)])


def _value_block_diag(v):
    return _head_block_diag([v[:, h * HEAD_V:(h + 1) * HEAD_V] for h in range(N_HEADS)])


def _midpoint_rows(b, s):
    rows, w = b.shape
    if s >= SUBLANES:
        blocks = b.reshape(rows // s, s, w)
        return jnp.broadcast_to(blocks[:, s // 2 - 1:s // 2, :], blocks.shape).reshape(rows, w)
    groups = b.reshape(rows // SUBLANES, SUBLANES, w)
    sub = lax.broadcasted_iota(jnp.int32, groups.shape, 1)
    out = None
    for first in reversed(range(0, SUBLANES, s)):
        ref_row = jnp.broadcast_to(groups[:, first + s // 2 - 1:first + s // 2, :], groups.shape)
        out = ref_row if out is None else jnp.where(sub < first + s, ref_row, out)
    return out.reshape(rows, w)


def _mixer_ffn2_kernel(x1_ref, proj_ref, la_ref,
                       kmask_ref, dret_ref, gq_ref, gk_ref, aret_ref, ltri_ref, lmask_ref,
                       rn_ref, gn_ref, wout_ref, g2_ref, wg_ref, wu_ref, wd_ref, gf_ref,
                       out_ref, o_ref, act_ref, x2_ref, h2_ref, sr_ref, sg_ref, *, steps_per_seq):
    rows_per_step = x1_ref.shape[0]
    step = pl.program_id(0)

    @pl.when(step == 0)
    def _():
        x2_ref[...] = jnp.zeros_like(x2_ref)
        h2_ref[...] = jnp.zeros_like(h2_ref)

    @pl.when(step % steps_per_seq == 0)
    def _():
        sr_ref[...] = jnp.zeros_like(sr_ref)
        sg_ref[...] = jnp.zeros_like(sg_ref)

    def scores(ql, kl):
        out = []
        for col in range(QK_W // LANE):
            lanes = slice(col * LANE, (col + 1) * LANE)
            w = jnp.concatenate([kl[:, lanes]] * 2, axis=0).T * kmask_ref[...]
            out.append(_dot(ql[:, lanes], w))
        return jnp.concatenate(out, axis=1)

    def attend(s, q_in, v, state):
        s = s.astype(BF16)
        zero = jnp.zeros((CHUNK, HEAD_V), BF16)
        out = []
        for col in range(QK_W // LANE):
            lanes = slice(col * LANE, (col + 1) * LANE)
            rhs = []
            for h in (2 * col, 2 * col + 1):
                vh = v[:, h * HEAD_V:(h + 1) * HEAD_V]
                rhs.append([vh, zero] if h % 2 == 0 else [zero, vh])
            for h in (2 * col, 2 * col + 1):
                sh = state[h * HEAD_QK:(h + 1) * HEAD_QK, :].astype(BF16)
                rhs.append([sh, zero] if h % 2 == 0 else [zero, sh])
            rhs = jnp.concatenate([jnp.concatenate(r, axis=1) for r in rhs], axis=0)
            out.append(_dot(jnp.concatenate([s[:, lanes], q_in[:, lanes]], axis=1), rhs))
        return jnp.concatenate(out, axis=1)

    def state_update(k_out, v):
        k_t = k_out.T
        return jnp.concatenate([_dot(k_t[h * HEAD_QK:(h + 1) * HEAD_QK, :], v[:, h * HEAD_V:(h + 1) * HEAD_V])
                                for h in range(N_HEADS)], axis=0)

    def mixer_stages():
        chunk_rows = [pl.ds(c * CHUNK, CHUNK) for c in range(rows_per_step // CHUNK)]

        q_r = [proj_ref[r, RQ:RQ + QK_W] for r in chunk_rows]
        k_r = [proj_ref[r, RK:RK + QK_W] for r in chunk_rows]
        v_r = [proj_ref[r, RV:RV + V_W] for r in chunk_rows]
        s_r = [scores(q, k) * dret_ref[...] for q, k in zip(q_r, k_r)]
        qin_r = [(q.astype(F32) * gq_ref[...]).astype(BF16) for q in q_r]
        yield
        kv_r = [state_update((k.astype(F32) * gk_ref[...]).astype(BF16), v) for k, v in zip(k_r, v_r)]
        yield

        b = []
        for r in chunk_rows:
            la = la_ref[r, :]
            la_hi = la.astype(BF16)
            la_lo = (la - la_hi.astype(F32)).astype(BF16)
            b.append(_dot(ltri_ref[...], jnp.concatenate([la_hi, la_lo], axis=0)) * LOG2_E)
        b_last = [jnp.broadcast_to(bc[CHUNK - 1:CHUNK, :], (2 * CHUNK, QK_W)) for bc in b]
        a_col = [jnp.exp2(bl.T)[:, :LANE] for bl in b_last]
        yield
        q_g = [proj_ref[r, GQ:GQ + QK_W] for r in chunk_rows]
        k_g = [proj_ref[r, GK:GK + QK_W] for r in chunk_rows]
        v_g = [proj_ref[r, GV:GV + V_W] for r in chunk_rows]
        s_g = [scores(q, k) * lmask_ref[0] for q, k in zip(q_g, k_g)]
        for l, s in enumerate(GLA_LEVELS):
            for c in range(len(chunk_rows)):
                if c % 4 == 0:
                    yield
                f = jnp.exp2(-jnp.abs(b[c] - _midpoint_rows(b[c], s))).astype(BF16)
                s_g[c] = s_g[c] + scores(q_g[c] * f, k_g[c] * f) * lmask_ref[1 + l]
        yield
        qin_g = [q * jnp.exp2(bc).astype(BF16) for q, bc in zip(q_g, b)]
        kv_g = [state_update(k * jnp.exp2(bl[:CHUNK, :] - bc).astype(BF16), v)
                for k, bc, bl, v in zip(k_g, b, b_last, v_g)]

        sr = sr_ref[...]
        sg = sg_ref[...]
        for c, r in enumerate(chunk_rows):
            if c % 2 == 0:
                yield
            o = attend(s_r[c], qin_r[c], v_r[c], sr)
            sr = sr * aret_ref[...] + kv_r[c]
            gate = proj_ref[r, RG:RG + V_W].astype(F32)
            o_ref[r, 0:V_W] = _head_norm_gate(o, rn_ref[...], gate).astype(BF16)
            o = attend(s_g[c], qin_g[c], v_g[c], sg)
            sg = sg * a_col[c] + kv_g[c]
            gate = proj_ref[r, GG:GG + V_W].astype(F32)
            o_ref[r, V_W:2 * V_W] = _head_norm_gate(o, gn_ref[...], gate).astype(BF16)
        sr_ref[...] = sr
        sg_ref[...] = sg

    def finish_previous(rows):
        h = h2_ref[rows, :]
        for c in range(D_FF // MXU_N):
            cols = slice(c * MXU_N, (c + 1) * MXU_N)
            gate = _dot(h, wg_ref[:, cols])
            up = _dot(h, wu_ref[:, cols])
            act_ref[rows, cols] = (_silu(gate) * up).astype(BF16)
            yield
        y = _dot(act_ref[rows, :], wd_ref[...])
        out_ref[rows, :] = _rms(x2_ref[rows, :] + 0.5 * y, gf_ref[...])

    n_half = rows_per_step // 2
    mixer = mixer_stages()
    halves = [finish_previous(pl.ds(0, n_half)), finish_previous(pl.ds(n_half, n_half))]
    for n_mixer in MIXER_STAGES_PER_ROUND:
        for g in halves:
            next(g, None)
        for _ in range(n_mixer):
            next(mixer, None)
    for g in halves + [mixer]:
        for _ in g:
            pass
    for i in range(2):
        rows = pl.ds(i * n_half, n_half)
        x2 = x1_ref[rows, :] + _dot(o_ref[rows, :], wout_ref[...])
        x2_ref[rows, :] = x2
        h2_ref[rows, :] = _rms(x2, g2_ref[...]).astype(BF16)


def _gla_tables():
    t = np.arange(CHUNK)[:, None]
    u = np.arange(CHUNK)[None, :]
    masks = [np.eye(CHUNK, dtype=bool)]
    for s in GLA_LEVELS:
        same_block = (t // s) == (u // s)
        masks.append(same_block & (((t % s) < s // 2) != ((u % s) < s // 2)))
    ltri = (u <= t).astype(np.float32)
    ltri2 = np.concatenate([ltri, ltri], axis=1)
    lmask = np.stack([np.tile(m, (1, N_HEADS)) for m in masks]).astype(np.float32)
    return ltri2, lmask


def _retention_tables():
    gamma = 1.0 - 2.0 ** (-5.0 - np.arange(N_HEADS, dtype=np.float64))
    head_of_lane = np.arange(QK_W) // HEAD_QK
    i = np.arange(CHUNK)
    dist = np.abs(i[:, None] - i[None, :])
    dret = np.concatenate([gamma[h] ** dist for h in range(N_HEADS)], axis=1)
    gq = gamma[head_of_lane][None, :] ** (i[:, None] + 1.0)
    gk = gamma[head_of_lane][None, :] ** (CHUNK - 1.0 - i[:, None])
    aret = np.broadcast_to((gamma[head_of_lane] ** CHUNK)[:, None], (QK_W, HEAD_V))
    return tuple(np.asarray(a, np.float32) for a in (dret, gq, gk, aret))


def _key_mask():
    idx = np.arange(LANE) // HEAD_QK
    return idx[:, None] == idx[None, :]


def _const_spec(shape):
    nd = len(shape)
    return pl.BlockSpec(shape, lambda *_: (0,) * nd, pipeline_mode=pl.Buffered(1))


def _slab_spec(shape, n_steps):
    rows, cols = shape
    slab = next(s for s in range(BF16_SUBLANES, rows + 1, BF16_SUBLANES)
                if rows % s == 0 and n_steps % (rows // s) == 0 and rows // s <= n_steps)
    repeat = n_steps // (rows // slab)
    return pl.BlockSpec((slab, cols), lambda i: (i // repeat, 0))


def _tile_rows(total_rows):
    tile = 512
    assert total_rows % tile == 0
    return tile


@jax.jit
def kernel(x, ffn1_norm_g, ffn1_w_gate, ffn1_w_up, ffn1_w_down, mix_norm_g, w_in, ret_norm_g, gla_w_a2,
           gla_b_a, gla_norm_g, w_out, ffn2_norm_g, ffn2_w_gate, ffn2_w_up, ffn2_w_down, final_norm_g):
    bsz, seq, d = x.shape
    assert d == D_MODEL and seq % CHUNK == 0 and ffn1_norm_g.shape[0] == 1
    tile = _tile_rows(seq)
    steps = seq // tile
    n_rows = bsz * seq

    row = lambda g: g.reshape(1, -1).astype(F32)
    w_in_t = jnp.swapaxes(w_in, 1, 2)[0]
    w_a2 = jnp.pad(gla_w_a2[0], ((0, LANE - GATE_RANK), (0, 0))).astype(BF16)

    half = HEAD_QK // 2
    pos = jnp.arange(seq, dtype=F32)
    inv = ROPE_BASE ** (-jnp.arange(half, dtype=F32) * 2.0 / HEAD_QK)
    ang = pos[:, None] * inv[None, :]
    sign = np.where((np.arange(LANE) % HEAD_QK) < half, -1.0, 1.0).astype(np.float32)
    cos = jnp.tile(jnp.cos(ang), (1, LANE // half))
    sin = jnp.tile(jnp.sin(ang), (1, LANE // half)) * sign[None, :]

    x2d = x.reshape(n_rows, d)
    n_tiles = n_rows // tile
    row_spec = lambda w: pl.BlockSpec((tile, w), lambda i: (i, 0))
    pos_spec = pl.BlockSpec((tile, LANE), lambda i: (i % steps, 0))
    hbm_spec = pl.BlockSpec(memory_space=pl.ANY)
    later = [w_out[0], ffn2_w_gate[0], ffn2_w_up[0], ffn2_w_down[0]]
    later_specs = [_slab_spec(w.shape, n_tiles) for w in later]
    x1, proj, la, w_out_b, wg2_b, wu2_b, wd2_b = pl.pallas_call(
        _ffn1_proj_kernel,
        grid=(n_tiles,),
        in_specs=[row_spec(d), pos_spec, pos_spec, _const_spec((1, d)),
                  hbm_spec, hbm_spec, hbm_spec, _const_spec((1, d)), hbm_spec,
                  _const_spec((LANE, QK_W)), _const_spec((1, QK_W))] + later_specs,
        out_specs=[row_spec(d), row_spec(PROJ_W), row_spec(QK_W)] + later_specs,
        out_shape=[jax.ShapeDtypeStruct((n_rows, d), F32),
                   jax.ShapeDtypeStruct((n_rows, PROJ_W), BF16),
                   jax.ShapeDtypeStruct((n_rows, QK_W), F32)]
                  + [jax.ShapeDtypeStruct(w.shape, BF16) for w in later],
        scratch_shapes=[pltpu.VMEM((tile, D_FF), BF16),
                        pltpu.VMEM((d, D_FF), BF16), pltpu.VMEM((d, D_FF), BF16), pltpu.VMEM((D_FF, d), BF16),
                        pltpu.VMEM((PROJ_W, d), BF16), pltpu.VMEM((LANE, d), BF16),
                        pltpu.VMEM((WEIGHT_STAGE_SLOTS, WEIGHT_CHUNK_ROWS, D_FF), F32),
                        pltpu.SemaphoreType.DMA((WEIGHT_STAGE_SLOTS,))],
        compiler_params=pltpu.CompilerParams(dimension_semantics=("arbitrary",),
                                             vmem_limit_bytes=VMEM_LIMIT),
        name="ffn1_proj",
    )(x2d, cos, sin, row(ffn1_norm_g), ffn1_w_gate[0], ffn1_w_up[0], ffn1_w_down[0], row(mix_norm_g),
      w_in_t, w_a2, row(gla_b_a), *later)

    ltri2, lmask = _gla_tables()
    dret, gq, gk, aret = _retention_tables()
    consts = [jnp.asarray(_key_mask(), BF16), jnp.asarray(dret), jnp.asarray(gq),
              jnp.asarray(gk), jnp.asarray(aret), jnp.asarray(ltri2, BF16), jnp.asarray(lmask)]

    tok_spec = lambda w: pl.BlockSpec((tile, w), lambda i: (jnp.minimum(i, n_tiles - 1), 0))
    out = pl.pallas_call(
        functools.partial(_mixer_ffn2_kernel, steps_per_seq=steps),
        grid=(n_tiles + 1,),
        in_specs=[tok_spec(d), tok_spec(PROJ_W), tok_spec(QK_W)]
                 + [_const_spec(c.shape) for c in consts]
                 + [_const_spec((1, V_W)), _const_spec((1, V_W)), _const_spec((d, d)), _const_spec((1, d)),
                    _const_spec((d, D_FF)), _const_spec((d, D_FF)), _const_spec((D_FF, d)),
                    _const_spec((1, d))],
        out_specs=pl.BlockSpec((tile, d), lambda i: (jnp.maximum(i - 1, 0), 0)),
        out_shape=jax.ShapeDtypeStruct((n_rows, d), F32),
        scratch_shapes=[pltpu.VMEM((tile, 2 * V_W), BF16), pltpu.VMEM((tile, D_FF), BF16),
                        pltpu.VMEM((tile, d), F32), pltpu.VMEM((tile, d), BF16),
                        pltpu.VMEM((QK_W, HEAD_V), F32), pltpu.VMEM((QK_W, HEAD_V), F32)],
        compiler_params=pltpu.CompilerParams(dimension_semantics=("arbitrary",),
                                             vmem_limit_bytes=VMEM_LIMIT),
        name="mixer_ffn2",
    )(x1, proj, la, *consts, row(ret_norm_g), row(gla_norm_g), w_out_b, row(ffn2_norm_g),
      wg2_b, wu2_b, wd2_b, row(final_norm_g))
    return out.reshape(bsz, seq, d)
```

```python
import functools

import numpy as np
import jax
import jax.numpy as jnp
from jax import lax
from jax.experimental import pallas as pl
from jax.experimental.pallas import tpu as pltpu

D_MODEL = 1024
CHUNK = 64
RMS_EPS = 1e-6
ROPE_BASE = 10000.0
N_HEADS = 4
QK_W = 256
V_W = 512
HEAD_V = V_W // N_HEADS
HEAD_QK = QK_W // N_HEADS
GATE_RANK = 16
GATE_NORM = 16.0
D_FF = 2816
PROJ_W = 2 * (2 * QK_W + 2 * V_W)
RQ, RK, RV, RG = 0, 256, 512, 1024
GQ, GK, GV, GG = 1536, 1792, 2048, 2560
GLOW = 3072

LANE = 128
SUBLANES = 8
BF16_SUBLANES = 16
MXU_N = 256
WEIGHT_CHUNK_ROWS = 128
WEIGHT_STAGE_SLOTS = 4
VMEM_LIMIT = 56 * 1024 * 1024

GLA_LEVELS = (64, 32, 16, 8, 4, 2)
MIXER_STAGES_PER_ROUND = (2,) * 11
LOG2_E = 1.4426950408889634

BF16 = jnp.bfloat16
F32 = jnp.float32


def _dot(a, b):
    return jnp.dot(a, b, preferred_element_type=F32)


def _dot_nt(a, b):
    return lax.dot_general(a, b, (((1,), (1,)), ((), ())), preferred_element_type=F32)


def _rms(x, g):
    ms = jnp.mean(x * x, axis=-1, keepdims=True)
    return x * lax.rsqrt(ms + RMS_EPS) * g


def _silu(x):
    h = 0.5 * x
    return h * jnp.tanh(h) + h


def _load_weight_as_bf16(src_hbm, dst_ref, stage_ref, sem):
    n_rows, n_cols = dst_ref.shape
    assert n_rows % WEIGHT_CHUNK_ROWS == 0 and n_cols <= stage_ref.shape[2]
    n_chunks = n_rows // WEIGHT_CHUNK_ROWS
    n_slots = stage_ref.shape[0]

    def chunk_copy(k):
        slot = k % n_slots
        return pltpu.make_async_copy(
            src_hbm.at[pl.ds(k * WEIGHT_CHUNK_ROWS, WEIGHT_CHUNK_ROWS), :],
            stage_ref.at[slot, :, pl.ds(0, n_cols)], sem.at[slot])

    for k in range(min(n_slots - 1, n_chunks)):
        chunk_copy(k).start()

    def body(k, carry):
        @pl.when(k + n_slots - 1 < n_chunks)
        def _():
            chunk_copy(k + n_slots - 1).start()

        chunk_copy(k).wait()
        rows = pl.ds(pl.multiple_of(k * WEIGHT_CHUNK_ROWS, WEIGHT_CHUNK_ROWS), WEIGHT_CHUNK_ROWS)
        dst_ref[rows, :] = stage_ref[k % n_slots, :, pl.ds(0, n_cols)].astype(BF16)
        return carry

    lax.fori_loop(0, n_chunks, body, 0)


def _ffn1_proj_kernel(x_ref, cos_ref, sin_ref, g1_ref, wg_hbm, wu_hbm, wd_hbm, gm_ref, win_hbm,
                      wa2_ref, ba_ref, *rest):
    later_f32, (x1_ref, proj_ref, la_ref), later_bf16 = rest[:4], rest[4:7], rest[7:11]
    act_ref, wg_ref, wu_ref, wd_ref, win_ref, wlow_ref, stage_ref, sem = rest[11:]

    @pl.when(pl.program_id(0) == 0)
    def _():
        for src, dst in ((wg_hbm, wg_ref), (wu_hbm, wu_ref), (wd_hbm, wd_ref), (win_hbm, win_ref)):
            _load_weight_as_bf16(src, dst, stage_ref, sem)
        n_cols = wlow_ref.shape[1]
        tail = pltpu.make_async_copy(win_hbm.at[pl.ds(GLOW, GATE_RANK), :],
                                     stage_ref.at[0, pl.ds(0, GATE_RANK), pl.ds(0, n_cols)], sem.at[0])
        tail.start()
        wlow_ref[...] = jnp.zeros_like(wlow_ref)
        tail.wait()
        wlow_ref[0:GATE_RANK, :] = stage_ref[0, 0:GATE_RANK, 0:n_cols].astype(BF16)

    def cast_later_weights():
        yield
        for src, dst in zip(later_f32, later_bf16):
            dst[...] = src[...].astype(BF16)

    def half_tile(rows):
        x = x_ref[rows, :]
        h = _rms(x, g1_ref[...]).astype(BF16)
        yield
        for c in range(D_FF // MXU_N):
            cols = slice(c * MXU_N, (c + 1) * MXU_N)
            gate = _dot(h, wg_ref[:, cols])
            up = _dot(h, wu_ref[:, cols])
            act_ref[rows, cols] = (_silu(gate) * up).astype(BF16)
        yield
        x1 = x + 0.5 * _dot(act_ref[rows, :], wd_ref[...])
        x1_ref[rows, :] = x1
        h = _rms(x1, gm_ref[...]).astype(BF16)
        yield
        cos = cos_ref[rows, :]
        sin = sin_ref[rows, :]
        first_half = (lax.broadcasted_iota(jnp.int32, cos.shape, 1) % HEAD_QK) < HEAD_QK // 2
        for off, scale in ((RQ, 1.0), (RK, HEAD_QK ** -0.5)):
            t = _dot_nt(h, win_ref[off:off + QK_W, :])
            for col in range(QK_W // LANE):
                tc = t[:, col * LANE:(col + 1) * LANE]
                partner = jnp.where(first_half, pltpu.roll(tc, LANE - HEAD_QK // 2, 1),
                                    pltpu.roll(tc, HEAD_QK // 2, 1))
                proj_ref[rows, off + col * LANE:off + (col + 1) * LANE] = (
                    (tc * cos + partner * sin) * scale).astype(BF16)
        proj_ref[rows, RV:GQ] = _dot_nt(h, win_ref[RV:GQ, :]).astype(BF16)
        proj_ref[rows, GQ:GK] = (_dot_nt(h, win_ref[GQ:GK, :]) * (HEAD_QK ** -0.5)).astype(BF16)
        proj_ref[rows, GK:PROJ_W] = _dot_nt(h, win_ref[GK:PROJ_W, :]).astype(BF16)
        low = _dot_nt(h, wlow_ref[...]).astype(BF16)
        yield
        z = _dot(low, wa2_ref[...]) + ba_ref[...]
        la_ref[rows, :] = (jnp.minimum(z, 0.0) - jnp.log(1.0 + jnp.exp(-jnp.abs(z)))) * (1.0 / GATE_NORM)

    n_half = x_ref.shape[0] // 2
    streams = [half_tile(pl.ds(i * n_half, n_half)) for i in range(2)] + [cast_later_weights()]
    while streams:
        streams = [g for g in streams if next(g, StopIteration) is not StopIteration]


def _head_norm_gate(o, norm_g, gate):
    outs = []
    for h in range(N_HEADS):
        cols = slice(h * HEAD_V, (h + 1) * HEAD_V)
        oh = o[:, cols]
        ms = jnp.mean(oh * oh, axis=-1, keepdims=True)
        outs.append(oh * lax.rsqrt(ms + RMS_EPS) * norm_g[:, cols] * _silu(gate[:, cols]))
    return jnp.concatenate(outs, axis=-1)


def _midpoint_rows(b, s):
    rows, w = b.shape
    if s >= SUBLANES:
        blocks = b.reshape(rows // s, s, w)
        return jnp.broadcast_to(blocks[:, s // 2 - 1:s // 2, :], blocks.shape).reshape(rows, w)
    groups = b.reshape(rows // SUBLANES, SUBLANES, w)
    sub = lax.broadcasted_iota(jnp.int32, groups.shape, 1)
    out = None
    for first in reversed(range(0, SUBLANES, s)):
        ref_row = jnp.broadcast_to(groups[:, first + s // 2 - 1:first + s // 2, :], groups.shape)
        out = ref_row if out is None else jnp.where(sub < first + s, ref_row, out)
    return out.reshape(rows, w)


def _mixer_ffn2_kernel(x1_ref, proj_ref, la_ref,
                       kmask_ref, dret_ref, gq_ref, gk_ref, aret_ref, ltri_ref, lmask_ref,
                       rn_ref, gn_ref, wout_ref, g2_ref, wg_ref, wu_ref, wd_ref, gf_ref,
                       out_ref, o_ref, act_ref, x2_ref, h2_ref, sr_ref, sg_ref, *, steps_per_seq):
    rows_per_step = x1_ref.shape[0]
    step = pl.program_id(0)

    @pl.when(step == 0)
    def _():
        x2_ref[...] = jnp.zeros_like(x2_ref)
        h2_ref[...] = jnp.zeros_like(h2_ref)

    @pl.when(step % steps_per_seq == 0)
    def _():
        sr_ref[...] = jnp.zeros_like(sr_ref)
        sg_ref[...] = jnp.zeros_like(sg_ref)

    def scores(ql, kl):
        out = []
        for col in range(QK_W // LANE):
            lanes = slice(col * LANE, (col + 1) * LANE)
            w = jnp.concatenate([kl[:, lanes]] * 2, axis=0).T * kmask_ref[...]
            out.append(_dot(ql[:, lanes], w))
        return jnp.concatenate(out, axis=1)

    def attend(s, q_in, v, state):
        s = s.astype(BF16)
        zero = jnp.zeros((CHUNK, HEAD_V), BF16)
        out = []
        for col in range(QK_W // LANE):
            lanes = slice(col * LANE, (col + 1) * LANE)
            rhs = []
            for h in (2 * col, 2 * col + 1):
                vh = v[:, h * HEAD_V:(h + 1) * HEAD_V]
                rhs.append([vh, zero] if h % 2 == 0 else [zero, vh])
            for h in (2 * col, 2 * col + 1):
                sh = state[h * HEAD_QK:(h + 1) * HEAD_QK, :].astype(BF16)
                rhs.append([sh, zero] if h % 2 == 0 else [zero, sh])
            rhs = jnp.concatenate([jnp.concatenate(r, axis=1) for r in rhs], axis=0)
            out.append(_dot(jnp.concatenate([s[:, lanes], q_in[:, lanes]], axis=1), rhs))
        return jnp.concatenate(out, axis=1)

    def state_update(k_out, v):
        k_t = k_out.T
        return jnp.concatenate([_dot(k_t[h * HEAD_QK:(h + 1) * HEAD_QK, :], v[:, h * HEAD_V:(h + 1) * HEAD_V])
                                for h in range(N_HEADS)], axis=0)

    def mixer_stages():
        chunk_rows = [pl.ds(c * CHUNK, CHUNK) for c in range(rows_per_step // CHUNK)]

        q_r = [proj_ref[r, RQ:RQ + QK_W] for r in chunk_rows]
        k_r = [proj_ref[r, RK:RK + QK_W] for r in chunk_rows]
        v_r = [proj_ref[r, RV:RV + V_W] for r in chunk_rows]
        s_r = [scores(q, k) * dret_ref[...] for q, k in zip(q_r, k_r)]
        qin_r = [(q.astype(F32) * gq_ref[...]).astype(BF16) for q in q_r]
        yield
        kv_r = [state_update((k.astype(F32) * gk_ref[...]).astype(BF16), v) for k, v in zip(k_r, v_r)]
        yield

        b = []
        for r in chunk_rows:
            la = la_ref[r, :]
            la_hi = la.astype(BF16)
            la_lo = (la - la_hi.astype(F32)).astype(BF16)
            b.append(_dot(ltri_ref[...], jnp.concatenate([la_hi, la_lo], axis=0)) * LOG2_E)
        b_last = [jnp.broadcast_to(bc[CHUNK - 1:CHUNK, :], (2 * CHUNK, QK_W)) for bc in b]
        a_col = [jnp.exp2(bl.T)[:, :LANE] for bl in b_last]
        yield
        q_g = [proj_ref[r, GQ:GQ + QK_W] for r in chunk_rows]
        k_g = [proj_ref[r, GK:GK + QK_W] for r in chunk_rows]
        v_g = [proj_ref[r, GV:GV + V_W] for r in chunk_rows]
        s_g = [scores(q, k) * lmask_ref[0] for q, k in zip(q_g, k_g)]
        for l, s in enumerate(GLA_LEVELS):
            for c in range(len(chunk_rows)):
                if c % 4 == 0:
                    yield
                f = jnp.exp2(-jnp.abs(b[c] - _midpoint_rows(b[c], s))).astype(BF16)
                s_g[c] = s_g[c] + scores(q_g[c] * f, k_g[c] * f) * lmask_ref[1 + l]
        yield
        qin_g = [q * jnp.exp2(bc).astype(BF16) for q, bc in zip(q_g, b)]
        kv_g = [state_update(k * jnp.exp2(bl[:CHUNK, :] - bc).astype(BF16), v)
                for k, bc, bl, v in zip(k_g, b, b_last, v_g)]

        sr = sr_ref[...]
        sg = sg_ref[...]
        for c, r in enumerate(chunk_rows):
            if c % 2 == 0:
                yield
            o = attend(s_r[c], qin_r[c], v_r[c], sr)
            sr = sr * aret_ref[...] + kv_r[c]
            gate = proj_ref[r, RG:RG + V_W].astype(F32)
            o_ref[r, 0:V_W] = _head_norm_gate(o, rn_ref[...], gate).astype(BF16)
            o = attend(s_g[c], qin_g[c], v_g[c], sg)
            sg = sg * a_col[c] + kv_g[c]
            gate = proj_ref[r, GG:GG + V_W].astype(F32)
            o_ref[r, V_W:2 * V_W] = _head_norm_gate(o, gn_ref[...], gate).astype(BF16)
        sr_ref[...] = sr
        sg_ref[...] = sg

    def finish_previous(rows):
        h = h2_ref[rows, :]
        for c in range(D_FF // MXU_N):
            cols = slice(c * MXU_N, (c + 1) * MXU_N)
            gate = _dot(h, wg_ref[:, cols])
            up = _dot(h, wu_ref[:, cols])
            act_ref[rows, cols] = (_silu(gate) * up).astype(BF16)
            yield
        y = _dot(act_ref[rows, :], wd_ref[...])
        out_ref[rows, :] = _rms(x2_ref[rows, :] + 0.5 * y, gf_ref[...])

    n_half = rows_per_step // 2
    mixer = mixer_stages()
    halves = [finish_previous(pl.ds(0, n_half)), finish_previous(pl.ds(n_half, n_half))]
    for n_mixer in MIXER_STAGES_PER_ROUND:
        for g in halves:
            next(g, None)
        for _ in range(n_mixer):
            next(mixer, None)
    for g in halves + [mixer]:
        for _ in g:
            pass
    for i in range(2):
        rows = pl.ds(i * n_half, n_half)
        x2 = x1_ref[rows, :] + _dot(o_ref[rows, :], wout_ref[...])
        x2_ref[rows, :] = x2
        h2_ref[rows, :] = _rms(x2, g2_ref[...]).astype(BF16)


def _gla_tables():
    t = np.arange(CHUNK)[:, None]
    u = np.arange(CHUNK)[None, :]
    masks = [np.eye(CHUNK, dtype=bool)]
    for s in GLA_LEVELS:
        same_block = (t // s) == (u // s)
        masks.append(same_block & (((t % s) < s // 2) != ((u % s) < s // 2)))
    ltri = (u <= t).astype(np.float32)
    ltri2 = np.concatenate([ltri, ltri], axis=1)
    lmask = np.stack([np.tile(m, (1, N_HEADS)) for m in masks]).astype(np.float32)
    return ltri2, lmask


def _retention_tables():
    gamma = 1.0 - 2.0 ** (-5.0 - np.arange(N_HEADS, dtype=np.float64))
    head_of_lane = np.arange(QK_W) // HEAD_QK
    i = np.arange(CHUNK)
    dist = np.abs(i[:, None] - i[None, :])
    dret = np.concatenate([gamma[h] ** dist for h in range(N_HEADS)], axis=1)
    gq = gamma[head_of_lane][None, :] ** (i[:, None] + 1.0)
    gk = gamma[head_of_lane][None, :] ** (CHUNK - 1.0 - i[:, None])
    aret = np.broadcast_to((gamma[head_of_lane] ** CHUNK)[:, None], (QK_W, HEAD_V))
    return tuple(np.asarray(a, np.float32) for a in (dret, gq, gk, aret))


def _key_mask():
    idx = np.arange(LANE) // HEAD_QK
    return idx[:, None] == idx[None, :]


def _const_spec(shape):
    nd = len(shape)
    return pl.BlockSpec(shape, lambda *_: (0,) * nd, pipeline_mode=pl.Buffered(1))


def _slab_spec(shape, n_steps):
    rows, cols = shape
    slab = next(s for s in range(BF16_SUBLANES, rows + 1, BF16_SUBLANES)
                if rows % s == 0 and n_steps % (rows // s) == 0 and rows // s <= n_steps)
    repeat = n_steps // (rows // slab)
    return pl.BlockSpec((slab, cols), lambda i: (i // repeat, 0))


def _tile_rows(total_rows):
    tile = 512
    assert total_rows % tile == 0
    return tile


@jax.jit
def kernel(x, ffn1_norm_g, ffn1_w_gate, ffn1_w_up, ffn1_w_down, mix_norm_g, w_in, ret_norm_g, gla_w_a2,
           gla_b_a, gla_norm_g, w_out, ffn2_norm_g, ffn2_w_gate, ffn2_w_up, ffn2_w_down, final_norm_g):
    bsz, seq, d = x.shape
    assert d == D_MODEL and seq % CHUNK == 0 and ffn1_norm_g.shape[0] == 1
    tile = _tile_rows(seq)
    steps = seq // tile
    n_rows = bsz * seq

    row = lambda g: g.reshape(1, -1).astype(F32)
    w_in_t = jnp.swapaxes(w_in, 1, 2)[0]
    w_a2 = jnp.pad(gla_w_a2[0], ((0, LANE - GATE_RANK), (0, 0))).astype(BF16)

    half = HEAD_QK // 2
    pos = jnp.arange(seq, dtype=F32)
    inv = ROPE_BASE ** (-jnp.arange(half, dtype=F32) * 2.0 / HEAD_QK)
    ang = pos[:, None] * inv[None, :]
    sign = np.where((np.arange(LANE) % HEAD_QK) < half, -1.0, 1.0).astype(np.float32)
    cos = jnp.tile(jnp.cos(ang), (1, LANE // half))
    sin = jnp.tile(jnp.sin(ang), (1, LANE // half)) * sign[None, :]

    x2d = x.reshape(n_rows, d)
    n_tiles = n_rows // tile
    row_spec = lambda w: pl.BlockSpec((tile, w), lambda i: (i, 0))
    pos_spec = pl.BlockSpec((tile, LANE), lambda i: (i % steps, 0))
    hbm_spec = pl.BlockSpec(memory_space=pl.ANY)
    later = [w_out[0], ffn2_w_gate[0], ffn2_w_up[0], ffn2_w_down[0]]
    later_specs = [_slab_spec(w.shape, n_tiles) for w in later]
    x1, proj, la, w_out_b, wg2_b, wu2_b, wd2_b = pl.pallas_call(
        _ffn1_proj_kernel,
        grid=(n_tiles,),
        in_specs=[row_spec(d), pos_spec, pos_spec, _const_spec((1, d)),
                  hbm_spec, hbm_spec, hbm_spec, _const_spec((1, d)), hbm_spec,
                  _const_spec((LANE, QK_W)), _const_spec((1, QK_W))] + later_specs,
        out_specs=[row_spec(d), row_spec(PROJ_W), row_spec(QK_W)] + later_specs,
        out_shape=[jax.ShapeDtypeStruct((n_rows, d), F32),
                   jax.ShapeDtypeStruct((n_rows, PROJ_W), BF16),
                   jax.ShapeDtypeStruct((n_rows, QK_W), F32)]
                  + [jax.ShapeDtypeStruct(w.shape, BF16) for w in later],
        scratch_shapes=[pltpu.VMEM((tile, D_FF), BF16),
                        pltpu.VMEM((d, D_FF), BF16), pltpu.VMEM((d, D_FF), BF16), pltpu.VMEM((D_FF, d), BF16),
                        pltpu.VMEM((PROJ_W, d), BF16), pltpu.VMEM((LANE, d), BF16),
                        pltpu.VMEM((WEIGHT_STAGE_SLOTS, WEIGHT_CHUNK_ROWS, D_FF), F32),
                        pltpu.SemaphoreType.DMA((WEIGHT_STAGE_SLOTS,))],
        compiler_params=pltpu.CompilerParams(dimension_semantics=("arbitrary",),
                                             vmem_limit_bytes=VMEM_LIMIT),
        name="ffn1_proj",
    )(x2d, cos, sin, row(ffn1_norm_g), ffn1_w_gate[0], ffn1_w_up[0], ffn1_w_down[0], row(mix_norm_g),
      w_in_t, w_a2, row(gla_b_a), *later)

    ltri2, lmask = _gla_tables()
    dret, gq, gk, aret = _retention_tables()
    consts = [jnp.asarray(_key_mask(), BF16), jnp.asarray(dret), jnp.asarray(gq),
              jnp.asarray(gk), jnp.asarray(aret), jnp.asarray(ltri2, BF16), jnp.asarray(lmask)]

    tok_spec = lambda w: pl.BlockSpec((tile, w), lambda i: (jnp.minimum(i, n_tiles - 1), 0))
    out = pl.pallas_call(
        functools.partial(_mixer_ffn2_kernel, steps_per_seq=steps),
        grid=(n_tiles + 1,),
        in_specs=[tok_spec(d), tok_spec(PROJ_W), tok_spec(QK_W)]
                 + [_const_spec(c.shape) for c in consts]
                 + [_const_spec((1, V_W)), _const_spec((1, V_W)), _const_spec((d, d)), _const_spec((1, d)),
                    _const_spec((d, D_FF)), _const_spec((d, D_FF)), _const_spec((D_FF, d)),
                    _const_spec((1, d))],
        out_specs=pl.BlockSpec((tile, d), lambda i: (jnp.maximum(i - 1, 0), 0)),
        out_shape=jax.ShapeDtypeStruct((n_rows, d), F32),
        scratch_shapes=[pltpu.VMEM((tile, 2 * V_W), BF16), pltpu.VMEM((tile, D_FF), BF16),
                        pltpu.VMEM((tile, d), F32), pltpu.VMEM((tile, d), BF16),
                        pltpu.VMEM((QK_W, HEAD_V), F32), pltpu.VMEM((QK_W, HEAD_V), F32)],
        compiler_params=pltpu.CompilerParams(dimension_semantics=("arbitrary",),
                                             vmem_limit_bytes=VMEM_LIMIT),
        name="mixer_ffn2",
    )(x1, proj, la, *consts, row(ret_norm_g), row(gla_norm_g), w_out_b, row(ffn2_norm_g),
      wg2_b, wu2_b, wd2_b, row(final_norm_g))
    return out.reshape(bsz, seq, d)
```

```python
import functools

import numpy as np
import jax
import jax.numpy as jnp
from jax import lax
from jax.experimental import pallas as pl
from jax.experimental.pallas import tpu as pltpu

D_MODEL = 1024
CHUNK = 64
RMS_EPS = 1e-6
ROPE_BASE = 10000.0
N_HEADS = 4
QK_W = 256
V_W = 512
HEAD_V = V_W // N_HEADS
HEAD_QK = QK_W // N_HEADS
GATE_RANK = 16
GATE_NORM = 16.0
D_FF = 2816
PROJ_W = 2 * (2 * QK_W + 2 * V_W)
RQ, RK, RV, RG = 0, 256, 512, 1024
GQ, GK, GV, GG = 1536, 1792, 2048, 2560
GLOW = 3072

LANE = 128
SUBLANES = 8
BF16_SUBLANES = 16
MXU_N = 256
WEIGHT_CHUNK_ROWS = 128
WEIGHT_STAGE_SLOTS = 4
VMEM_LIMIT = 56 * 1024 * 1024

GLA_LEVELS = (64, 32, 16, 8, 4, 2)
MIXER_STAGES_PER_ROUND = (2,) * 11
LOG2_E = 1.4426950408889634

BF16 = jnp.bfloat16
F32 = jnp.float32


def _dot(a, b):
    return jnp.dot(a, b, preferred_element_type=F32)


def _dot_nt(a, b):
    return lax.dot_general(a, b, (((1,), (1,)), ((), ())), preferred_element_type=F32)


def _rms(x, g):
    ms = jnp.mean(x * x, axis=-1, keepdims=True)
    return x * lax.rsqrt(ms + RMS_EPS) * g


def _silu(x):
    h = 0.5 * x
    return h * jnp.tanh(h) + h


def _load_weight_as_bf16(src_hbm, dst_ref, stage_ref, sem):
    n_rows, n_cols = dst_ref.shape
    assert n_rows % WEIGHT_CHUNK_ROWS == 0 and n_cols <= stage_ref.shape[2]
    n_chunks = n_rows // WEIGHT_CHUNK_ROWS
    n_slots = stage_ref.shape[0]

    def chunk_copy(k):
        slot = k % n_slots
        return pltpu.make_async_copy(
            src_hbm.at[pl.ds(k * WEIGHT_CHUNK_ROWS, WEIGHT_CHUNK_ROWS), :],
            stage_ref.at[slot, :, pl.ds(0, n_cols)], sem.at[slot])

    for k in range(min(n_slots - 1, n_chunks)):
        chunk_copy(k).start()

    def body(k, carry):
        @pl.when(k + n_slots - 1 < n_chunks)
        def _():
            chunk_copy(k + n_slots - 1).start()

        chunk_copy(k).wait()
        rows = pl.ds(pl.multiple_of(k * WEIGHT_CHUNK_ROWS, WEIGHT_CHUNK_ROWS), WEIGHT_CHUNK_ROWS)
        dst_ref[rows, :] = stage_ref[k % n_slots, :, pl.ds(0, n_cols)].astype(BF16)
        return carry

    lax.fori_loop(0, n_chunks, body, 0)


def _ffn1_proj_kernel(x_ref, cos_ref, sin_ref, g1_ref, wg_hbm, wu_hbm, wd_hbm, gm_ref, win_hbm,
                      wa2_ref, ba_ref, *rest):
    later_f32, (x1_ref, proj_ref, la_ref), later_bf16 = rest[:4], rest[4:7], rest[7:11]
    act_ref, wg_ref, wu_ref, wd_ref, win_ref, wlow_ref, stage_ref, sem = rest[11:]

    @pl.when(pl.program_id(0) == 0)
    def _():
        for src, dst in ((wg_hbm, wg_ref), (wu_hbm, wu_ref), (wd_hbm, wd_ref), (win_hbm, win_ref)):
            _load_weight_as_bf16(src, dst, stage_ref, sem)
        n_cols = wlow_ref.shape[1]
        tail = pltpu.make_async_copy(win_hbm.at[pl.ds(GLOW, GATE_RANK), :],
                                     stage_ref.at[0, pl.ds(0, GATE_RANK), pl.ds(0, n_cols)], sem.at[0])
        tail.start()
        wlow_ref[...] = jnp.zeros_like(wlow_ref)
        tail.wait()
        wlow_ref[0:GATE_RANK, :] = stage_ref[0, 0:GATE_RANK, 0:n_cols].astype(BF16)

    def cast_later_weights():
        yield
        for src, dst in zip(later_f32, later_bf16):
            dst[...] = src[...].astype(BF16)

    def half_tile(rows):
        x = x_ref[rows, :]
        h = _rms(x, g1_ref[...]).astype(BF16)
        yield
        for c in range(D_FF // MXU_N):
            cols = slice(c * MXU_N, (c + 1) * MXU_N)
            gate = _dot(h, wg_ref[:, cols])
            up = _dot(h, wu_ref[:, cols])
            act_ref[rows, cols] = (_silu(gate) * up).astype(BF16)
        yield
        x1 = x + 0.5 * _dot(act_ref[rows, :], wd_ref[...])
        x1_ref[rows, :] = x1
        h = _rms(x1, gm_ref[...]).astype(BF16)
        yield
        cos = cos_ref[rows, :]
        sin = sin_ref[rows, :]
        first_half = (lax.broadcasted_iota(jnp.int32, cos.shape, 1) % HEAD_QK) < HEAD_QK // 2
        for off, scale in ((RQ, 1.0), (RK, HEAD_QK ** -0.5)):
            t = _dot_nt(h, win_ref[off:off + QK_W, :])
            for col in range(QK_W // LANE):
                tc = t[:, col * LANE:(col + 1) * LANE]
                partner = jnp.where(first_half, pltpu.roll(tc, LANE - HEAD_QK // 2, 1),
                                    pltpu.roll(tc, HEAD_QK // 2, 1))
                proj_ref[rows, off + col * LANE:off + (col + 1) * LANE] = (
                    (tc * cos + partner * sin) * scale).astype(BF16)
        proj_ref[rows, RV:GQ] = _dot_nt(h, win_ref[RV:GQ, :]).astype(BF16)
        proj_ref[rows, GQ:GK] = (_dot_nt(h, win_ref[GQ:GK, :]) * (HEAD_QK ** -0.5)).astype(BF16)
        proj_ref[rows, GK:PROJ_W] = _dot_nt(h, win_ref[GK:PROJ_W, :]).astype(BF16)
        low = _dot_nt(h, wlow_ref[...]).astype(BF16)
        yield
        z = _dot(low, wa2_ref[...]) + ba_ref[...]
        la_ref[rows, :] = (jnp.minimum(z, 0.0) - jnp.log(1.0 + jnp.exp(-jnp.abs(z)))) * (1.0 / GATE_NORM)

    n_half = x_ref.shape[0] // 2
    streams = [half_tile(pl.ds(i * n_half, n_half)) for i in range(2)] + [cast_later_weights()]
    while streams:
        streams = [g for g in streams if next(g, StopIteration) is not StopIteration]


def _head_norm_gate(o, norm_g, gate):
    outs = []
    for h in range(N_HEADS):
        cols = slice(h * HEAD_V, (h + 1) * HEAD_V)
        oh = o[:, cols]
        ms = jnp.mean(oh * oh, axis=-1, keepdims=True)
        outs.append(oh * lax.rsqrt(ms + RMS_EPS) * norm_g[:, cols] * _silu(gate[:, cols]))
    return jnp.concatenate(outs, axis=-1)


def _midpoint_rows(b, s):
    rows, w = b.shape
    if s >= SUBLANES:
        blocks = b.reshape(rows // s, s, w)
        return jnp.broadcast_to(blocks[:, s // 2 - 1:s // 2, :], blocks.shape).reshape(rows, w)
    groups = b.reshape(rows // SUBLANES, SUBLANES, w)
    sub = lax.broadcasted_iota(jnp.int32, groups.shape, 1)
    out = None
    for first in reversed(range(0, SUBLANES, s)):
        ref_row = jnp.broadcast_to(groups[:, first + s // 2 - 1:first + s // 2, :], groups.shape)
        out = ref_row if out is None else jnp.where(sub < first + s, ref_row, out)
    return out.reshape(rows, w)


def _mixer_ffn2_kernel(x1_ref, proj_ref, la_ref,
                       kmask_ref, dret_ref, gq_ref, gk_ref, aret_ref, ltri_ref, lmask_ref,
                       rn_ref, gn_ref, wout_ref, g2_ref, wg_ref, wu_ref, wd_ref, gf_ref,
                       out_ref, o_ref, act_ref, x2_ref, h2_ref, sr_ref, sg_ref, *, steps_per_seq):
    rows_per_step = x1_ref.shape[0]
    step = pl.program_id(0)

    @pl.when(step == 0)
    def _():
        x2_ref[...] = jnp.zeros_like(x2_ref)
        h2_ref[...] = jnp.zeros_like(h2_ref)

    @pl.when(step % steps_per_seq == 0)
    def _():
        sr_ref[...] = jnp.zeros_like(sr_ref)
        sg_ref[...] = jnp.zeros_like(sg_ref)

    def scores(ql, kl):
        out = []
        for col in range(QK_W // LANE):
            lanes = slice(col * LANE, (col + 1) * LANE)
            w = jnp.concatenate([kl[:, lanes]] * 2, axis=0).T * kmask_ref[...]
            out.append(_dot(ql[:, lanes], w))
        return jnp.concatenate(out, axis=1)

    def attend(s, q_in, v, state):
        s = s.astype(BF16)
        zero = jnp.zeros((CHUNK, HEAD_V), BF16)
        out = []
        for col in range(QK_W // LANE):
            lanes = slice(col * LANE, (col + 1) * LANE)
            rhs = []
            for h in (2 * col, 2 * col + 1):
                vh = v[:, h * HEAD_V:(h + 1) * HEAD_V]
                rhs.append([vh, zero] if h % 2 == 0 else [zero, vh])
            for h in (2 * col, 2 * col + 1):
                sh = state[h * HEAD_QK:(h + 1) * HEAD_QK, :].astype(BF16)
                rhs.append([sh, zero] if h % 2 == 0 else [zero, sh])
            rhs = jnp.concatenate([jnp.concatenate(r, axis=1) for r in rhs], axis=0)
            out.append(_dot(jnp.concatenate([s[:, lanes], q_in[:, lanes]], axis=1), rhs))
        return jnp.concatenate(out, axis=1)

    def state_update(k_out, v):
        k_t = k_out.T
        return jnp.concatenate([_dot(k_t[h * HEAD_QK:(h + 1) * HEAD_QK, :], v[:, h * HEAD_V:(h + 1) * HEAD_V])
                                for h in range(N_HEADS)], axis=0)

    def mixer_stages():
        chunk_rows = [pl.ds(c * CHUNK, CHUNK) for c in range(rows_per_step // CHUNK)]

        q_r = [proj_ref[r, RQ:RQ + QK_W] for r in chunk_rows]
        k_r = [proj_ref[r, RK:RK + QK_W] for r in chunk_rows]
        v_r = [proj_ref[r, RV:RV + V_W] for r in chunk_rows]
        s_r = [scores(q, k) * dret_ref[...] for q, k in zip(q_r, k_r)]
        qin_r = [(q.astype(F32) * gq_ref[...]).astype(BF16) for q in q_r]
        yield
        kv_r = [state_update((k.astype(F32) * gk_ref[...]).astype(BF16), v) for k, v in zip(k_r, v_r)]
        yield

        b = []
        for r in chunk_rows:
            la = la_ref[r, :]
            la_hi = la.astype(BF16)
            la_lo = (la - la_hi.astype(F32)).astype(BF16)
            b.append(_dot(ltri_ref[...], jnp.concatenate([la_hi, la_lo], axis=0)) * LOG2_E)
        b_last = [jnp.broadcast_to(bc[CHUNK - 1:CHUNK, :], (2 * CHUNK, QK_W)) for bc in b]
        a_col = [jnp.exp2(bl.T)[:, :LANE] for bl in b_last]
        yield
        q_g = [proj_ref[r, GQ:GQ + QK_W] for r in chunk_rows]
        k_g = [proj_ref[r, GK:GK + QK_W] for r in chunk_rows]
        v_g = [proj_ref[r, GV:GV + V_W] for r in chunk_rows]
        s_g = [scores(q, k) * lmask_ref[0] for q, k in zip(q_g, k_g)]
        for l, s in enumerate(GLA_LEVELS):
            for c in range(len(chunk_rows)):
                if c % 4 == 0:
                    yield
                f = jnp.exp2(-jnp.abs(b[c] - _midpoint_rows(b[c], s))).astype(BF16)
                s_g[c] = s_g[c] + scores(q_g[c] * f, k_g[c] * f) * lmask_ref[1 + l]
        yield
        qin_g = [q * jnp.exp2(bc).astype(BF16) for q, bc in zip(q_g, b)]
        kv_g = [state_update(k * jnp.exp2(bl[:CHUNK, :] - bc).astype(BF16), v)
                for k, bc, bl, v in zip(k_g, b, b_last, v_g)]

        sr = sr_ref[...]
        sg = sg_ref[...]
        for c, r in enumerate(chunk_rows):
            if c % 2 == 0:
                yield
            o = attend(s_r[c], qin_r[c], v_r[c], sr)
            sr = sr * aret_ref[...] + kv_r[c]
            gate = proj_ref[r, RG:RG + V_W].astype(F32)
            o_ref[r, 0:V_W] = _head_norm_gate(o, rn_ref[...], gate).astype(BF16)
            o = attend(s_g[c], qin_g[c], v_g[c], sg)
            sg = sg * a_col[c] + kv_g[c]
            gate = proj_ref[r, GG:GG + V_W].astype(F32)
            o_ref[r, V_W:2 * V_W] = _head_norm_gate(o, gn_ref[...], gate).astype(BF16)
        sr_ref[...] = sr
        sg_ref[...] = sg

    def finish_previous(rows):
        h = h2_ref[rows, :]
        for c in range(D_FF // MXU_N):
            cols = slice(c * MXU_N, (c + 1) * MXU_N)
            gate = _dot(h, wg_ref[:, cols])
            up = _dot(h, wu_ref[:, cols])
            act_ref[rows, cols] = (_silu(gate) * up).astype(BF16)
            yield
        y = _dot(act_ref[rows, :], wd_ref[...])
        out_ref[rows, :] = _rms(x2_ref[rows, :] + 0.5 * y, gf_ref[...])

    n_half = rows_per_step // 2
    mixer = mixer_stages()
    halves = [finish_previous(pl.ds(0, n_half)), finish_previous(pl.ds(n_half, n_half))]
    for n_mixer in MIXER_STAGES_PER_ROUND:
        for g in halves:
            next(g, None)
        for _ in range(n_mixer):
            next(mixer, None)
    for g in halves + [mixer]:
        for _ in g:
            pass
    for i in range(2):
        rows = pl.ds(i * n_half, n_half)
        x2 = x1_ref[rows, :] + _dot(o_ref[rows, :], wout_ref[...])
        x2_ref[rows, :] = x2
        h2_ref[rows, :] = _rms(x2, g2_ref[...]).astype(BF16)


def _gla_tables():
    t = np.arange(CHUNK)[:, None]
    u = np.arange(CHUNK)[None, :]
    masks = [np.eye(CHUNK, dtype=bool)]
    for s in GLA_LEVELS:
        same_block = (t // s) == (u // s)
        masks.append(same_block & (((t % s) < s // 2) != ((u % s) < s // 2)))
    ltri = (u <= t).astype(np.float32)
    ltri2 = np.concatenate([ltri, ltri], axis=1)
    lmask = np.stack([np.tile(m, (1, N_HEADS)) for m in masks]).astype(np.float32)
    return ltri2, lmask


def _retention_tables():
    gamma = 1.0 - 2.0 ** (-5.0 - np.arange(N_HEADS, dtype=np.float64))
    head_of_lane = np.arange(QK_W) // HEAD_QK
    i = np.arange(CHUNK)
    dist = np.abs(i[:, None] - i[None, :])
    dret = np.concatenate([gamma[h] ** dist for h in range(N_HEADS)], axis=1)
    gq = gamma[head_of_lane][None, :] ** (i[:, None] + 1.0)
    gk = gamma[head_of_lane][None, :] ** (CHUNK - 1.0 - i[:, None])
    aret = np.broadcast_to((gamma[head_of_lane] ** CHUNK)[:, None], (QK_W, HEAD_V))
    return tuple(np.asarray(a, np.float32) for a in (dret, gq, gk, aret))


def _rotary_tables(seq):
    half = HEAD_QK // 2
    inv = ROPE_BASE ** (-np.arange(half, dtype=np.float64) * 2.0 / HEAD_QK)
    ang = np.arange(seq, dtype=np.float64)[:, None] * inv[None, :]
    sign = np.where((np.arange(LANE) % HEAD_QK) < half, -1.0, 1.0)
    cos = np.tile(np.cos(ang), (1, LANE // half))
    sin = np.tile(np.sin(ang), (1, LANE // half)) * sign[None, :]
    return cos.astype(np.float32), sin.astype(np.float32)


def _key_mask():
    idx = np.arange(LANE) // HEAD_QK
    return idx[:, None] == idx[None, :]


def _const_spec(shape):
    nd = len(shape)
    return pl.BlockSpec(shape, lambda *_: (0,) * nd, pipeline_mode=pl.Buffered(1))


def _slab_spec(shape, n_steps):
    rows, cols = shape
    slab = next(s for s in range(BF16_SUBLANES, rows + 1, BF16_SUBLANES)
                if rows % s == 0 and n_steps % (rows // s) == 0 and rows // s <= n_steps)
    repeat = n_steps // (rows // slab)
    return pl.BlockSpec((slab, cols), lambda i: (i // repeat, 0))


def _tile_rows(total_rows):
    tile = 512
    assert total_rows % tile == 0
    return tile


@jax.jit
def kernel(x, ffn1_norm_g, ffn1_w_gate, ffn1_w_up, ffn1_w_down, mix_norm_g, w_in, ret_norm_g, gla_w_a2,
           gla_b_a, gla_norm_g, w_out, ffn2_norm_g, ffn2_w_gate, ffn2_w_up, ffn2_w_down, final_norm_g):
    bsz, seq, d = x.shape
    assert d == D_MODEL and seq % CHUNK == 0 and ffn1_norm_g.shape[0] == 1
    tile = _tile_rows(seq)
    steps = seq // tile
    n_rows = bsz * seq

    row = lambda g: g.reshape(1, -1).astype(F32)
    w_in_t = jnp.swapaxes(w_in, 1, 2)[0]
    w_a2 = jnp.pad(gla_w_a2[0], ((0, LANE - GATE_RANK), (0, 0))).astype(BF16)

    cos, sin = _rotary_tables(seq)

    x2d = x.reshape(n_rows, d)
    n_tiles = n_rows // tile
    row_spec = lambda w: pl.BlockSpec((tile, w), lambda i: (i, 0))
    pos_spec = pl.BlockSpec((tile, LANE), lambda i: (i % steps, 0))
    hbm_spec = pl.BlockSpec(memory_space=pl.ANY)
    later = [w_out[0], ffn2_w_gate[0], ffn2_w_up[0], ffn2_w_down[0]]
    later_specs = [_slab_spec(w.shape, n_tiles) for w in later]
    x1, proj, la, w_out_b, wg2_b, wu2_b, wd2_b = pl.pallas_call(
        _ffn1_proj_kernel,
        grid=(n_tiles,),
        in_specs=[row_spec(d), pos_spec, pos_spec, _const_spec((1, d)),
                  hbm_spec, hbm_spec, hbm_spec, _const_spec((1, d)), hbm_spec,
                  _const_spec((LANE, QK_W)), _const_spec((1, QK_W))] + later_specs,
        out_specs=[row_spec(d), row_spec(PROJ_W), row_spec(QK_W)] + later_specs,
        out_shape=[jax.ShapeDtypeStruct((n_rows, d), F32),
                   jax.ShapeDtypeStruct((n_rows, PROJ_W), BF16),
                   jax.ShapeDtypeStruct((n_rows, QK_W), F32)]
                  + [jax.ShapeDtypeStruct(w.shape, BF16) for w in later],
        scratch_shapes=[pltpu.VMEM((tile, D_FF), BF16),
                        pltpu.VMEM((d, D_FF), BF16), pltpu.VMEM((d, D_FF), BF16), pltpu.VMEM((D_FF, d), BF16),
                        pltpu.VMEM((PROJ_W, d), BF16), pltpu.VMEM((LANE, d), BF16),
                        pltpu.VMEM((WEIGHT_STAGE_SLOTS, WEIGHT_CHUNK_ROWS, D_FF), F32),
                        pltpu.SemaphoreType.DMA((WEIGHT_STAGE_SLOTS,))],
        compiler_params=pltpu.CompilerParams(dimension_semantics=("arbitrary",),
                                             vmem_limit_bytes=VMEM_LIMIT),
        name="ffn1_proj",
    )(x2d, cos, sin, row(ffn1_norm_g), ffn1_w_gate[0], ffn1_w_up[0], ffn1_w_down[0], row(mix_norm_g),
      w_in_t, w_a2, row(gla_b_a), *later)

    ltri2, lmask = _gla_tables()
    dret, gq, gk, aret = _retention_tables()
    consts = [jnp.asarray(_key_mask(), BF16), jnp.asarray(dret), jnp.asarray(gq),
              jnp.asarray(gk), jnp.asarray(aret), jnp.asarray(ltri2, BF16), jnp.asarray(lmask)]

    tok_spec = lambda w: pl.BlockSpec((tile, w), lambda i: (jnp.minimum(i, n_tiles - 1), 0))
    out = pl.pallas_call(
        functools.partial(_mixer_ffn2_kernel, steps_per_seq=steps),
        grid=(n_tiles + 1,),
        in_specs=[tok_spec(d), tok_spec(PROJ_W), tok_spec(QK_W)]
                 + [_const_spec(c.shape) for c in consts]
                 + [_const_spec((1, V_W)), _const_spec((1, V_W)), _const_spec((d, d)), _const_spec((1, d)),
                    _const_spec((d, D_FF)), _const_spec((d, D_FF)), _const_spec((D_FF, d)),
                    _const_spec((1, d))],
        out_specs=pl.BlockSpec((tile, d), lambda i: (jnp.maximum(i - 1, 0), 0)),
        out_shape=jax.ShapeDtypeStruct((n_rows, d), F32),
        scratch_shapes=[pltpu.VMEM((tile, 2 * V_W), BF16), pltpu.VMEM((tile, D_FF), BF16),
                        pltpu.VMEM((tile, d), F32), pltpu.VMEM((tile, d), BF16),
                        pltpu.VMEM((QK_W, HEAD_V), F32), pltpu.VMEM((QK_W, HEAD_V), F32)],
        compiler_params=pltpu.CompilerParams(dimension_semantics=("arbitrary",),
                                             vmem_limit_bytes=VMEM_LIMIT),
        name="mixer_ffn2",
    )(x1, proj, la, *consts, row(ret_norm_g), row(gla_norm_g), w_out_b, row(ffn2_norm_g),
      wg2_b, wu2_b, wd2_b, row(final_norm_g))
    return out.reshape(bsz, seq, d)
```

```python
import functools

import numpy as np
import jax
import jax.numpy as jnp
from jax import lax
from jax.experimental import pallas as pl
from jax.experimental.pallas import tpu as pltpu

D_MODEL = 1024
CHUNK = 64
RMS_EPS = 1e-6
ROPE_BASE = 10000.0
N_HEADS = 4
QK_W = 256
V_W = 512
HEAD_V = V_W // N_HEADS
HEAD_QK = QK_W // N_HEADS
GATE_RANK = 16
GATE_NORM = 16.0
D_FF = 2816
PROJ_W = 2 * (2 * QK_W + 2 * V_W)
RQ, RK, RV, RG = 0, 256, 512, 1024
GQ, GK, GV, GG = 1536, 1792, 2048, 2560
GLOW = 3072

LANE = 128
SUBLANES = 8
BF16_SUBLANES = 16
MXU_N = 256
FFN_STREAM_ROWS = 256
MIXER_TILE_ROWS = 256
WEIGHT_CHUNK_ROWS = 128
WEIGHT_STAGE_SLOTS = 4
VMEM_LIMIT = 56 * 1024 * 1024

GLA_LEVELS = (64, 32, 16, 8, 4, 2)
MIXER_STAGES_PER_ROUND = (2,) * 11
LOG2_E = 1.4426950408889634

BF16 = jnp.bfloat16
F32 = jnp.float32


def _dot(a, b):
    return jnp.dot(a, b, preferred_element_type=F32)


def _dot_nt(a, b):
    return lax.dot_general(a, b, (((1,), (1,)), ((), ())), preferred_element_type=F32)


def _rms(x, g):
    ms = jnp.mean(x * x, axis=-1, keepdims=True)
    return x * lax.rsqrt(ms + RMS_EPS) * g


def _silu(x):
    h = 0.5 * x
    return h * jnp.tanh(h) + h


def _load_weight_as_bf16(src_hbm, dst_ref, stage_ref, sem):
    n_rows, n_cols = dst_ref.shape
    assert n_rows % WEIGHT_CHUNK_ROWS == 0 and n_cols <= stage_ref.shape[2]
    n_chunks = n_rows // WEIGHT_CHUNK_ROWS
    n_slots = stage_ref.shape[0]

    def chunk_copy(k):
        slot = k % n_slots
        return pltpu.make_async_copy(
            src_hbm.at[pl.ds(k * WEIGHT_CHUNK_ROWS, WEIGHT_CHUNK_ROWS), :],
            stage_ref.at[slot, :, pl.ds(0, n_cols)], sem.at[slot])

    for k in range(min(n_slots - 1, n_chunks)):
        chunk_copy(k).start()

    def body(k, carry):
        @pl.when(k + n_slots - 1 < n_chunks)
        def _():
            chunk_copy(k + n_slots - 1).start()

        chunk_copy(k).wait()
        rows = pl.ds(pl.multiple_of(k * WEIGHT_CHUNK_ROWS, WEIGHT_CHUNK_ROWS), WEIGHT_CHUNK_ROWS)
        dst_ref[rows, :] = stage_ref[k % n_slots, :, pl.ds(0, n_cols)].astype(BF16)
        return carry

    lax.fori_loop(0, n_chunks, body, 0)


def _ffn1_proj_kernel(x_ref, cos_ref, sin_ref, g1_ref, wg_hbm, wu_hbm, wd_hbm, gm_ref, win_hbm,
                      wa2_ref, ba_ref, *rest):
    later_f32, (x1_ref, proj_ref, la_ref), later_bf16 = rest[:4], rest[4:7], rest[7:11]
    act_ref, wg_ref, wu_ref, wd_ref, win_ref, wlow_ref, stage_ref, sem = rest[11:]

    @pl.when(pl.program_id(0) == 0)
    def _():
        for src, dst in ((wg_hbm, wg_ref), (wu_hbm, wu_ref), (wd_hbm, wd_ref), (win_hbm, win_ref)):
            _load_weight_as_bf16(src, dst, stage_ref, sem)
        n_cols = wlow_ref.shape[1]
        tail = pltpu.make_async_copy(win_hbm.at[pl.ds(GLOW, GATE_RANK), :],
                                     stage_ref.at[0, pl.ds(0, GATE_RANK), pl.ds(0, n_cols)], sem.at[0])
        tail.start()
        wlow_ref[...] = jnp.zeros_like(wlow_ref)
        tail.wait()
        wlow_ref[0:GATE_RANK, :] = stage_ref[0, 0:GATE_RANK, 0:n_cols].astype(BF16)

    def cast_later_weights():
        yield
        for src, dst in zip(later_f32, later_bf16):
            dst[...] = src[...].astype(BF16)

    def half_tile(rows):
        x = x_ref[rows, :]
        h = _rms(x, g1_ref[...]).astype(BF16)
        yield
        for c in range(D_FF // MXU_N):
            cols = slice(c * MXU_N, (c + 1) * MXU_N)
            gate = _dot(h, wg_ref[:, cols])
            up = _dot(h, wu_ref[:, cols])
            act_ref[rows, cols] = (_silu(gate) * up).astype(BF16)
        yield
        x1 = x + 0.5 * _dot(act_ref[rows, :], wd_ref[...])
        x1_ref[rows, :] = x1
        h = _rms(x1, gm_ref[...]).astype(BF16)
        yield
        cos = cos_ref[rows, :]
        sin = sin_ref[rows, :]
        first_half = (lax.broadcasted_iota(jnp.int32, cos.shape, 1) % HEAD_QK) < HEAD_QK // 2
        for off, scale in ((RQ, 1.0), (RK, HEAD_QK ** -0.5)):
            t = _dot_nt(h, win_ref[off:off + QK_W, :])
            for col in range(QK_W // LANE):
                tc = t[:, col * LANE:(col + 1) * LANE]
                partner = jnp.where(first_half, pltpu.roll(tc, LANE - HEAD_QK // 2, 1),
                                    pltpu.roll(tc, HEAD_QK // 2, 1))
                proj_ref[rows, off + col * LANE:off + (col + 1) * LANE] = (
                    (tc * cos + partner * sin) * scale).astype(BF16)
        proj_ref[rows, RV:GQ] = _dot_nt(h, win_ref[RV:GQ, :]).astype(BF16)
        proj_ref[rows, GQ:GK] = (_dot_nt(h, win_ref[GQ:GK, :]) * (HEAD_QK ** -0.5)).astype(BF16)
        proj_ref[rows, GK:PROJ_W] = _dot_nt(h, win_ref[GK:PROJ_W, :]).astype(BF16)
        low = _dot_nt(h, wlow_ref[...]).astype(BF16)
        yield
        z = _dot(low, wa2_ref[...]) + ba_ref[...]
        la_ref[rows, :] = (jnp.minimum(z, 0.0) - jnp.log(1.0 + jnp.exp(-jnp.abs(z)))) * (1.0 / GATE_NORM)

    n_half = x_ref.shape[0] // 2
    streams = [half_tile(pl.ds(i * n_half, n_half)) for i in range(2)] + [cast_later_weights()]
    while streams:
        streams = [g for g in streams if next(g, StopIteration) is not StopIteration]


def _head_norm_gate(o, norm_g, gate):
    outs = []
    for h in range(N_HEADS):
        cols = slice(h * HEAD_V, (h + 1) * HEAD_V)
        oh = o[:, cols]
        ms = jnp.mean(oh * oh, axis=-1, keepdims=True)
        outs.append(oh * lax.rsqrt(ms + RMS_EPS) * norm_g[:, cols] * _silu(gate[:, cols]))
    return jnp.concatenate(outs, axis=-1)


def _midpoint_rows(b, s):
    rows, w = b.shape
    if s >= SUBLANES:
        blocks = b.reshape(rows // s, s, w)
        return jnp.broadcast_to(blocks[:, s // 2 - 1:s // 2, :], blocks.shape).reshape(rows, w)
    groups = b.reshape(rows // SUBLANES, SUBLANES, w)
    sub = lax.broadcasted_iota(jnp.int32, groups.shape, 1)
    out = None
    for first in reversed(range(0, SUBLANES, s)):
        ref_row = jnp.broadcast_to(groups[:, first + s // 2 - 1:first + s // 2, :], groups.shape)
        out = ref_row if out is None else jnp.where(sub < first + s, ref_row, out)
    return out.reshape(rows, w)


def _mixer_ffn2_kernel(x1_ref, proj_ref, la_ref,
                       kmask_ref, dret_ref, gq_ref, gk_ref, aret_ref, ltri_ref, lmask_ref,
                       rn_ref, gn_ref, wout_ref, g2_ref, wg_ref, wu_ref, wd_ref, gf_ref,
                       out_ref, o_ref, act_ref, x2_ref, h2_ref, sr_ref, sg_ref, *, steps_per_seq):
    rows_per_step = x1_ref.shape[0]
    step = pl.program_id(0)

    @pl.when(step == 0)
    def _():
        x2_ref[...] = jnp.zeros_like(x2_ref)
        h2_ref[...] = jnp.zeros_like(h2_ref)

    @pl.when(step % steps_per_seq == 0)
    def _():
        sr_ref[...] = jnp.zeros_like(sr_ref)
        sg_ref[...] = jnp.zeros_like(sg_ref)

    def scores(ql, kl):
        out = []
        for col in range(QK_W // LANE):
            lanes = slice(col * LANE, (col + 1) * LANE)
            w = jnp.concatenate([kl[:, lanes]] * 2, axis=0).T * kmask_ref[...]
            out.append(_dot(ql[:, lanes], w))
        return jnp.concatenate(out, axis=1)

    def attend(s, q_in, v, state):
        s = s.astype(BF16)
        zero = jnp.zeros((CHUNK, HEAD_V), BF16)
        out = []
        for col in range(QK_W // LANE):
            lanes = slice(col * LANE, (col + 1) * LANE)
            rhs = []
            for h in (2 * col, 2 * col + 1):
                vh = v[:, h * HEAD_V:(h + 1) * HEAD_V]
                rhs.append([vh, zero] if h % 2 == 0 else [zero, vh])
            for h in (2 * col, 2 * col + 1):
                sh = state[h * HEAD_QK:(h + 1) * HEAD_QK, :].astype(BF16)
                rhs.append([sh, zero] if h % 2 == 0 else [zero, sh])
            rhs = jnp.concatenate([jnp.concatenate(r, axis=1) for r in rhs], axis=0)
            out.append(_dot(jnp.concatenate([s[:, lanes], q_in[:, lanes]], axis=1), rhs))
        return jnp.concatenate(out, axis=1)

    def state_update(k_out, v):
        k_t = k_out.T
        return jnp.concatenate([_dot(k_t[h * HEAD_QK:(h + 1) * HEAD_QK, :], v[:, h * HEAD_V:(h + 1) * HEAD_V])
                                for h in range(N_HEADS)], axis=0)

    def mixer_stages():
        chunk_rows = [pl.ds(c * CHUNK, CHUNK) for c in range(rows_per_step // CHUNK)]

        q_r = [proj_ref[r, RQ:RQ + QK_W] for r in chunk_rows]
        k_r = [proj_ref[r, RK:RK + QK_W] for r in chunk_rows]
        v_r = [proj_ref[r, RV:RV + V_W] for r in chunk_rows]
        s_r = [scores(q, k) * dret_ref[...] for q, k in zip(q_r, k_r)]
        qin_r = [(q.astype(F32) * gq_ref[...]).astype(BF16) for q in q_r]
        yield
        kv_r = [state_update((k.astype(F32) * gk_ref[...]).astype(BF16), v) for k, v in zip(k_r, v_r)]
        yield

        b = []
        for r in chunk_rows:
            la = la_ref[r, :]
            la_hi = la.astype(BF16)
            la_lo = (la - la_hi.astype(F32)).astype(BF16)
            b.append(_dot(ltri_ref[...], jnp.concatenate([la_hi, la_lo], axis=0)) * LOG2_E)
        b_last = [jnp.broadcast_to(bc[CHUNK - 1:CHUNK, :], (2 * CHUNK, QK_W)) for bc in b]
        a_col = [jnp.exp2(bl.T)[:, :LANE] for bl in b_last]
        yield
        q_g = [proj_ref[r, GQ:GQ + QK_W] for r in chunk_rows]
        k_g = [proj_ref[r, GK:GK + QK_W] for r in chunk_rows]
        v_g = [proj_ref[r, GV:GV + V_W] for r in chunk_rows]
        s_g = [scores(q, k) * lmask_ref[0] for q, k in zip(q_g, k_g)]
        for l, s in enumerate(GLA_LEVELS):
            for c in range(len(chunk_rows)):
                if c % 4 == 0:
                    yield
                f = jnp.exp2(-jnp.abs(b[c] - _midpoint_rows(b[c], s))).astype(BF16)
                s_g[c] = s_g[c] + scores(q_g[c] * f, k_g[c] * f) * lmask_ref[1 + l]
        yield
        qin_g = [q * jnp.exp2(bc).astype(BF16) for q, bc in zip(q_g, b)]
        kv_g = [state_update(k * jnp.exp2(bl[:CHUNK, :] - bc).astype(BF16), v)
                for k, bc, bl, v in zip(k_g, b, b_last, v_g)]

        sr = sr_ref[...]
        sg = sg_ref[...]
        for c, r in enumerate(chunk_rows):
            if c % 2 == 0:
                yield
            o = attend(s_r[c], qin_r[c], v_r[c], sr)
            sr = sr * aret_ref[...] + kv_r[c]
            gate = proj_ref[r, RG:RG + V_W].astype(F32)
            o_ref[r, 0:V_W] = _head_norm_gate(o, rn_ref[...], gate).astype(BF16)
            o = attend(s_g[c], qin_g[c], v_g[c], sg)
            sg = sg * a_col[c] + kv_g[c]
            gate = proj_ref[r, GG:GG + V_W].astype(F32)
            o_ref[r, V_W:2 * V_W] = _head_norm_gate(o, gn_ref[...], gate).astype(BF16)
        sr_ref[...] = sr
        sg_ref[...] = sg

    def finish_previous(rows):
        h = h2_ref[rows, :]
        for c in range(D_FF // MXU_N):
            cols = slice(c * MXU_N, (c + 1) * MXU_N)
            gate = _dot(h, wg_ref[:, cols])
            up = _dot(h, wu_ref[:, cols])
            act_ref[rows, cols] = (_silu(gate) * up).astype(BF16)
            yield
        y = _dot(act_ref[rows, :], wd_ref[...])
        out_ref[rows, :] = _rms(x2_ref[rows, :] + 0.5 * y, gf_ref[...])

    n_half = FFN_STREAM_ROWS
    mixer = mixer_stages()
    halves = [finish_previous(pl.ds(i * n_half, n_half)) for i in range(rows_per_step // n_half)]
    for n_mixer in MIXER_STAGES_PER_ROUND:
        for g in halves:
            next(g, None)
        for _ in range(n_mixer):
            next(mixer, None)
    for g in halves + [mixer]:
        for _ in g:
            pass
    for i in range(rows_per_step // n_half):
        rows = pl.ds(i * n_half, n_half)
        x2 = x1_ref[rows, :] + _dot(o_ref[rows, :], wout_ref[...])
        x2_ref[rows, :] = x2
        h2_ref[rows, :] = _rms(x2, g2_ref[...]).astype(BF16)


def _gla_tables():
    t = np.arange(CHUNK)[:, None]
    u = np.arange(CHUNK)[None, :]
    masks = [np.eye(CHUNK, dtype=bool)]
    for s in GLA_LEVELS:
        same_block = (t // s) == (u // s)
        masks.append(same_block & (((t % s) < s // 2) != ((u % s) < s // 2)))
    ltri = (u <= t).astype(np.float32)
    ltri2 = np.concatenate([ltri, ltri], axis=1)
    lmask = np.stack([np.tile(m, (1, N_HEADS)) for m in masks]).astype(np.float32)
    return ltri2, lmask


def _retention_tables():
    gamma = 1.0 - 2.0 ** (-5.0 - np.arange(N_HEADS, dtype=np.float64))
    head_of_lane = np.arange(QK_W) // HEAD_QK
    i = np.arange(CHUNK)
    dist = np.abs(i[:, None] - i[None, :])
    dret = np.concatenate([gamma[h] ** dist for h in range(N_HEADS)], axis=1)
    gq = gamma[head_of_lane][None, :] ** (i[:, None] + 1.0)
    gk = gamma[head_of_lane][None, :] ** (CHUNK - 1.0 - i[:, None])
    aret = np.broadcast_to((gamma[head_of_lane] ** CHUNK)[:, None], (QK_W, HEAD_V))
    return tuple(np.asarray(a, np.float32) for a in (dret, gq, gk, aret))


def _rotary_tables(seq):
    half = HEAD_QK // 2
    inv = ROPE_BASE ** (-np.arange(half, dtype=np.float64) * 2.0 / HEAD_QK)
    ang = np.arange(seq, dtype=np.float64)[:, None] * inv[None, :]
    sign = np.where((np.arange(LANE) % HEAD_QK) < half, -1.0, 1.0)
    cos = np.tile(np.cos(ang), (1, LANE // half))
    sin = np.tile(np.sin(ang), (1, LANE // half)) * sign[None, :]
    return cos.astype(np.float32), sin.astype(np.float32)


def _key_mask():
    idx = np.arange(LANE) // HEAD_QK
    return idx[:, None] == idx[None, :]


def _const_spec(shape):
    nd = len(shape)
    return pl.BlockSpec(shape, lambda *_: (0,) * nd, pipeline_mode=pl.Buffered(1))


def _slab_spec(shape, n_steps):
    rows, cols = shape
    slab = next(s for s in range(BF16_SUBLANES, rows + 1, BF16_SUBLANES)
                if rows % s == 0 and n_steps % (rows // s) == 0 and rows // s <= n_steps)
    repeat = n_steps // (rows // slab)
    return pl.BlockSpec((slab, cols), lambda i: (i // repeat, 0))


def _tile_rows(total_rows):
    tile = 512
    assert total_rows % tile == 0
    return tile


@jax.jit
def kernel(x, ffn1_norm_g, ffn1_w_gate, ffn1_w_up, ffn1_w_down, mix_norm_g, w_in, ret_norm_g, gla_w_a2,
           gla_b_a, gla_norm_g, w_out, ffn2_norm_g, ffn2_w_gate, ffn2_w_up, ffn2_w_down, final_norm_g):
    bsz, seq, d = x.shape
    assert d == D_MODEL and seq % CHUNK == 0 and ffn1_norm_g.shape[0] == 1
    tile = _tile_rows(seq)
    steps = seq // tile
    n_rows = bsz * seq

    row = lambda g: g.reshape(1, -1).astype(F32)
    w_in_t = jnp.swapaxes(w_in, 1, 2)[0]
    w_a2 = jnp.pad(gla_w_a2[0], ((0, LANE - GATE_RANK), (0, 0))).astype(BF16)

    cos, sin = _rotary_tables(seq)

    x2d = x.reshape(n_rows, d)
    n_tiles = n_rows // tile
    row_spec = lambda w: pl.BlockSpec((tile, w), lambda i: (i, 0))
    pos_spec = pl.BlockSpec((tile, LANE), lambda i: (i % steps, 0))
    hbm_spec = pl.BlockSpec(memory_space=pl.ANY)
    later = [w_out[0], ffn2_w_gate[0], ffn2_w_up[0], ffn2_w_down[0]]
    later_specs = [_slab_spec(w.shape, n_tiles) for w in later]
    x1, proj, la, w_out_b, wg2_b, wu2_b, wd2_b = pl.pallas_call(
        _ffn1_proj_kernel,
        grid=(n_tiles,),
        in_specs=[row_spec(d), pos_spec, pos_spec, _const_spec((1, d)),
                  hbm_spec, hbm_spec, hbm_spec, _const_spec((1, d)), hbm_spec,
                  _const_spec((LANE, QK_W)), _const_spec((1, QK_W))] + later_specs,
        out_specs=[row_spec(d), row_spec(PROJ_W), row_spec(QK_W)] + later_specs,
        out_shape=[jax.ShapeDtypeStruct((n_rows, d), F32),
                   jax.ShapeDtypeStruct((n_rows, PROJ_W), BF16),
                   jax.ShapeDtypeStruct((n_rows, QK_W), F32)]
                  + [jax.ShapeDtypeStruct(w.shape, BF16) for w in later],
        scratch_shapes=[pltpu.VMEM((tile, D_FF), BF16),
                        pltpu.VMEM((d, D_FF), BF16), pltpu.VMEM((d, D_FF), BF16), pltpu.VMEM((D_FF, d), BF16),
                        pltpu.VMEM((PROJ_W, d), BF16), pltpu.VMEM((LANE, d), BF16),
                        pltpu.VMEM((WEIGHT_STAGE_SLOTS, WEIGHT_CHUNK_ROWS, D_FF), F32),
                        pltpu.SemaphoreType.DMA((WEIGHT_STAGE_SLOTS,))],
        compiler_params=pltpu.CompilerParams(dimension_semantics=("arbitrary",),
                                             vmem_limit_bytes=VMEM_LIMIT),
        name="ffn1_proj",
    )(x2d, cos, sin, row(ffn1_norm_g), ffn1_w_gate[0], ffn1_w_up[0], ffn1_w_down[0], row(mix_norm_g),
      w_in_t, w_a2, row(gla_b_a), *later)

    ltri2, lmask = _gla_tables()
    dret, gq, gk, aret = _retention_tables()
    consts = [jnp.asarray(_key_mask(), BF16), jnp.asarray(dret), jnp.asarray(gq),
              jnp.asarray(gk), jnp.asarray(aret), jnp.asarray(ltri2, BF16), jnp.asarray(lmask)]

    tile = MIXER_TILE_ROWS
    assert seq % tile == 0
    n_tiles = n_rows // tile
    tok_spec = lambda w: pl.BlockSpec((tile, w), lambda i: (jnp.minimum(i, n_tiles - 1), 0))
    out = pl.pallas_call(
        functools.partial(_mixer_ffn2_kernel, steps_per_seq=seq // tile),
        grid=(n_tiles + 1,),
        in_specs=[tok_spec(d), tok_spec(PROJ_W), tok_spec(QK_W)]
                 + [_const_spec(c.shape) for c in consts]
                 + [_const_spec((1, V_W)), _const_spec((1, V_W)), _const_spec((d, d)), _const_spec((1, d)),
                    _const_spec((d, D_FF)), _const_spec((d, D_FF)), _const_spec((D_FF, d)),
                    _const_spec((1, d))],
        out_specs=pl.BlockSpec((tile, d), lambda i: (jnp.maximum(i - 1, 0), 0)),
        out_shape=jax.ShapeDtypeStruct((n_rows, d), F32),
        scratch_shapes=[pltpu.VMEM((tile, 2 * V_W), BF16), pltpu.VMEM((tile, D_FF), BF16),
                        pltpu.VMEM((tile, d), F32), pltpu.VMEM((tile, d), BF16),
                        pltpu.VMEM((QK_W, HEAD_V), F32), pltpu.VMEM((QK_W, HEAD_V), F32)],
        compiler_params=pltpu.CompilerParams(dimension_semantics=("arbitrary",),
                                             vmem_limit_bytes=VMEM_LIMIT),
        name="mixer_ffn2",
    )(x1, proj, la, *consts, row(ret_norm_g), row(gla_norm_g), w_out_b, row(ffn2_norm_g),
      wg2_b, wu2_b, wd2_b, row(final_norm_g))
    return out.reshape(bsz, seq, d)
```

```python
import functools

import numpy as np
import jax
import jax.numpy as jnp
from jax import lax
from jax.experimental import pallas as pl
from jax.experimental.pallas import tpu as pltpu

D_MODEL = 1024
CHUNK = 64
RMS_EPS = 1e-6
ROPE_BASE = 10000.0
N_HEADS = 4
QK_W = 256
V_W = 512
HEAD_V = V_W // N_HEADS
HEAD_QK = QK_W // N_HEADS
GATE_RANK = 16
GATE_NORM = 16.0
D_FF = 2816
PROJ_W = 2 * (2 * QK_W + 2 * V_W)
RQ, RK, RV, RG = 0, 256, 512, 1024
GQ, GK, GV, GG = 1536, 1792, 2048, 2560
GLOW = 3072

LANE = 128
SUBLANES = 8
BF16_SUBLANES = 16
MXU_N = 256
FFN_STREAM_ROWS = 256
MIXER_TILE_ROWS = 512
WEIGHT_CHUNK_ROWS = 128
WEIGHT_STAGE_SLOTS = 4
VMEM_LIMIT = 56 * 1024 * 1024

GLA_LEVELS = (64, 32, 16, 8, 4, 2)
MIXER_GROUP_CHUNKS = 4
MIXER_STAGES_PER_ROUND = (3, 2) * 5 + (2,)
LOG2_E = 1.4426950408889634

BF16 = jnp.bfloat16
F32 = jnp.float32


def _dot(a, b):
    return jnp.dot(a, b, preferred_element_type=F32)


def _dot_nt(a, b):
    return lax.dot_general(a, b, (((1,), (1,)), ((), ())), preferred_element_type=F32)


def _rms(x, g):
    ms = jnp.mean(x * x, axis=-1, keepdims=True)
    return x * lax.rsqrt(ms + RMS_EPS) * g


def _silu(x):
    h = 0.5 * x
    return h * jnp.tanh(h) + h


def _load_weight_as_bf16(src_hbm, dst_ref, stage_ref, sem):
    n_rows, n_cols = dst_ref.shape
    assert n_rows % WEIGHT_CHUNK_ROWS == 0 and n_cols <= stage_ref.shape[2]
    n_chunks = n_rows // WEIGHT_CHUNK_ROWS
    n_slots = stage_ref.shape[0]

    def chunk_copy(k):
        slot = k % n_slots
        return pltpu.make_async_copy(
            src_hbm.at[pl.ds(k * WEIGHT_CHUNK_ROWS, WEIGHT_CHUNK_ROWS), :],
            stage_ref.at[slot, :, pl.ds(0, n_cols)], sem.at[slot])

    for k in range(min(n_slots - 1, n_chunks)):
        chunk_copy(k).start()

    def body(k, carry):
        @pl.when(k + n_slots - 1 < n_chunks)
        def _():
            chunk_copy(k + n_slots - 1).start()

        chunk_copy(k).wait()
        rows = pl.ds(pl.multiple_of(k * WEIGHT_CHUNK_ROWS, WEIGHT_CHUNK_ROWS), WEIGHT_CHUNK_ROWS)
        dst_ref[rows, :] = stage_ref[k % n_slots, :, pl.ds(0, n_cols)].astype(BF16)
        return carry

    lax.fori_loop(0, n_chunks, body, 0)


def _ffn1_proj_kernel(x_ref, cos_ref, sin_ref, g1_ref, wg_hbm, wu_hbm, wd_hbm, gm_ref, win_hbm,
                      wa2_ref, ba_ref, *rest):
    later_f32, (x1_ref, proj_ref, la_ref), later_bf16 = rest[:4], rest[4:7], rest[7:11]
    act_ref, wg_ref, wu_ref, wd_ref, win_ref, wlow_ref, stage_ref, sem = rest[11:]

    @pl.when(pl.program_id(0) == 0)
    def _():
        for src, dst in ((wg_hbm, wg_ref), (wu_hbm, wu_ref), (wd_hbm, wd_ref), (win_hbm, win_ref)):
            _load_weight_as_bf16(src, dst, stage_ref, sem)
        n_cols = wlow_ref.shape[1]
        tail = pltpu.make_async_copy(win_hbm.at[pl.ds(GLOW, GATE_RANK), :],
                                     stage_ref.at[0, pl.ds(0, GATE_RANK), pl.ds(0, n_cols)], sem.at[0])
        tail.start()
        wlow_ref[...] = jnp.zeros_like(wlow_ref)
        tail.wait()
        wlow_ref[0:GATE_RANK, :] = stage_ref[0, 0:GATE_RANK, 0:n_cols].astype(BF16)

    def cast_later_weights():
        yield
        for src, dst in zip(later_f32, later_bf16):
            dst[...] = src[...].astype(BF16)

    def half_tile(rows):
        x = x_ref[rows, :]
        h = _rms(x, g1_ref[...]).astype(BF16)
        yield
        for c in range(D_FF // MXU_N):
            cols = slice(c * MXU_N, (c + 1) * MXU_N)
            gate = _dot(h, wg_ref[:, cols])
            up = _dot(h, wu_ref[:, cols])
            act_ref[rows, cols] = (_silu(gate) * up).astype(BF16)
        yield
        x1 = x + 0.5 * _dot(act_ref[rows, :], wd_ref[...])
        x1_ref[rows, :] = x1
        h = _rms(x1, gm_ref[...]).astype(BF16)
        yield
        cos = cos_ref[rows, :]
        sin = sin_ref[rows, :]
        first_half = (lax.broadcasted_iota(jnp.int32, cos.shape, 1) % HEAD_QK) < HEAD_QK // 2
        for off, scale in ((RQ, 1.0), (RK, HEAD_QK ** -0.5)):
            t = _dot_nt(h, win_ref[off:off + QK_W, :])
            for col in range(QK_W // LANE):
                tc = t[:, col * LANE:(col + 1) * LANE]
                partner = jnp.where(first_half, pltpu.roll(tc, LANE - HEAD_QK // 2, 1),
                                    pltpu.roll(tc, HEAD_QK // 2, 1))
                proj_ref[rows, off + col * LANE:off + (col + 1) * LANE] = (
                    (tc * cos + partner * sin) * scale).astype(BF16)
        proj_ref[rows, RV:GQ] = _dot_nt(h, win_ref[RV:GQ, :]).astype(BF16)
        proj_ref[rows, GQ:GK] = (_dot_nt(h, win_ref[GQ:GK, :]) * (HEAD_QK ** -0.5)).astype(BF16)
        proj_ref[rows, GK:PROJ_W] = _dot_nt(h, win_ref[GK:PROJ_W, :]).astype(BF16)
        low = _dot_nt(h, wlow_ref[...]).astype(BF16)
        yield
        z = _dot(low, wa2_ref[...]) + ba_ref[...]
        la_ref[rows, :] = (jnp.minimum(z, 0.0) - jnp.log(1.0 + jnp.exp(-jnp.abs(z)))) * (1.0 / GATE_NORM)

    n_half = x_ref.shape[0] // 2
    streams = [half_tile(pl.ds(i * n_half, n_half)) for i in range(2)] + [cast_later_weights()]
    while streams:
        streams = [g for g in streams if next(g, StopIteration) is not StopIteration]


def _head_norm_gate(o, norm_g, gate):
    outs = []
    for h in range(N_HEADS):
        cols = slice(h * HEAD_V, (h + 1) * HEAD_V)
        oh = o[:, cols]
        ms = jnp.mean(oh * oh, axis=-1, keepdims=True)
        outs.append(oh * lax.rsqrt(ms + RMS_EPS) * norm_g[:, cols] * _silu(gate[:, cols]))
    return jnp.concatenate(outs, axis=-1)


def _midpoint_rows(b, s):
    rows, w = b.shape
    if s >= SUBLANES:
        blocks = b.reshape(rows // s, s, w)
        return jnp.broadcast_to(blocks[:, s // 2 - 1:s // 2, :], blocks.shape).reshape(rows, w)
    groups = b.reshape(rows // SUBLANES, SUBLANES, w)
    sub = lax.broadcasted_iota(jnp.int32, groups.shape, 1)
    out = None
    for first in reversed(range(0, SUBLANES, s)):
        ref_row = jnp.broadcast_to(groups[:, first + s // 2 - 1:first + s // 2, :], groups.shape)
        out = ref_row if out is None else jnp.where(sub < first + s, ref_row, out)
    return out.reshape(rows, w)


def _mixer_ffn2_kernel(x1_ref, proj_ref, la_ref,
                       kmask_ref, dret_ref, gq_ref, gk_ref, aret_ref, ltri_ref, lmask_ref,
                       rn_ref, gn_ref, wout_ref, g2_ref, wg_ref, wu_ref, wd_ref, gf_ref,
                       out_ref, o_ref, act_ref, x2_ref, h2_ref, sr_ref, sg_ref, *, steps_per_seq):
    rows_per_step = x1_ref.shape[0]
    step = pl.program_id(0)

    @pl.when(step == 0)
    def _():
        x2_ref[...] = jnp.zeros_like(x2_ref)
        h2_ref[...] = jnp.zeros_like(h2_ref)

    @pl.when(step % steps_per_seq == 0)
    def _():
        sr_ref[...] = jnp.zeros_like(sr_ref)
        sg_ref[...] = jnp.zeros_like(sg_ref)

    def scores(ql, kl):
        out = []
        for col in range(QK_W // LANE):
            lanes = slice(col * LANE, (col + 1) * LANE)
            w = jnp.concatenate([kl[:, lanes]] * 2, axis=0).T * kmask_ref[...]
            out.append(_dot(ql[:, lanes], w))
        return jnp.concatenate(out, axis=1)

    def attend(s, q_in, v, state):
        s = s.astype(BF16)
        zero = jnp.zeros((CHUNK, HEAD_V), BF16)
        out = []
        for col in range(QK_W // LANE):
            lanes = slice(col * LANE, (col + 1) * LANE)
            rhs = []
            for h in (2 * col, 2 * col + 1):
                vh = v[:, h * HEAD_V:(h + 1) * HEAD_V]
                rhs.append([vh, zero] if h % 2 == 0 else [zero, vh])
            for h in (2 * col, 2 * col + 1):
                sh = state[h * HEAD_QK:(h + 1) * HEAD_QK, :].astype(BF16)
                rhs.append([sh, zero] if h % 2 == 0 else [zero, sh])
            rhs = jnp.concatenate([jnp.concatenate(r, axis=1) for r in rhs], axis=0)
            out.append(_dot(jnp.concatenate([s[:, lanes], q_in[:, lanes]], axis=1), rhs))
        return jnp.concatenate(out, axis=1)

    def state_update(k_out, v):
        k_t = k_out.T
        return jnp.concatenate([_dot(k_t[h * HEAD_QK:(h + 1) * HEAD_QK, :], v[:, h * HEAD_V:(h + 1) * HEAD_V])
                                for h in range(N_HEADS)], axis=0)

    def mixer_stages():
        state = [sr_ref[...], sg_ref[...]]
        for first in range(0, rows_per_step // CHUNK, MIXER_GROUP_CHUNKS):
            yield from chunk_group_stages([pl.ds((first + c) * CHUNK, CHUNK)
                                           for c in range(MIXER_GROUP_CHUNKS)], state)
        sr_ref[...], sg_ref[...] = state

    def chunk_group_stages(chunk_rows, state):
        q_r = [proj_ref[r, RQ:RQ + QK_W] for r in chunk_rows]
        k_r = [proj_ref[r, RK:RK + QK_W] for r in chunk_rows]
        v_r = [proj_ref[r, RV:RV + V_W] for r in chunk_rows]
        s_r = [scores(q, k) * dret_ref[...] for q, k in zip(q_r, k_r)]
        qin_r = [(q.astype(F32) * gq_ref[...]).astype(BF16) for q in q_r]
        yield
        kv_r = [state_update((k.astype(F32) * gk_ref[...]).astype(BF16), v) for k, v in zip(k_r, v_r)]
        yield

        b = []
        for r in chunk_rows:
            la = la_ref[r, :]
            la_hi = la.astype(BF16)
            la_lo = (la - la_hi.astype(F32)).astype(BF16)
            b.append(_dot(ltri_ref[...], jnp.concatenate([la_hi, la_lo], axis=0)) * LOG2_E)
        b_last = [jnp.broadcast_to(bc[CHUNK - 1:CHUNK, :], (2 * CHUNK, QK_W)) for bc in b]
        a_col = [jnp.exp2(bl.T)[:, :LANE] for bl in b_last]
        yield
        q_g = [proj_ref[r, GQ:GQ + QK_W] for r in chunk_rows]
        k_g = [proj_ref[r, GK:GK + QK_W] for r in chunk_rows]
        v_g = [proj_ref[r, GV:GV + V_W] for r in chunk_rows]
        s_g = [scores(q, k) * lmask_ref[0] for q, k in zip(q_g, k_g)]
        for l, s in enumerate(GLA_LEVELS):
            for c in range(len(chunk_rows)):
                if c % 4 == 0:
                    yield
                f = jnp.exp2(-jnp.abs(b[c] - _midpoint_rows(b[c], s))).astype(BF16)
                s_g[c] = s_g[c] + scores(q_g[c] * f, k_g[c] * f) * lmask_ref[1 + l]
        yield
        qin_g = [q * jnp.exp2(bc).astype(BF16) for q, bc in zip(q_g, b)]
        kv_g = [state_update(k * jnp.exp2(bl[:CHUNK, :] - bc).astype(BF16), v)
                for k, bc, bl, v in zip(k_g, b, b_last, v_g)]

        sr, sg = state
        for c, r in enumerate(chunk_rows):
            if c % 2 == 0:
                yield
            o = attend(s_r[c], qin_r[c], v_r[c], sr)
            sr = sr * aret_ref[...] + kv_r[c]
            gate = proj_ref[r, RG:RG + V_W].astype(F32)
            o_ref[r, 0:V_W] = _head_norm_gate(o, rn_ref[...], gate).astype(BF16)
            o = attend(s_g[c], qin_g[c], v_g[c], sg)
            sg = sg * a_col[c] + kv_g[c]
            gate = proj_ref[r, GG:GG + V_W].astype(F32)
            o_ref[r, V_W:2 * V_W] = _head_norm_gate(o, gn_ref[...], gate).astype(BF16)
        state[:] = [sr, sg]

    def finish_previous(rows):
        h = h2_ref[rows, :]
        for c in range(D_FF // MXU_N):
            cols = slice(c * MXU_N, (c + 1) * MXU_N)
            gate = _dot(h, wg_ref[:, cols])
            up = _dot(h, wu_ref[:, cols])
            act_ref[rows, cols] = (_silu(gate) * up).astype(BF16)
            yield
        y = _dot(act_ref[rows, :], wd_ref[...])
        out_ref[rows, :] = _rms(x2_ref[rows, :] + 0.5 * y, gf_ref[...])

    n_half = FFN_STREAM_ROWS
    mixer = mixer_stages()
    halves = [finish_previous(pl.ds(i * n_half, n_half)) for i in range(rows_per_step // n_half)]
    for n_mixer in MIXER_STAGES_PER_ROUND:
        for g in halves:
            next(g, None)
        for _ in range(n_mixer):
            next(mixer, None)
    for g in halves + [mixer]:
        for _ in g:
            pass
    for i in range(rows_per_step // n_half):
        rows = pl.ds(i * n_half, n_half)
        x2 = x1_ref[rows, :] + _dot(o_ref[rows, :], wout_ref[...])
        x2_ref[rows, :] = x2
        h2_ref[rows, :] = _rms(x2, g2_ref[...]).astype(BF16)


def _gla_tables():
    t = np.arange(CHUNK)[:, None]
    u = np.arange(CHUNK)[None, :]
    masks = [np.eye(CHUNK, dtype=bool)]
    for s in GLA_LEVELS:
        same_block = (t // s) == (u // s)
        masks.append(same_block & (((t % s) < s // 2) != ((u % s) < s // 2)))
    ltri = (u <= t).astype(np.float32)
    ltri2 = np.concatenate([ltri, ltri], axis=1)
    lmask = np.stack([np.tile(m, (1, N_HEADS)) for m in masks]).astype(np.float32)
    return ltri2, lmask


def _retention_tables():
    gamma = 1.0 - 2.0 ** (-5.0 - np.arange(N_HEADS, dtype=np.float64))
    head_of_lane = np.arange(QK_W) // HEAD_QK
    i = np.arange(CHUNK)
    dist = np.abs(i[:, None] - i[None, :])
    dret = np.concatenate([gamma[h] ** dist for h in range(N_HEADS)], axis=1)
    gq = gamma[head_of_lane][None, :] ** (i[:, None] + 1.0)
    gk = gamma[head_of_lane][None, :] ** (CHUNK - 1.0 - i[:, None])
    aret = np.broadcast_to((gamma[head_of_lane] ** CHUNK)[:, None], (QK_W, HEAD_V))
    return tuple(np.asarray(a, np.float32) for a in (dret, gq, gk, aret))


def _rotary_tables(seq):
    half = HEAD_QK // 2
    inv = ROPE_BASE ** (-np.arange(half, dtype=np.float64) * 2.0 / HEAD_QK)
    ang = np.arange(seq, dtype=np.float64)[:, None] * inv[None, :]
    sign = np.where((np.arange(LANE) % HEAD_QK) < half, -1.0, 1.0)
    cos = np.tile(np.cos(ang), (1, LANE // half))
    sin = np.tile(np.sin(ang), (1, LANE // half)) * sign[None, :]
    return cos.astype(np.float32), sin.astype(np.float32)


def _key_mask():
    idx = np.arange(LANE) // HEAD_QK
    return idx[:, None] == idx[None, :]


def _const_spec(shape):
    nd = len(shape)
    return pl.BlockSpec(shape, lambda *_: (0,) * nd, pipeline_mode=pl.Buffered(1))


def _slab_spec(shape, n_steps):
    rows, cols = shape
    slab = next(s for s in range(BF16_SUBLANES, rows + 1, BF16_SUBLANES)
                if rows % s == 0 and n_steps % (rows // s) == 0 and rows // s <= n_steps)
    repeat = n_steps // (rows // slab)
    return pl.BlockSpec((slab, cols), lambda i: (i // repeat, 0))


def _tile_rows(total_rows):
    tile = 512
    assert total_rows % tile == 0
    return tile


@jax.jit
def kernel(x, ffn1_norm_g, ffn1_w_gate, ffn1_w_up, ffn1_w_down, mix_norm_g, w_in, ret_norm_g, gla_w_a2,
           gla_b_a, gla_norm_g, w_out, ffn2_norm_g, ffn2_w_gate, ffn2_w_up, ffn2_w_down, final_norm_g):
    bsz, seq, d = x.shape
    assert d == D_MODEL and seq % CHUNK == 0 and ffn1_norm_g.shape[0] == 1
    tile = _tile_rows(seq)
    steps = seq // tile
    n_rows = bsz * seq

    row = lambda g: g.reshape(1, -1).astype(F32)
    w_in_t = jnp.swapaxes(w_in, 1, 2)[0]
    w_a2 = jnp.pad(gla_w_a2[0], ((0, LANE - GATE_RANK), (0, 0))).astype(BF16)

    cos, sin = _rotary_tables(seq)

    x2d = x.reshape(n_rows, d)
    n_tiles = n_rows // tile
    row_spec = lambda w: pl.BlockSpec((tile, w), lambda i: (i, 0))
    pos_spec = pl.BlockSpec((tile, LANE), lambda i: (i % steps, 0))
    hbm_spec = pl.BlockSpec(memory_space=pl.ANY)
    later = [w_out[0], ffn2_w_gate[0], ffn2_w_up[0], ffn2_w_down[0]]
    later_specs = [_slab_spec(w.shape, n_tiles) for w in later]
    x1, proj, la, w_out_b, wg2_b, wu2_b, wd2_b = pl.pallas_call(
        _ffn1_proj_kernel,
        grid=(n_tiles,),
        in_specs=[row_spec(d), pos_spec, pos_spec, _const_spec((1, d)),
                  hbm_spec, hbm_spec, hbm_spec, _const_spec((1, d)), hbm_spec,
                  _const_spec((LANE, QK_W)), _const_spec((1, QK_W))] + later_specs,
        out_specs=[row_spec(d), row_spec(PROJ_W), row_spec(QK_W)] + later_specs,
        out_shape=[jax.ShapeDtypeStruct((n_rows, d), F32),
                   jax.ShapeDtypeStruct((n_rows, PROJ_W), BF16),
                   jax.ShapeDtypeStruct((n_rows, QK_W), F32)]
                  + [jax.ShapeDtypeStruct(w.shape, BF16) for w in later],
        scratch_shapes=[pltpu.VMEM((tile, D_FF), BF16),
                        pltpu.VMEM((d, D_FF), BF16), pltpu.VMEM((d, D_FF), BF16), pltpu.VMEM((D_FF, d), BF16),
                        pltpu.VMEM((PROJ_W, d), BF16), pltpu.VMEM((LANE, d), BF16),
                        pltpu.VMEM((WEIGHT_STAGE_SLOTS, WEIGHT_CHUNK_ROWS, D_FF), F32),
                        pltpu.SemaphoreType.DMA((WEIGHT_STAGE_SLOTS,))],
        compiler_params=pltpu.CompilerParams(dimension_semantics=("arbitrary",),
                                             vmem_limit_bytes=VMEM_LIMIT),
        name="ffn1_proj",
    )(x2d, cos, sin, row(ffn1_norm_g), ffn1_w_gate[0], ffn1_w_up[0], ffn1_w_down[0], row(mix_norm_g),
      w_in_t, w_a2, row(gla_b_a), *later)

    ltri2, lmask = _gla_tables()
    dret, gq, gk, aret = _retention_tables()
    consts = [jnp.asarray(_key_mask(), BF16), jnp.asarray(dret), jnp.asarray(gq),
              jnp.asarray(gk), jnp.asarray(aret), jnp.asarray(ltri2, BF16), jnp.asarray(lmask)]

    tile = MIXER_TILE_ROWS
    assert seq % tile == 0
    n_tiles = n_rows // tile
    tok_spec = lambda w: pl.BlockSpec((tile, w), lambda i: (jnp.minimum(i, n_tiles - 1), 0))
    out = pl.pallas_call(
        functools.partial(_mixer_ffn2_kernel, steps_per_seq=seq // tile),
        grid=(n_tiles + 1,),
        in_specs=[tok_spec(d), tok_spec(PROJ_W), tok_spec(QK_W)]
                 + [_const_spec(c.shape) for c in consts]
                 + [_const_spec((1, V_W)), _const_spec((1, V_W)), _const_spec((d, d)), _const_spec((1, d)),
                    _const_spec((d, D_FF)), _const_spec((d, D_FF)), _const_spec((D_FF, d)),
                    _const_spec((1, d))],
        out_specs=pl.BlockSpec((tile, d), lambda i: (jnp.maximum(i - 1, 0), 0)),
        out_shape=jax.ShapeDtypeStruct((n_rows, d), F32),
        scratch_shapes=[pltpu.VMEM((tile, 2 * V_W), BF16), pltpu.VMEM((tile, D_FF), BF16),
                        pltpu.VMEM((tile, d), F32), pltpu.VMEM((tile, d), BF16),
                        pltpu.VMEM((QK_W, HEAD_V), F32), pltpu.VMEM((QK_W, HEAD_V), F32)],
        compiler_params=pltpu.CompilerParams(dimension_semantics=("arbitrary",),
                                             vmem_limit_bytes=VMEM_LIMIT),
        name="mixer_ffn2",
    )(x1, proj, la, *consts, row(ret_norm_g), row(gla_norm_g), w_out_b, row(ffn2_norm_g),
      wg2_b, wu2_b, wd2_b, row(final_norm_g))
    return out.reshape(bsz, seq, d)
```

```python
import functools

import numpy as np
import jax
import jax.numpy as jnp
from jax import lax
from jax.experimental import pallas as pl
from jax.experimental.pallas import tpu as pltpu

D_MODEL = 1024
CHUNK = 64
RMS_EPS = 1e-6
ROPE_BASE = 10000.0
N_HEADS = 4
QK_W = 256
V_W = 512
HEAD_V = V_W // N_HEADS
HEAD_QK = QK_W // N_HEADS
GATE_RANK = 16
GATE_NORM = 16.0
D_FF = 2816
PROJ_W = 2 * (2 * QK_W + 2 * V_W)
RQ, RK, RV, RG = 0, 256, 512, 1024
GQ, GK, GV, GG = 1536, 1792, 2048, 2560
GLOW = 3072

LANE = 128
SUBLANES = 8
BF16_SUBLANES = 16
MXU_N = 256
FFN_STREAM_ROWS = 256
MIXER_TILE_ROWS = 512
WEIGHT_CHUNK_ROWS = 128
WEIGHT_STAGE_SLOTS = 3
VMEM_LIMIT = 56 * 1024 * 1024

GLA_LEVELS = (64, 32, 16, 8, 4, 2)
MIXER_GROUP_CHUNKS = 4
MIXER_STAGES_PER_ROUND = (3, 2) * 5 + (2,)
LOG2_E = 1.4426950408889634

BF16 = jnp.bfloat16
F32 = jnp.float32


def _dot(a, b):
    return jnp.dot(a, b, preferred_element_type=F32)


def _dot_nt(a, b):
    return lax.dot_general(a, b, (((1,), (1,)), ((), ())), preferred_element_type=F32)


def _rms(x, g):
    ms = jnp.mean(x * x, axis=-1, keepdims=True)
    return x * lax.rsqrt(ms + RMS_EPS) * g


def _silu(x):
    h = 0.5 * x
    return h * jnp.tanh(h) + h


def _load_weight_as_bf16(src_hbm, dst_ref, stage_ref, sem):
    n_rows, n_cols = dst_ref.shape
    assert n_rows % WEIGHT_CHUNK_ROWS == 0 and n_cols <= stage_ref.shape[2]
    n_chunks = n_rows // WEIGHT_CHUNK_ROWS
    n_slots = stage_ref.shape[0]

    def chunk_copy(k):
        slot = k % n_slots
        return pltpu.make_async_copy(
            src_hbm.at[pl.ds(k * WEIGHT_CHUNK_ROWS, WEIGHT_CHUNK_ROWS), :],
            stage_ref.at[slot, :, pl.ds(0, n_cols)], sem.at[slot])

    for k in range(min(n_slots - 1, n_chunks)):
        chunk_copy(k).start()

    def body(k, carry):
        @pl.when(k + n_slots - 1 < n_chunks)
        def _():
            chunk_copy(k + n_slots - 1).start()

        chunk_copy(k).wait()
        rows = pl.ds(pl.multiple_of(k * WEIGHT_CHUNK_ROWS, WEIGHT_CHUNK_ROWS), WEIGHT_CHUNK_ROWS)
        dst_ref[rows, :] = stage_ref[k % n_slots, :, pl.ds(0, n_cols)].astype(BF16)
        return carry

    lax.fori_loop(0, n_chunks, body, 0)


def _ffn1_proj_kernel(x_ref, xnext_ref, cos_ref, sin_ref, g1_ref, wg_hbm, wu_hbm, wd_hbm, gm_ref, win_hbm,
                      wa2_ref, ba_ref, *rest):
    later_f32, (x1_ref, proj_ref, la_ref), later_bf16 = rest[:4], rest[4:7], rest[7:11]
    act_ref, hfirst_ref, wg_ref, wu_ref, wd_ref, win_ref, wlow_ref, stage_ref, sem = rest[11:]
    n_half = x_ref.shape[0] // 2

    @pl.when(pl.program_id(0) == 0)
    def _():
        hfirst_ref[...] = _rms(x_ref[0:n_half, :], g1_ref[...]).astype(BF16)

    @pl.when(pl.program_id(0) == 0)
    def _():
        for src, dst in ((wg_hbm, wg_ref), (wu_hbm, wu_ref), (wd_hbm, wd_ref), (win_hbm, win_ref)):
            _load_weight_as_bf16(src, dst, stage_ref, sem)
        n_cols = wlow_ref.shape[1]
        tail = pltpu.make_async_copy(win_hbm.at[pl.ds(GLOW, GATE_RANK), :],
                                     stage_ref.at[0, pl.ds(0, GATE_RANK), pl.ds(0, n_cols)], sem.at[0])
        tail.start()
        wlow_ref[...] = jnp.zeros_like(wlow_ref)
        tail.wait()
        wlow_ref[0:GATE_RANK, :] = stage_ref[0, 0:GATE_RANK, 0:n_cols].astype(BF16)

    def cast_later_weights():
        yield
        for src, dst in zip(later_f32, later_bf16):
            dst[...] = src[...].astype(BF16)

    def prepare_next_first_half():
        yield
        yield
        yield
        hfirst_ref[...] = _rms(xnext_ref[...], g1_ref[...]).astype(BF16)

    def half_tile(rows, first):
        x = x_ref[rows, :]
        h = hfirst_ref[...] if first else _rms(x, g1_ref[...]).astype(BF16)
        yield
        for c in range(D_FF // MXU_N):
            cols = slice(c * MXU_N, (c + 1) * MXU_N)
            gate = _dot(h, wg_ref[:, cols])
            up = _dot(h, wu_ref[:, cols])
            act_ref[rows, cols] = (_silu(gate) * up).astype(BF16)
        yield
        x1 = x + 0.5 * _dot(act_ref[rows, :], wd_ref[...])
        x1_ref[rows, :] = x1
        h = _rms(x1, gm_ref[...]).astype(BF16)
        yield
        cos = cos_ref[rows, :]
        sin = sin_ref[rows, :]
        first_half = (lax.broadcasted_iota(jnp.int32, cos.shape, 1) % HEAD_QK) < HEAD_QK // 2
        for off, scale in ((RQ, 1.0), (RK, HEAD_QK ** -0.5)):
            t = _dot_nt(h, win_ref[off:off + QK_W, :])
            for col in range(QK_W // LANE):
                tc = t[:, col * LANE:(col + 1) * LANE]
                partner = jnp.where(first_half, pltpu.roll(tc, LANE - HEAD_QK // 2, 1),
                                    pltpu.roll(tc, HEAD_QK // 2, 1))
                proj_ref[rows, off + col * LANE:off + (col + 1) * LANE] = (
                    (tc * cos + partner * sin) * scale).astype(BF16)
        proj_ref[rows, RV:GQ] = _dot_nt(h, win_ref[RV:GQ, :]).astype(BF16)
        proj_ref[rows, GQ:GK] = (_dot_nt(h, win_ref[GQ:GK, :]) * (HEAD_QK ** -0.5)).astype(BF16)
        proj_ref[rows, GK:PROJ_W] = _dot_nt(h, win_ref[GK:PROJ_W, :]).astype(BF16)
        low = _dot_nt(h, wlow_ref[...]).astype(BF16)
        yield
        z = _dot(low, wa2_ref[...]) + ba_ref[...]
        la_ref[rows, :] = (jnp.minimum(z, 0.0) - jnp.log(1.0 + jnp.exp(-jnp.abs(z)))) * (1.0 / GATE_NORM)

    streams = ([half_tile(pl.ds(i * n_half, n_half), i == 0) for i in range(2)]
               + [cast_later_weights(), prepare_next_first_half()])
    while streams:
        streams = [g for g in streams if next(g, StopIteration) is not StopIteration]


def _head_norm_gate(o, norm_g, gate):
    outs = []
    for h in range(N_HEADS):
        cols = slice(h * HEAD_V, (h + 1) * HEAD_V)
        oh = o[:, cols]
        ms = jnp.mean(oh * oh, axis=-1, keepdims=True)
        outs.append(oh * lax.rsqrt(ms + RMS_EPS) * norm_g[:, cols] * _silu(gate[:, cols]))
    return jnp.concatenate(outs, axis=-1)


def _midpoint_rows(b, s):
    rows, w = b.shape
    if s >= SUBLANES:
        blocks = b.reshape(rows // s, s, w)
        return jnp.broadcast_to(blocks[:, s // 2 - 1:s // 2, :], blocks.shape).reshape(rows, w)
    groups = b.reshape(rows // SUBLANES, SUBLANES, w)
    sub = lax.broadcasted_iota(jnp.int32, groups.shape, 1)
    out = None
    for first in reversed(range(0, SUBLANES, s)):
        ref_row = jnp.broadcast_to(groups[:, first + s // 2 - 1:first + s // 2, :], groups.shape)
        out = ref_row if out is None else jnp.where(sub < first + s, ref_row, out)
    return out.reshape(rows, w)


def _mixer_ffn2_kernel(x1_ref, proj_ref, la_ref,
                       kmask_ref, dret_ref, gq_ref, gk_ref, aret_ref, ltri_ref, lmask_ref,
                       rn_ref, gn_ref, wout_ref, g2_ref, wg_ref, wu_ref, wd_ref, gf_ref,
                       out_ref, o_ref, act_ref, x2_ref, h2_ref, sr_ref, sg_ref, *, steps_per_seq):
    rows_per_step = x1_ref.shape[0]
    step = pl.program_id(0)

    @pl.when(step == 0)
    def _():
        x2_ref[...] = jnp.zeros_like(x2_ref)
        h2_ref[...] = jnp.zeros_like(h2_ref)

    @pl.when(step % steps_per_seq == 0)
    def _():
        sr_ref[...] = jnp.zeros_like(sr_ref)
        sg_ref[...] = jnp.zeros_like(sg_ref)

    def scores(ql, kl):
        out = []
        for col in range(QK_W // LANE):
            lanes = slice(col * LANE, (col + 1) * LANE)
            w = jnp.concatenate([kl[:, lanes]] * 2, axis=0).T * kmask_ref[...]
            out.append(_dot(ql[:, lanes], w))
        return jnp.concatenate(out, axis=1)

    def attend(s, q_in, v, state):
        s = s.astype(BF16)
        zero = jnp.zeros((CHUNK, HEAD_V), BF16)
        out = []
        for col in range(QK_W // LANE):
            lanes = slice(col * LANE, (col + 1) * LANE)
            rhs = []
            for h in (2 * col, 2 * col + 1):
                vh = v[:, h * HEAD_V:(h + 1) * HEAD_V]
                rhs.append([vh, zero] if h % 2 == 0 else [zero, vh])
            for h in (2 * col, 2 * col + 1):
                sh = state[h * HEAD_QK:(h + 1) * HEAD_QK, :].astype(BF16)
                rhs.append([sh, zero] if h % 2 == 0 else [zero, sh])
            rhs = jnp.concatenate([jnp.concatenate(r, axis=1) for r in rhs], axis=0)
            out.append(_dot(jnp.concatenate([s[:, lanes], q_in[:, lanes]], axis=1), rhs))
        return jnp.concatenate(out, axis=1)

    def state_update(k_out, v):
        k_t = k_out.T
        return jnp.concatenate([_dot(k_t[h * HEAD_QK:(h + 1) * HEAD_QK, :], v[:, h * HEAD_V:(h + 1) * HEAD_V])
                                for h in range(N_HEADS)], axis=0)

    def mixer_stages():
        state = [sr_ref[...], sg_ref[...]]
        for first in range(0, rows_per_step // CHUNK, MIXER_GROUP_CHUNKS):
            yield from chunk_group_stages([pl.ds((first + c) * CHUNK, CHUNK)
                                           for c in range(MIXER_GROUP_CHUNKS)], state)
        sr_ref[...], sg_ref[...] = state

    def chunk_group_stages(chunk_rows, state):
        q_r = [proj_ref[r, RQ:RQ + QK_W] for r in chunk_rows]
        k_r = [proj_ref[r, RK:RK + QK_W] for r in chunk_rows]
        v_r = [proj_ref[r, RV:RV + V_W] for r in chunk_rows]
        s_r = [scores(q, k) * dret_ref[...] for q, k in zip(q_r, k_r)]
        qin_r = [(q.astype(F32) * gq_ref[...]).astype(BF16) for q in q_r]
        yield
        kv_r = [state_update((k.astype(F32) * gk_ref[...]).astype(BF16), v) for k, v in zip(k_r, v_r)]
        yield

        b = []
        for r in chunk_rows:
            la = la_ref[r, :]
            la_hi = la.astype(BF16)
            la_lo = (la - la_hi.astype(F32)).astype(BF16)
            b.append(_dot(ltri_ref[...], jnp.concatenate([la_hi, la_lo], axis=0)) * LOG2_E)
        b_last = [jnp.broadcast_to(bc[CHUNK - 1:CHUNK, :], (2 * CHUNK, QK_W)) for bc in b]
        a_col = [jnp.exp2(bl.T)[:, :LANE] for bl in b_last]
        yield
        q_g = [proj_ref[r, GQ:GQ + QK_W] for r in chunk_rows]
        k_g = [proj_ref[r, GK:GK + QK_W] for r in chunk_rows]
        v_g = [proj_ref[r, GV:GV + V_W] for r in chunk_rows]
        s_g = [scores(q, k) * lmask_ref[0] for q, k in zip(q_g, k_g)]
        for l, s in enumerate(GLA_LEVELS):
            for c in range(len(chunk_rows)):
                if c % 4 == 0:
                    yield
                f = jnp.exp2(-jnp.abs(b[c] - _midpoint_rows(b[c], s))).astype(BF16)
                s_g[c] = s_g[c] + scores(q_g[c] * f, k_g[c] * f) * lmask_ref[1 + l]
        yield
        qin_g = [q * jnp.exp2(bc).astype(BF16) for q, bc in zip(q_g, b)]
        kv_g = [state_update(k * jnp.exp2(bl[:CHUNK, :] - bc).astype(BF16), v)
                for k, bc, bl, v in zip(k_g, b, b_last, v_g)]

        sr, sg = state
        for c, r in enumerate(chunk_rows):
            if c % 2 == 0:
                yield
            o = attend(s_r[c], qin_r[c], v_r[c], sr)
            sr = sr * aret_ref[...] + kv_r[c]
            gate = proj_ref[r, RG:RG + V_W].astype(F32)
            o_ref[r, 0:V_W] = _head_norm_gate(o, rn_ref[...], gate).astype(BF16)
            o = attend(s_g[c], qin_g[c], v_g[c], sg)
            sg = sg * a_col[c] + kv_g[c]
            gate = proj_ref[r, GG:GG + V_W].astype(F32)
            o_ref[r, V_W:2 * V_W] = _head_norm_gate(o, gn_ref[...], gate).astype(BF16)
        state[:] = [sr, sg]

    def finish_previous(rows):
        h = h2_ref[rows, :]
        for c in range(D_FF // MXU_N):
            cols = slice(c * MXU_N, (c + 1) * MXU_N)
            gate = _dot(h, wg_ref[:, cols])
            up = _dot(h, wu_ref[:, cols])
            act_ref[rows, cols] = (_silu(gate) * up).astype(BF16)
            yield
        y = _dot(act_ref[rows, :], wd_ref[...])
        out_ref[rows, :] = _rms(x2_ref[rows, :] + 0.5 * y, gf_ref[...])

    n_half = FFN_STREAM_ROWS
    mixer = mixer_stages()
    halves = [finish_previous(pl.ds(i * n_half, n_half)) for i in range(rows_per_step // n_half)]
    for n_mixer in MIXER_STAGES_PER_ROUND:
        for g in halves:
            next(g, None)
        for _ in range(n_mixer):
            next(mixer, None)
    for g in halves + [mixer]:
        for _ in g:
            pass
    for i in range(rows_per_step // n_half):
        rows = pl.ds(i * n_half, n_half)
        x2 = x1_ref[rows, :] + _dot(o_ref[rows, :], wout_ref[...])
        x2_ref[rows, :] = x2
        h2_ref[rows, :] = _rms(x2, g2_ref[...]).astype(BF16)


def _gla_tables():
    t = np.arange(CHUNK)[:, None]
    u = np.arange(CHUNK)[None, :]
    masks = [np.eye(CHUNK, dtype=bool)]
    for s in GLA_LEVELS:
        same_block = (t // s) == (u // s)
        masks.append(same_block & (((t % s) < s // 2) != ((u % s) < s // 2)))
    ltri = (u <= t).astype(np.float32)
    ltri2 = np.concatenate([ltri, ltri], axis=1)
    lmask = np.stack([np.tile(m, (1, N_HEADS)) for m in masks]).astype(np.float32)
    return ltri2, lmask


def _retention_tables():
    gamma = 1.0 - 2.0 ** (-5.0 - np.arange(N_HEADS, dtype=np.float64))
    head_of_lane = np.arange(QK_W) // HEAD_QK
    i = np.arange(CHUNK)
    dist = np.abs(i[:, None] - i[None, :])
    dret = np.concatenate([gamma[h] ** dist for h in range(N_HEADS)], axis=1)
    gq = gamma[head_of_lane][None, :] ** (i[:, None] + 1.0)
    gk = gamma[head_of_lane][None, :] ** (CHUNK - 1.0 - i[:, None])
    aret = np.broadcast_to((gamma[head_of_lane] ** CHUNK)[:, None], (QK_W, HEAD_V))
    return tuple(np.asarray(a, np.float32) for a in (dret, gq, gk, aret))


def _rotary_tables(seq):
    half = HEAD_QK // 2
    inv = ROPE_BASE ** (-np.arange(half, dtype=np.float64) * 2.0 / HEAD_QK)
    ang = np.arange(seq, dtype=np.float64)[:, None] * inv[None, :]
    sign = np.where((np.arange(LANE) % HEAD_QK) < half, -1.0, 1.0)
    cos = np.tile(np.cos(ang), (1, LANE // half))
    sin = np.tile(np.sin(ang), (1, LANE // half)) * sign[None, :]
    return cos.astype(np.float32), sin.astype(np.float32)


def _key_mask():
    idx = np.arange(LANE) // HEAD_QK
    return idx[:, None] == idx[None, :]


def _const_spec(shape):
    nd = len(shape)
    return pl.BlockSpec(shape, lambda *_: (0,) * nd, pipeline_mode=pl.Buffered(1))


def _slab_spec(shape, n_steps):
    rows, cols = shape
    slab = next(s for s in range(BF16_SUBLANES, rows + 1, BF16_SUBLANES)
                if rows % s == 0 and n_steps % (rows // s) == 0 and rows // s <= n_steps)
    repeat = n_steps // (rows // slab)
    return pl.BlockSpec((slab, cols), lambda i: (i // repeat, 0))


def _tile_rows(total_rows):
    tile = 512
    assert total_rows % tile == 0
    return tile


@jax.jit
def kernel(x, ffn1_norm_g, ffn1_w_gate, ffn1_w_up, ffn1_w_down, mix_norm_g, w_in, ret_norm_g, gla_w_a2,
           gla_b_a, gla_norm_g, w_out, ffn2_norm_g, ffn2_w_gate, ffn2_w_up, ffn2_w_down, final_norm_g):
    bsz, seq, d = x.shape
    assert d == D_MODEL and seq % CHUNK == 0 and ffn1_norm_g.shape[0] == 1
    tile = _tile_rows(seq)
    steps = seq // tile
    n_rows = bsz * seq

    row = lambda g: g.reshape(1, -1).astype(F32)
    w_in_t = jnp.swapaxes(w_in, 1, 2)[0]
    w_a2 = jnp.pad(gla_w_a2[0], ((0, LANE - GATE_RANK), (0, 0))).astype(BF16)

    cos, sin = _rotary_tables(seq)

    x2d = x.reshape(n_rows, d)
    n_tiles = n_rows // tile
    row_spec = lambda w: pl.BlockSpec((tile, w), lambda i: (i, 0))
    pos_spec = pl.BlockSpec((tile, LANE), lambda i: (i % steps, 0))
    hbm_spec = pl.BlockSpec(memory_space=pl.ANY)
    next_first_half_spec = pl.BlockSpec((tile // 2, d), lambda i: (2 * jnp.minimum(i + 1, n_tiles - 1), 0))
    later = [w_out[0], ffn2_w_gate[0], ffn2_w_up[0], ffn2_w_down[0]]
    later_specs = [_slab_spec(w.shape, n_tiles) for w in later]
    x1, proj, la, w_out_b, wg2_b, wu2_b, wd2_b = pl.pallas_call(
        _ffn1_proj_kernel,
        grid=(n_tiles,),
        in_specs=[row_spec(d), next_first_half_spec, pos_spec, pos_spec, _const_spec((1, d)),
                  hbm_spec, hbm_spec, hbm_spec, _const_spec((1, d)), hbm_spec,
                  _const_spec((LANE, QK_W)), _const_spec((1, QK_W))] + later_specs,
        out_specs=[row_spec(d), row_spec(PROJ_W), row_spec(QK_W)] + later_specs,
        out_shape=[jax.ShapeDtypeStruct((n_rows, d), F32),
                   jax.ShapeDtypeStruct((n_rows, PROJ_W), BF16),
                   jax.ShapeDtypeStruct((n_rows, QK_W), F32)]
                  + [jax.ShapeDtypeStruct(w.shape, BF16) for w in later],
        scratch_shapes=[pltpu.VMEM((tile, D_FF), BF16), pltpu.VMEM((tile // 2, d), BF16),
                        pltpu.VMEM((d, D_FF), BF16), pltpu.VMEM((d, D_FF), BF16), pltpu.VMEM((D_FF, d), BF16),
                        pltpu.VMEM((PROJ_W, d), BF16), pltpu.VMEM((LANE, d), BF16),
                        pltpu.VMEM((WEIGHT_STAGE_SLOTS, WEIGHT_CHUNK_ROWS, D_FF), F32),
                        pltpu.SemaphoreType.DMA((WEIGHT_STAGE_SLOTS,))],
        compiler_params=pltpu.CompilerParams(dimension_semantics=("arbitrary",),
                                             vmem_limit_bytes=VMEM_LIMIT),
        name="ffn1_proj",
    )(x2d, x2d, cos, sin, row(ffn1_norm_g), ffn1_w_gate[0], ffn1_w_up[0], ffn1_w_down[0], row(mix_norm_g),
      w_in_t, w_a2, row(gla_b_a), *later)

    ltri2, lmask = _gla_tables()
    dret, gq, gk, aret = _retention_tables()
    consts = [jnp.asarray(_key_mask(), BF16), jnp.asarray(dret), jnp.asarray(gq),
              jnp.asarray(gk), jnp.asarray(aret), jnp.asarray(ltri2, BF16), jnp.asarray(lmask)]

    tile = MIXER_TILE_ROWS
    assert seq % tile == 0
    n_tiles = n_rows // tile
    tok_spec = lambda w: pl.BlockSpec((tile, w), lambda i: (jnp.minimum(i, n_tiles - 1), 0))
    out = pl.pallas_call(
        functools.partial(_mixer_ffn2_kernel, steps_per_seq=seq // tile),
        grid=(n_tiles + 1,),
        in_specs=[tok_spec(d), tok_spec(PROJ_W), tok_spec(QK_W)]
                 + [_const_spec(c.shape) for c in consts]
                 + [_const_spec((1, V_W)), _const_spec((1, V_W)), _const_spec((d, d)), _const_spec((1, d)),
                    _const_spec((d, D_FF)), _const_spec((d, D_FF)), _const_spec((D_FF, d)),
                    _const_spec((1, d))],
        out_specs=pl.BlockSpec((tile, d), lambda i: (jnp.maximum(i - 1, 0), 0)),
        out_shape=jax.ShapeDtypeStruct((n_rows, d), F32),
        scratch_shapes=[pltpu.VMEM((tile, 2 * V_W), BF16), pltpu.VMEM((tile, D_FF), BF16),
                        pltpu.VMEM((tile, d), F32), pltpu.VMEM((tile, d), BF16),
                        pltpu.VMEM((QK_W, HEAD_V), F32), pltpu.VMEM((QK_W, HEAD_V), F32)],
        compiler_params=pltpu.CompilerParams(dimension_semantics=("arbitrary",),
                                             vmem_limit_bytes=VMEM_LIMIT),
        name="mixer_ffn2",
    )(x1, proj, la, *consts, row(ret_norm_g), row(gla_norm_g), w_out_b, row(ffn2_norm_g),
      wg2_b, wu2_b, wd2_b, row(final_norm_g))
    return out.reshape(bsz, seq, d)
```

```python
import functools

import numpy as np
import jax
import jax.numpy as jnp
from jax import lax
from jax.experimental import pallas as pl
from jax.experimental.pallas import tpu as pltpu

D_MODEL = 1024
CHUNK = 64
RMS_EPS = 1e-6
ROPE_BASE = 10000.0
N_HEADS = 4
QK_W = 256
V_W = 512
HEAD_V = V_W // N_HEADS
HEAD_QK = QK_W // N_HEADS
GATE_RANK = 16
GATE_NORM = 16.0
D_FF = 2816
PROJ_W = 2 * (2 * QK_W + 2 * V_W)
RQ, RK, RV, RG = 0, 256, 512, 1024
GQ, GK, GV, GG = 1536, 1792, 2048, 2560
GLOW = 3072

LANE = 128
SUBLANES = 8
BF16_SUBLANES = 16
MXU_N = 256
FFN_STREAM_ROWS = 512
MIXER_TILE_ROWS = 512
WEIGHT_CHUNK_ROWS = 128
WEIGHT_STAGE_SLOTS = 4
VMEM_LIMIT = 56 * 1024 * 1024

GLA_LEVELS = (64, 32, 16, 8, 4, 2)
MIXER_GROUP_CHUNKS = 4
MIXER_STAGES_PER_ROUND = (3, 2) * 5 + (2,)
LOG2_E = 1.4426950408889634

BF16 = jnp.bfloat16
F32 = jnp.float32


def _dot(a, b):
    return jnp.dot(a, b, preferred_element_type=F32)


def _dot_nt(a, b):
    return lax.dot_general(a, b, (((1,), (1,)), ((), ())), preferred_element_type=F32)


def _rms(x, g):
    ms = jnp.mean(x * x, axis=-1, keepdims=True)
    return x * lax.rsqrt(ms + RMS_EPS) * g


def _silu(x):
    h = 0.5 * x
    return h * jnp.tanh(h) + h


def _load_weight_as_bf16(src_hbm, dst_ref, stage_ref, sem):
    n_rows, n_cols = dst_ref.shape
    assert n_rows % WEIGHT_CHUNK_ROWS == 0 and n_cols <= stage_ref.shape[2]
    n_chunks = n_rows // WEIGHT_CHUNK_ROWS
    n_slots = stage_ref.shape[0]

    def chunk_copy(k):
        slot = k % n_slots
        return pltpu.make_async_copy(
            src_hbm.at[pl.ds(k * WEIGHT_CHUNK_ROWS, WEIGHT_CHUNK_ROWS), :],
            stage_ref.at[slot, :, pl.ds(0, n_cols)], sem.at[slot])

    for k in range(min(n_slots - 1, n_chunks)):
        chunk_copy(k).start()

    def body(k, carry):
        @pl.when(k + n_slots - 1 < n_chunks)
        def _():
            chunk_copy(k + n_slots - 1).start()

        chunk_copy(k).wait()
        rows = pl.ds(pl.multiple_of(k * WEIGHT_CHUNK_ROWS, WEIGHT_CHUNK_ROWS), WEIGHT_CHUNK_ROWS)
        dst_ref[rows, :] = stage_ref[k % n_slots, :, pl.ds(0, n_cols)].astype(BF16)
        return carry

    lax.fori_loop(0, n_chunks, body, 0)


def _ffn1_proj_kernel(x_ref, cos_ref, sin_ref, g1_ref, wg_hbm, wu_hbm, wd_hbm, gm_ref, win_hbm,
                      wa2_ref, ba_ref, *rest):
    later_f32, (x1_ref, proj_ref, la_ref), later_bf16 = rest[:4], rest[4:7], rest[7:11]
    act_ref, wg_ref, wu_ref, wd_ref, win_ref, wlow_ref, stage_ref, sem = rest[11:]

    @pl.when(pl.program_id(0) == 0)
    def _():
        for src, dst in ((wg_hbm, wg_ref), (wu_hbm, wu_ref), (wd_hbm, wd_ref), (win_hbm, win_ref)):
            _load_weight_as_bf16(src, dst, stage_ref, sem)
        n_cols = wlow_ref.shape[1]
        tail = pltpu.make_async_copy(win_hbm.at[pl.ds(GLOW, GATE_RANK), :],
                                     stage_ref.at[0, pl.ds(0, GATE_RANK), pl.ds(0, n_cols)], sem.at[0])
        tail.start()
        wlow_ref[...] = jnp.zeros_like(wlow_ref)
        tail.wait()
        wlow_ref[0:GATE_RANK, :] = stage_ref[0, 0:GATE_RANK, 0:n_cols].astype(BF16)

    def cast_later_weights():
        yield
        for src, dst in zip(later_f32, later_bf16):
            dst[...] = src[...].astype(BF16)

    def half_tile(rows):
        x = x_ref[rows, :]
        h = _rms(x, g1_ref[...]).astype(BF16)
        yield
        for c in range(D_FF // MXU_N):
            cols = slice(c * MXU_N, (c + 1) * MXU_N)
            gate = _dot(h, wg_ref[:, cols])
            up = _dot(h, wu_ref[:, cols])
            act_ref[rows, cols] = (_silu(gate) * up).astype(BF16)
        yield
        x1 = x + 0.5 * _dot(act_ref[rows, :], wd_ref[...])
        x1_ref[rows, :] = x1
        h = _rms(x1, gm_ref[...]).astype(BF16)
        yield
        cos = cos_ref[rows, :]
        sin = sin_ref[rows, :]
        first_half = (lax.broadcasted_iota(jnp.int32, cos.shape, 1) % HEAD_QK) < HEAD_QK // 2
        for off, scale in ((RQ, 1.0), (RK, HEAD_QK ** -0.5)):
            t = _dot_nt(h, win_ref[off:off + QK_W, :])
            for col in range(QK_W // LANE):
                tc = t[:, col * LANE:(col + 1) * LANE]
                partner = jnp.where(first_half, pltpu.roll(tc, LANE - HEAD_QK // 2, 1),
                                    pltpu.roll(tc, HEAD_QK // 2, 1))
                proj_ref[rows, off + col * LANE:off + (col + 1) * LANE] = (
                    (tc * cos + partner * sin) * scale).astype(BF16)
        t = _dot_nt(h, win_ref[RV:GQ, :])
        proj_ref[rows, RV:RG] = t[:, :RG - RV].astype(BF16)
        proj_ref[rows, RG:GQ] = _silu(t[:, RG - RV:]).astype(BF16)
        proj_ref[rows, GQ:GK] = (_dot_nt(h, win_ref[GQ:GK, :]) * (HEAD_QK ** -0.5)).astype(BF16)
        t = _dot_nt(h, win_ref[GK:PROJ_W, :])
        proj_ref[rows, GK:GG] = t[:, :GG - GK].astype(BF16)
        proj_ref[rows, GG:PROJ_W] = _silu(t[:, GG - GK:]).astype(BF16)
        low = _dot_nt(h, wlow_ref[...]).astype(BF16)
        yield
        z = _dot(low, wa2_ref[...]) + ba_ref[...]
        la_ref[rows, :] = (jnp.minimum(z, 0.0) - jnp.log(1.0 + jnp.exp(-jnp.abs(z)))) * (1.0 / GATE_NORM)

    n_half = x_ref.shape[0] // 2
    streams = [half_tile(pl.ds(i * n_half, n_half)) for i in range(2)] + [cast_later_weights()]
    while streams:
        streams = [g for g in streams if next(g, StopIteration) is not StopIteration]


def _head_norm_gate(o, norm_g, gate):
    outs = []
    for h in range(N_HEADS):
        cols = slice(h * HEAD_V, (h + 1) * HEAD_V)
        oh = o[:, cols]
        ms = jnp.mean(oh * oh, axis=-1, keepdims=True)
        outs.append(oh * lax.rsqrt(ms + RMS_EPS) * norm_g[:, cols] * gate[:, cols])
    return jnp.concatenate(outs, axis=-1)


def _midpoint_rows(b, s):
    rows, w = b.shape
    if s >= SUBLANES:
        blocks = b.reshape(rows // s, s, w)
        return jnp.broadcast_to(blocks[:, s // 2 - 1:s // 2, :], blocks.shape).reshape(rows, w)
    groups = b.reshape(rows // SUBLANES, SUBLANES, w)
    sub = lax.broadcasted_iota(jnp.int32, groups.shape, 1)
    out = None
    for first in reversed(range(0, SUBLANES, s)):
        ref_row = jnp.broadcast_to(groups[:, first + s // 2 - 1:first + s // 2, :], groups.shape)
        out = ref_row if out is None else jnp.where(sub < first + s, ref_row, out)
    return out.reshape(rows, w)


def _mixer_ffn2_kernel(x1_ref, proj_ref, la_ref,
                       kmask_ref, dret_ref, gq_ref, gk_ref, aret_ref, ltri_ref, lmask_ref,
                       rn_ref, gn_ref, wout_ref, g2_ref, wg_ref, wu_ref, wd_ref, gf_ref,
                       out_ref, o_ref, act_ref, x2_ref, h2_ref, sr_ref, sg_ref, *, steps_per_seq):
    rows_per_step = x1_ref.shape[0]
    step = pl.program_id(0)

    @pl.when(step == 0)
    def _():
        x2_ref[...] = jnp.zeros_like(x2_ref)
        h2_ref[...] = jnp.zeros_like(h2_ref)

    @pl.when(step % steps_per_seq == 0)
    def _():
        sr_ref[...] = jnp.zeros_like(sr_ref)
        sg_ref[...] = jnp.zeros_like(sg_ref)

    def scores(ql, kl):
        out = []
        for col in range(QK_W // LANE):
            lanes = slice(col * LANE, (col + 1) * LANE)
            w = jnp.concatenate([kl[:, lanes]] * 2, axis=0).T * kmask_ref[...]
            out.append(_dot(ql[:, lanes], w))
        return jnp.concatenate(out, axis=1)

    def attend(s, q_in, v, state):
        s = s.astype(BF16)
        zero = jnp.zeros((CHUNK, HEAD_V), BF16)
        out = []
        for col in range(QK_W // LANE):
            lanes = slice(col * LANE, (col + 1) * LANE)
            rhs = []
            for h in (2 * col, 2 * col + 1):
                vh = v[:, h * HEAD_V:(h + 1) * HEAD_V]
                rhs.append([vh, zero] if h % 2 == 0 else [zero, vh])
            for h in (2 * col, 2 * col + 1):
                sh = state[h * HEAD_QK:(h + 1) * HEAD_QK, :].astype(BF16)
                rhs.append([sh, zero] if h % 2 == 0 else [zero, sh])
            rhs = jnp.concatenate([jnp.concatenate(r, axis=1) for r in rhs], axis=0)
            out.append(_dot(jnp.concatenate([s[:, lanes], q_in[:, lanes]], axis=1), rhs))
        return jnp.concatenate(out, axis=1)

    def state_update(k_out, v):
        k_t = k_out.T
        return jnp.concatenate([_dot(k_t[h * HEAD_QK:(h + 1) * HEAD_QK, :], v[:, h * HEAD_V:(h + 1) * HEAD_V])
                                for h in range(N_HEADS)], axis=0)

    def mixer_stages():
        state = [sr_ref[...], sg_ref[...]]
        for first in range(0, rows_per_step // CHUNK, MIXER_GROUP_CHUNKS):
            yield from chunk_group_stages([pl.ds((first + c) * CHUNK, CHUNK)
                                           for c in range(MIXER_GROUP_CHUNKS)], state)
        sr_ref[...], sg_ref[...] = state

    def chunk_group_stages(chunk_rows, state):
        q_r = [proj_ref[r, RQ:RQ + QK_W] for r in chunk_rows]
        k_r = [proj_ref[r, RK:RK + QK_W] for r in chunk_rows]
        v_r = [proj_ref[r, RV:RV + V_W] for r in chunk_rows]
        s_r = [scores(q, k) * dret_ref[...] for q, k in zip(q_r, k_r)]
        qin_r = [(q.astype(F32) * gq_ref[...]).astype(BF16) for q in q_r]
        yield
        kv_r = [state_update((k.astype(F32) * gk_ref[...]).astype(BF16), v) for k, v in zip(k_r, v_r)]
        yield

        b = []
        for r in chunk_rows:
            la = la_ref[r, :]
            la_hi = la.astype(BF16)
            la_lo = (la - la_hi.astype(F32)).astype(BF16)
            b.append(_dot(ltri_ref[...], jnp.concatenate([la_hi, la_lo], axis=0)) * LOG2_E)
        b_last = [jnp.broadcast_to(bc[CHUNK - 1:CHUNK, :], (2 * CHUNK, QK_W)) for bc in b]
        a_col = [jnp.exp2(bl.T)[:, :LANE] for bl in b_last]
        yield
        q_g = [proj_ref[r, GQ:GQ + QK_W] for r in chunk_rows]
        k_g = [proj_ref[r, GK:GK + QK_W] for r in chunk_rows]
        v_g = [proj_ref[r, GV:GV + V_W] for r in chunk_rows]
        s_g = [scores(q, k) * lmask_ref[0] for q, k in zip(q_g, k_g)]
        for l, s in enumerate(GLA_LEVELS):
            for c in range(len(chunk_rows)):
                if c % 4 == 0:
                    yield
                f = jnp.exp2(-jnp.abs(b[c] - _midpoint_rows(b[c], s))).astype(BF16)
                s_g[c] = s_g[c] + scores(q_g[c] * f, k_g[c] * f) * lmask_ref[1 + l]
        yield
        qin_g = [q * jnp.exp2(bc).astype(BF16) for q, bc in zip(q_g, b)]
        kv_g = [state_update(k * jnp.exp2(bl[:CHUNK, :] - bc).astype(BF16), v)
                for k, bc, bl, v in zip(k_g, b, b_last, v_g)]

        sr, sg = state
        for c, r in enumerate(chunk_rows):
            if c % 2 == 0:
                yield
            o = attend(s_r[c], qin_r[c], v_r[c], sr)
            sr = sr * aret_ref[...] + kv_r[c]
            gate = proj_ref[r, RG:RG + V_W].astype(F32)
            o_ref[r, 0:V_W] = _head_norm_gate(o, rn_ref[...], gate).astype(BF16)
            o = attend(s_g[c], qin_g[c], v_g[c], sg)
            sg = sg * a_col[c] + kv_g[c]
            gate = proj_ref[r, GG:GG + V_W].astype(F32)
            o_ref[r, V_W:2 * V_W] = _head_norm_gate(o, gn_ref[...], gate).astype(BF16)
        state[:] = [sr, sg]

    def finish_previous(rows):
        h = h2_ref[rows, :]
        for c in range(D_FF // MXU_N):
            cols = slice(c * MXU_N, (c + 1) * MXU_N)
            gate = _dot(h, wg_ref[:, cols])
            up = _dot(h, wu_ref[:, cols])
            act_ref[rows, cols] = (_silu(gate) * up).astype(BF16)
            yield
        y = _dot(act_ref[rows, :], wd_ref[...])
        out_ref[rows, :] = _rms(x2_ref[rows, :] + 0.5 * y, gf_ref[...])

    n_half = FFN_STREAM_ROWS
    mixer = mixer_stages()
    halves = [finish_previous(pl.ds(i * n_half, n_half)) for i in range(rows_per_step // n_half)]
    for n_mixer in MIXER_STAGES_PER_ROUND:
        for g in halves:
            next(g, None)
        for _ in range(n_mixer):
            next(mixer, None)
    for g in halves + [mixer]:
        for _ in g:
            pass
    for i in range(rows_per_step // n_half):
        rows = pl.ds(i * n_half, n_half)
        x2 = x1_ref[rows, :] + _dot(o_ref[rows, :], wout_ref[...])
        x2_ref[rows, :] = x2
        h2_ref[rows, :] = _rms(x2, g2_ref[...]).astype(BF16)


def _gla_tables():
    t = np.arange(CHUNK)[:, None]
    u = np.arange(CHUNK)[None, :]
    masks = [np.eye(CHUNK, dtype=bool)]
    for s in GLA_LEVELS:
        same_block = (t // s) == (u // s)
        masks.append(same_block & (((t % s) < s // 2) != ((u % s) < s // 2)))
    ltri = (u <= t).astype(np.float32)
    ltri2 = np.concatenate([ltri, ltri], axis=1)
    lmask = np.stack([np.tile(m, (1, N_HEADS)) for m in masks]).astype(np.float32)
    return ltri2, lmask


def _retention_tables():
    gamma = 1.0 - 2.0 ** (-5.0 - np.arange(N_HEADS, dtype=np.float64))
    head_of_lane = np.arange(QK_W) // HEAD_QK
    i = np.arange(CHUNK)
    dist = np.abs(i[:, None] - i[None, :])
    dret = np.concatenate([gamma[h] ** dist for h in range(N_HEADS)], axis=1)
    gq = gamma[head_of_lane][None, :] ** (i[:, None] + 1.0)
    gk = gamma[head_of_lane][None, :] ** (CHUNK - 1.0 - i[:, None])
    aret = np.broadcast_to((gamma[head_of_lane] ** CHUNK)[:, None], (QK_W, HEAD_V))
    return tuple(np.asarray(a, np.float32) for a in (dret, gq, gk, aret))


def _rotary_tables(seq):
    half = HEAD_QK // 2
    inv = ROPE_BASE ** (-np.arange(half, dtype=np.float64) * 2.0 / HEAD_QK)
    ang = np.arange(seq, dtype=np.float64)[:, None] * inv[None, :]
    sign = np.where((np.arange(LANE) % HEAD_QK) < half, -1.0, 1.0)
    cos = np.tile(np.cos(ang), (1, LANE // half))
    sin = np.tile(np.sin(ang), (1, LANE // half)) * sign[None, :]
    return cos.astype(np.float32), sin.astype(np.float32)


def _key_mask():
    idx = np.arange(LANE) // HEAD_QK
    return idx[:, None] == idx[None, :]


def _const_spec(shape):
    nd = len(shape)
    return pl.BlockSpec(shape, lambda *_: (0,) * nd, pipeline_mode=pl.Buffered(1))


def _slab_spec(shape, n_steps):
    rows, cols = shape
    slab = next(s for s in range(BF16_SUBLANES, rows + 1, BF16_SUBLANES)
                if rows % s == 0 and n_steps % (rows // s) == 0 and rows // s <= n_steps)
    repeat = n_steps // (rows // slab)
    return pl.BlockSpec((slab, cols), lambda i: (i // repeat, 0))


def _tile_rows(total_rows):
    tile = 512
    assert total_rows % tile == 0
    return tile


@jax.jit
def kernel(x, ffn1_norm_g, ffn1_w_gate, ffn1_w_up, ffn1_w_down, mix_norm_g, w_in, ret_norm_g, gla_w_a2,
           gla_b_a, gla_norm_g, w_out, ffn2_norm_g, ffn2_w_gate, ffn2_w_up, ffn2_w_down, final_norm_g):
    bsz, seq, d = x.shape
    assert d == D_MODEL and seq % CHUNK == 0 and ffn1_norm_g.shape[0] == 1
    tile = _tile_rows(seq)
    steps = seq // tile
    n_rows = bsz * seq

    row = lambda g: g.reshape(1, -1).astype(F32)
    w_in_t = jnp.swapaxes(w_in, 1, 2)[0]
    w_a2 = jnp.pad(gla_w_a2[0], ((0, LANE - GATE_RANK), (0, 0))).astype(BF16)

    cos, sin = _rotary_tables(seq)

    x2d = x.reshape(n_rows, d)
    n_tiles = n_rows // tile
    row_spec = lambda w: pl.BlockSpec((tile, w), lambda i: (i, 0))
    pos_spec = pl.BlockSpec((tile, LANE), lambda i: (i % steps, 0))
    hbm_spec = pl.BlockSpec(memory_space=pl.ANY)
    later = [w_out[0], ffn2_w_gate[0], ffn2_w_up[0], ffn2_w_down[0]]
    later_specs = [_slab_spec(w.shape, n_tiles) for w in later]
    x1, proj, la, w_out_b, wg2_b, wu2_b, wd2_b = pl.pallas_call(
        _ffn1_proj_kernel,
        grid=(n_tiles,),
        in_specs=[row_spec(d), pos_spec, pos_spec, _const_spec((1, d)),
                  hbm_spec, hbm_spec, hbm_spec, _const_spec((1, d)), hbm_spec,
                  _const_spec((LANE, QK_W)), _const_spec((1, QK_W))] + later_specs,
        out_specs=[row_spec(d), row_spec(PROJ_W), row_spec(QK_W)] + later_specs,
        out_shape=[jax.ShapeDtypeStruct((n_rows, d), F32),
                   jax.ShapeDtypeStruct((n_rows, PROJ_W), BF16),
                   jax.ShapeDtypeStruct((n_rows, QK_W), F32)]
                  + [jax.ShapeDtypeStruct(w.shape, BF16) for w in later],
        scratch_shapes=[pltpu.VMEM((tile, D_FF), BF16),
                        pltpu.VMEM((d, D_FF), BF16), pltpu.VMEM((d, D_FF), BF16), pltpu.VMEM((D_FF, d), BF16),
                        pltpu.VMEM((PROJ_W, d), BF16), pltpu.VMEM((LANE, d), BF16),
                        pltpu.VMEM((WEIGHT_STAGE_SLOTS, WEIGHT_CHUNK_ROWS, D_FF), F32),
                        pltpu.SemaphoreType.DMA((WEIGHT_STAGE_SLOTS,))],
        compiler_params=pltpu.CompilerParams(dimension_semantics=("arbitrary",),
                                             vmem_limit_bytes=VMEM_LIMIT),
        name="ffn1_proj",
    )(x2d, cos, sin, row(ffn1_norm_g), ffn1_w_gate[0], ffn1_w_up[0], ffn1_w_down[0], row(mix_norm_g),
      w_in_t, w_a2, row(gla_b_a), *later)

    ltri2, lmask = _gla_tables()
    dret, gq, gk, aret = _retention_tables()
    consts = [jnp.asarray(_key_mask(), BF16), jnp.asarray(dret), jnp.asarray(gq),
              jnp.asarray(gk), jnp.asarray(aret), jnp.asarray(ltri2, BF16), jnp.asarray(lmask)]

    tile = MIXER_TILE_ROWS
    assert seq % tile == 0
    n_tiles = n_rows // tile
    tok_spec = lambda w: pl.BlockSpec((tile, w), lambda i: (jnp.minimum(i, n_tiles - 1), 0))
    out = pl.pallas_call(
        functools.partial(_mixer_ffn2_kernel, steps_per_seq=seq // tile),
        grid=(n_tiles + 1,),
        in_specs=[tok_spec(d), tok_spec(PROJ_W), tok_spec(QK_W)]
                 + [_const_spec(c.shape) for c in consts]
                 + [_const_spec((1, V_W)), _const_spec((1, V_W)), _const_spec((d, d)), _const_spec((1, d)),
                    _const_spec((d, D_FF)), _const_spec((d, D_FF)), _const_spec((D_FF, d)),
                    _const_spec((1, d))],
        out_specs=pl.BlockSpec((tile, d), lambda i: (jnp.maximum(i - 1, 0), 0)),
        out_shape=jax.ShapeDtypeStruct((n_rows, d), F32),
        scratch_shapes=[pltpu.VMEM((tile, 2 * V_W), BF16), pltpu.VMEM((tile, D_FF), BF16),
                        pltpu.VMEM((tile, d), F32), pltpu.VMEM((tile, d), BF16),
                        pltpu.VMEM((QK_W, HEAD_V), F32), pltpu.VMEM((QK_W, HEAD_V), F32)],
        compiler_params=pltpu.CompilerParams(dimension_semantics=("arbitrary",),
                                             vmem_limit_bytes=VMEM_LIMIT),
        name="mixer_ffn2",
    )(x1, proj, la, *consts, row(ret_norm_g), row(gla_norm_g), w_out_b, row(ffn2_norm_g),
      wg2_b, wu2_b, wd2_b, row(final_norm_g))
    return out.reshape(bsz, seq, d)
```

```python
import functools

import numpy as np
import jax
import jax.numpy as jnp
from jax import lax
from jax.experimental import pallas as pl
from jax.experimental.pallas import tpu as pltpu

D_MODEL = 1024
CHUNK = 64
RMS_EPS = 1e-6
ROPE_BASE = 10000.0
N_HEADS = 4
QK_W = 256
V_W = 512
HEAD_V = V_W // N_HEADS
HEAD_QK = QK_W // N_HEADS
GATE_RANK = 16
GATE_NORM = 16.0
D_FF = 2816
PROJ_W = 2 * (2 * QK_W + 2 * V_W)
RQ, RK, RV, RG = 0, 256, 512, 1024
GQ, GK, GV, GG = 1536, 1792, 2048, 2560
GLOW = 3072

LANE = 128
SUBLANES = 8
BF16_SUBLANES = 16
MXU_N = 256
FFN_STREAM_ROWS = 512
MIXER_TILE_ROWS = 512
WEIGHT_CHUNK_ROWS = 128
WEIGHT_STAGE_SLOTS = 4
VMEM_LIMIT = 56 * 1024 * 1024

GLA_LEVELS = (64, 32, 16, 8, 4, 2)
MIXER_GROUP_CHUNKS = 4
MIXER_STAGES_PER_ROUND = (3, 2) * 5 + (2,)
LOG2_E = 1.4426950408889634

BF16 = jnp.bfloat16
F32 = jnp.float32


def _dot(a, b):
    return jnp.dot(a, b, preferred_element_type=F32)


def _dot_nt(a, b):
    return lax.dot_general(a, b, (((1,), (1,)), ((), ())), preferred_element_type=F32)


def _rms(x, g):
    ms = jnp.mean(x * x, axis=-1, keepdims=True)
    return x * lax.rsqrt(ms + RMS_EPS) * g


def _silu(x):
    h = 0.5 * x
    return h * jnp.tanh(h) + h


def _load_weight_as_bf16(src_hbm, dst_ref, stage_ref, sem):
    n_rows, n_cols = dst_ref.shape
    assert n_rows % WEIGHT_CHUNK_ROWS == 0 and n_cols <= stage_ref.shape[2]
    n_chunks = n_rows // WEIGHT_CHUNK_ROWS
    n_slots = stage_ref.shape[0]

    def chunk_copy(k):
        slot = k % n_slots
        return pltpu.make_async_copy(
            src_hbm.at[pl.ds(k * WEIGHT_CHUNK_ROWS, WEIGHT_CHUNK_ROWS), :],
            stage_ref.at[slot, :, pl.ds(0, n_cols)], sem.at[slot])

    for k in range(min(n_slots - 1, n_chunks)):
        chunk_copy(k).start()

    def body(k, carry):
        @pl.when(k + n_slots - 1 < n_chunks)
        def _():
            chunk_copy(k + n_slots - 1).start()

        chunk_copy(k).wait()
        rows = pl.ds(pl.multiple_of(k * WEIGHT_CHUNK_ROWS, WEIGHT_CHUNK_ROWS), WEIGHT_CHUNK_ROWS)
        dst_ref[rows, :] = stage_ref[k % n_slots, :, pl.ds(0, n_cols)].astype(BF16)
        return carry

    lax.fori_loop(0, n_chunks, body, 0)


def _ffn1_proj_kernel(x_ref, cos_ref, sin_ref, g1_ref, wg_hbm, wu_hbm, wd_hbm, gm_ref, win_hbm,
                      wa2_ref, ba_ref, *rest):
    later_f32, (x1_ref, proj_ref, la_ref), later_bf16 = rest[:4], rest[4:7], rest[7:11]
    act_ref, wg_ref, wu_ref, wd_ref, win_ref, wlow_ref, stage_ref, sem = rest[11:]

    @pl.when(pl.program_id(0) == 0)
    def _():
        for src, dst in ((wg_hbm, wg_ref), (wu_hbm, wu_ref), (wd_hbm, wd_ref), (win_hbm, win_ref)):
            _load_weight_as_bf16(src, dst, stage_ref, sem)
        n_cols = wlow_ref.shape[1]
        tail = pltpu.make_async_copy(win_hbm.at[pl.ds(GLOW, GATE_RANK), :],
                                     stage_ref.at[0, pl.ds(0, GATE_RANK), pl.ds(0, n_cols)], sem.at[0])
        tail.start()
        wlow_ref[...] = jnp.zeros_like(wlow_ref)
        tail.wait()
        wlow_ref[0:GATE_RANK, :] = stage_ref[0, 0:GATE_RANK, 0:n_cols].astype(BF16)

    def cast_later_weights():
        yield
        for src, dst in zip(later_f32, later_bf16):
            dst[...] = src[...].astype(BF16)

    def half_tile(rows):
        x = x_ref[rows, :]
        h = _rms(x, g1_ref[...]).astype(BF16)
        yield
        for c in range(D_FF // MXU_N):
            cols = slice(c * MXU_N, (c + 1) * MXU_N)
            gate = _dot(h, wg_ref[:, cols])
            up = _dot(h, wu_ref[:, cols])
            act_ref[rows, cols] = (_silu(gate) * up).astype(BF16)
        yield
        x1 = x + 0.5 * _dot(act_ref[rows, :], wd_ref[...])
        x1_ref[rows, :] = x1
        h = _rms(x1, gm_ref[...]).astype(BF16)
        yield
        cos = cos_ref[rows, :]
        sin = sin_ref[rows, :]
        first_half = (lax.broadcasted_iota(jnp.int32, cos.shape, 1) % HEAD_QK) < HEAD_QK // 2
        for off, scale in ((RQ, 1.0), (RK, HEAD_QK ** -0.5)):
            t = _dot_nt(h, win_ref[off:off + QK_W, :])
            for col in range(QK_W // LANE):
                tc = t[:, col * LANE:(col + 1) * LANE]
                partner = jnp.where(first_half, pltpu.roll(tc, LANE - HEAD_QK // 2, 1),
                                    pltpu.roll(tc, HEAD_QK // 2, 1))
                proj_ref[rows, off + col * LANE:off + (col + 1) * LANE] = (
                    (tc * cos + partner * sin) * scale).astype(BF16)
        t = _dot_nt(h, win_ref[RV:GQ, :])
        proj_ref[rows, RV:RG] = t[:, :RG - RV].astype(BF16)
        proj_ref[rows, RG:GQ] = _silu(t[:, RG - RV:]).astype(BF16)
        proj_ref[rows, GQ:GK] = (_dot_nt(h, win_ref[GQ:GK, :]) * (HEAD_QK ** -0.5)).astype(BF16)
        t = _dot_nt(h, win_ref[GK:PROJ_W, :])
        proj_ref[rows, GK:GG] = t[:, :GG - GK].astype(BF16)
        proj_ref[rows, GG:PROJ_W] = _silu(t[:, GG - GK:]).astype(BF16)
        low = _dot_nt(h, wlow_ref[...]).astype(BF16)
        yield
        z = _dot(low, wa2_ref[...]) + ba_ref[...]
        la_ref[rows, :] = (jnp.minimum(z, 0.0) - jnp.log(1.0 + jnp.exp(-jnp.abs(z)))) * (1.0 / GATE_NORM)

    n_half = x_ref.shape[0] // 2
    streams = [half_tile(pl.ds(i * n_half, n_half)) for i in range(2)] + [cast_later_weights()]
    while streams:
        streams = [g for g in streams if next(g, StopIteration) is not StopIteration]


def _head_norm_gate(o, norm_g, gate):
    outs = []
    for h in range(N_HEADS):
        cols = slice(h * HEAD_V, (h + 1) * HEAD_V)
        oh = o[:, cols]
        ms = jnp.mean(oh * oh, axis=-1, keepdims=True)
        outs.append(oh * lax.rsqrt(ms + RMS_EPS) * norm_g[:, cols] * gate[:, cols])
    return jnp.concatenate(outs, axis=-1)


def _midpoint_rows(b, s):
    rows, w = b.shape
    if s >= SUBLANES:
        blocks = b.reshape(rows // s, s, w)
        return jnp.broadcast_to(blocks[:, s // 2 - 1:s // 2, :], blocks.shape).reshape(rows, w)
    groups = b.reshape(rows // SUBLANES, SUBLANES, w)
    sub = lax.broadcasted_iota(jnp.int32, groups.shape, 1)
    out = None
    for first in reversed(range(0, SUBLANES, s)):
        ref_row = jnp.broadcast_to(groups[:, first + s // 2 - 1:first + s // 2, :], groups.shape)
        out = ref_row if out is None else jnp.where(sub < first + s, ref_row, out)
    return out.reshape(rows, w)


def _mixer_ffn2_kernel(x1_ref, proj_ref, la_ref,
                       kmask_ref, dret_ref, gq_ref, gk_ref, aret_ref, ltri_ref, lmask_ref,
                       rn_ref, gn_ref, wout_ref, g2_ref, wg_ref, wu_ref, wd_ref, gf_ref,
                       out_ref, o_ref, act_ref, x2_ref, h2_ref, sr_ref, sg_ref, *, steps_per_seq):
    rows_per_step = x1_ref.shape[0]
    step = pl.program_id(0)

    @pl.when(step == 0)
    def _():
        x2_ref[...] = jnp.zeros_like(x2_ref)
        h2_ref[...] = jnp.zeros_like(h2_ref)

    @pl.when(step % steps_per_seq == 0)
    def _():
        sr_ref[...] = jnp.zeros_like(sr_ref)
        sg_ref[...] = jnp.zeros_like(sg_ref)

    def scores(ql, kl):
        out = []
        for col in range(QK_W // LANE):
            lanes = slice(col * LANE, (col + 1) * LANE)
            w = jnp.concatenate([kl[:, lanes]] * 2, axis=0).T * kmask_ref[...]
            out.append(_dot(ql[:, lanes], w))
        return jnp.concatenate(out, axis=1)

    def attend(s, q_in, v, state):
        s = s.astype(BF16)
        zero = jnp.zeros((CHUNK, HEAD_V), BF16)
        out = []
        for col in range(QK_W // LANE):
            lanes = slice(col * LANE, (col + 1) * LANE)
            rhs = []
            for h in (2 * col, 2 * col + 1):
                vh = v[:, h * HEAD_V:(h + 1) * HEAD_V]
                rhs.append([vh, zero] if h % 2 == 0 else [zero, vh])
            for h in (2 * col, 2 * col + 1):
                sh = state[h * HEAD_QK:(h + 1) * HEAD_QK, :].astype(BF16)
                rhs.append([sh, zero] if h % 2 == 0 else [zero, sh])
            rhs = jnp.concatenate([jnp.concatenate(r, axis=1) for r in rhs], axis=0)
            out.append(_dot(jnp.concatenate([s[:, lanes], q_in[:, lanes]], axis=1), rhs))
        return jnp.concatenate(out, axis=1)

    def state_update(k_out, v):
        k_t = k_out.T
        return jnp.concatenate([_dot(k_t[h * HEAD_QK:(h + 1) * HEAD_QK, :], v[:, h * HEAD_V:(h + 1) * HEAD_V])
                                for h in range(N_HEADS)], axis=0)

    def mixer_stages():
        state = [sr_ref[...], sg_ref[...]]
        for first in range(0, rows_per_step // CHUNK, MIXER_GROUP_CHUNKS):
            yield from chunk_group_stages([pl.ds((first + c) * CHUNK, CHUNK)
                                           for c in range(MIXER_GROUP_CHUNKS)], state)
        sr_ref[...], sg_ref[...] = state

    def chunk_group_stages(chunk_rows, state):
        q_r = [proj_ref[r, RQ:RQ + QK_W] for r in chunk_rows]
        k_r = [proj_ref[r, RK:RK + QK_W] for r in chunk_rows]
        v_r = [proj_ref[r, RV:RV + V_W] for r in chunk_rows]
        s_r = [scores(q, k) * dret_ref[...] for q, k in zip(q_r, k_r)]
        qin_r = [(q.astype(F32) * gq_ref[...]).astype(BF16) for q in q_r]
        yield
        kv_r = [state_update((k.astype(F32) * gk_ref[...]).astype(BF16), v) for k, v in zip(k_r, v_r)]
        yield

        b = []
        for r in chunk_rows:
            la = la_ref[r, :]
            la_hi = la.astype(BF16)
            la_lo = (la - la_hi.astype(F32)).astype(BF16)
            b.append(_dot(ltri_ref[...], jnp.concatenate([la_hi, la_lo], axis=0)) * LOG2_E)
        b_last = [jnp.broadcast_to(bc[CHUNK - 1:CHUNK, :], (2 * CHUNK, QK_W)) for bc in b]
        a_col = [jnp.exp2(bl.T)[:, :LANE] for bl in b_last]
        yield
        q_g = [proj_ref[r, GQ:GQ + QK_W] for r in chunk_rows]
        k_g = [proj_ref[r, GK:GK + QK_W] for r in chunk_rows]
        v_g = [proj_ref[r, GV:GV + V_W] for r in chunk_rows]
        s_g = [scores(q, k) * lmask_ref[0] for q, k in zip(q_g, k_g)]
        for l, s in enumerate(GLA_LEVELS):
            for c in range(len(chunk_rows)):
                if c % 4 == 0:
                    yield
                f = jnp.exp2(-jnp.abs(b[c] - _midpoint_rows(b[c], s))).astype(BF16)
                s_g[c] = s_g[c] + scores(q_g[c] * f, k_g[c] * f) * lmask_ref[1 + l]
        yield
        qin_g = [q * jnp.exp2(bc).astype(BF16) for q, bc in zip(q_g, b)]
        kv_g = [state_update(k * jnp.exp2(bl[:CHUNK, :] - bc).astype(BF16), v)
                for k, bc, bl, v in zip(k_g, b, b_last, v_g)]

        sr, sg = state
        for c, r in enumerate(chunk_rows):
            if c % 2 == 0:
                yield
            o = attend(s_r[c], qin_r[c], v_r[c], sr)
            sr = sr * aret_ref[...] + kv_r[c]
            gate = proj_ref[r, RG:RG + V_W].astype(F32)
            o_ref[r, 0:V_W] = _head_norm_gate(o, rn_ref[...], gate).astype(BF16)
            o = attend(s_g[c], qin_g[c], v_g[c], sg)
            sg = sg * a_col[c] + kv_g[c]
            gate = proj_ref[r, GG:GG + V_W].astype(F32)
            o_ref[r, V_W:2 * V_W] = _head_norm_gate(o, gn_ref[...], gate).astype(BF16)
        state[:] = [sr, sg]

    def finish_previous(rows):
        h = h2_ref[rows, :]
        for c in range(D_FF // MXU_N):
            cols = slice(c * MXU_N, (c + 1) * MXU_N)
            gate = _dot(h, wg_ref[:, cols])
            up = _dot(h, wu_ref[:, cols])
            act_ref[rows, cols] = (_silu(gate) * up).astype(BF16)
            yield
        y = _dot(act_ref[rows, :], wd_ref[...])
        out_ref[rows, :] = _rms(x2_ref[rows, :] + 0.5 * y, gf_ref[...])

    n_rows = FFN_STREAM_ROWS
    mixer = mixer_stages()
    ffn_streams = [finish_previous(pl.ds(i * n_rows, n_rows)) for i in range(rows_per_step // n_rows)]
    for n_mixer in MIXER_STAGES_PER_ROUND:
        for g in ffn_streams:
            next(g, None)
        for _ in range(n_mixer):
            next(mixer, None)
    for g in ffn_streams + [mixer]:
        for _ in g:
            pass
    for i in range(rows_per_step // n_rows):
        rows = pl.ds(i * n_rows, n_rows)
        x2 = x1_ref[rows, :] + _dot(o_ref[rows, :], wout_ref[...])
        x2_ref[rows, :] = x2
        h2_ref[rows, :] = _rms(x2, g2_ref[...]).astype(BF16)


def _gla_tables():
    t = np.arange(CHUNK)[:, None]
    u = np.arange(CHUNK)[None, :]
    masks = [np.eye(CHUNK, dtype=bool)]
    for s in GLA_LEVELS:
        same_block = (t // s) == (u // s)
        masks.append(same_block & (((t % s) < s // 2) != ((u % s) < s // 2)))
    ltri = (u <= t).astype(np.float32)
    ltri2 = np.concatenate([ltri, ltri], axis=1)
    lmask = np.stack([np.tile(m, (1, N_HEADS)) for m in masks]).astype(np.float32)
    return ltri2, lmask


def _retention_tables():
    gamma = 1.0 - 2.0 ** (-5.0 - np.arange(N_HEADS, dtype=np.float64))
    head_of_lane = np.arange(QK_W) // HEAD_QK
    i = np.arange(CHUNK)
    dist = np.abs(i[:, None] - i[None, :])
    dret = np.concatenate([gamma[h] ** dist for h in range(N_HEADS)], axis=1)
    gq = gamma[head_of_lane][None, :] ** (i[:, None] + 1.0)
    gk = gamma[head_of_lane][None, :] ** (CHUNK - 1.0 - i[:, None])
    aret = np.broadcast_to((gamma[head_of_lane] ** CHUNK)[:, None], (QK_W, HEAD_V))
    return tuple(np.asarray(a, np.float32) for a in (dret, gq, gk, aret))


def _rotary_tables(seq):
    half = HEAD_QK // 2
    inv = ROPE_BASE ** (-np.arange(half, dtype=np.float64) * 2.0 / HEAD_QK)
    ang = np.arange(seq, dtype=np.float64)[:, None] * inv[None, :]
    sign = np.where((np.arange(LANE) % HEAD_QK) < half, -1.0, 1.0)
    cos = np.tile(np.cos(ang), (1, LANE // half))
    sin = np.tile(np.sin(ang), (1, LANE // half)) * sign[None, :]
    return cos.astype(np.float32), sin.astype(np.float32)


def _key_mask():
    idx = np.arange(LANE) // HEAD_QK
    return idx[:, None] == idx[None, :]


def _const_spec(shape):
    nd = len(shape)
    return pl.BlockSpec(shape, lambda *_: (0,) * nd, pipeline_mode=pl.Buffered(1))


def _slab_spec(shape, n_steps):
    rows, cols = shape
    slab = next(s for s in range(BF16_SUBLANES, rows + 1, BF16_SUBLANES)
                if rows % s == 0 and n_steps % (rows // s) == 0 and rows // s <= n_steps)
    repeat = n_steps // (rows // slab)
    return pl.BlockSpec((slab, cols), lambda i: (i // repeat, 0))


def _tile_rows(total_rows):
    tile = 512
    assert total_rows % tile == 0
    return tile


@jax.jit
def kernel(x, ffn1_norm_g, ffn1_w_gate, ffn1_w_up, ffn1_w_down, mix_norm_g, w_in, ret_norm_g, gla_w_a2,
           gla_b_a, gla_norm_g, w_out, ffn2_norm_g, ffn2_w_gate, ffn2_w_up, ffn2_w_down, final_norm_g):
    bsz, seq, d = x.shape
    assert d == D_MODEL and seq % CHUNK == 0 and ffn1_norm_g.shape[0] == 1
    tile = _tile_rows(seq)
    steps = seq // tile
    n_rows = bsz * seq

    row = lambda g: g.reshape(1, -1).astype(F32)
    w_in_t = jnp.swapaxes(w_in, 1, 2)[0]
    w_a2 = jnp.pad(gla_w_a2[0], ((0, LANE - GATE_RANK), (0, 0))).astype(BF16)

    cos, sin = _rotary_tables(seq)

    x2d = x.reshape(n_rows, d)
    n_tiles = n_rows // tile
    row_spec = lambda w: pl.BlockSpec((tile, w), lambda i: (i, 0))
    pos_spec = pl.BlockSpec((tile, LANE), lambda i: (i % steps, 0))
    hbm_spec = pl.BlockSpec(memory_space=pl.ANY)
    later = [w_out[0], ffn2_w_gate[0], ffn2_w_up[0], ffn2_w_down[0]]
    later_specs = [_slab_spec(w.shape, n_tiles) for w in later]
    x1, proj, la, w_out_b, wg2_b, wu2_b, wd2_b = pl.pallas_call(
        _ffn1_proj_kernel,
        grid=(n_tiles,),
        in_specs=[row_spec(d), pos_spec, pos_spec, _const_spec((1, d)),
                  hbm_spec, hbm_spec, hbm_spec, _const_spec((1, d)), hbm_spec,
                  _const_spec((LANE, QK_W)), _const_spec((1, QK_W))] + later_specs,
        out_specs=[row_spec(d), row_spec(PROJ_W), row_spec(QK_W)] + later_specs,
        out_shape=[jax.ShapeDtypeStruct((n_rows, d), F32),
                   jax.ShapeDtypeStruct((n_rows, PROJ_W), BF16),
                   jax.ShapeDtypeStruct((n_rows, QK_W), F32)]
                  + [jax.ShapeDtypeStruct(w.shape, BF16) for w in later],
        scratch_shapes=[pltpu.VMEM((tile, D_FF), BF16),
                        pltpu.VMEM((d, D_FF), BF16), pltpu.VMEM((d, D_FF), BF16), pltpu.VMEM((D_FF, d), BF16),
                        pltpu.VMEM((PROJ_W, d), BF16), pltpu.VMEM((LANE, d), BF16),
                        pltpu.VMEM((WEIGHT_STAGE_SLOTS, WEIGHT_CHUNK_ROWS, D_FF), F32),
                        pltpu.SemaphoreType.DMA((WEIGHT_STAGE_SLOTS,))],
        compiler_params=pltpu.CompilerParams(dimension_semantics=("arbitrary",),
                                             vmem_limit_bytes=VMEM_LIMIT),
        name="ffn1_proj",
    )(x2d, cos, sin, row(ffn1_norm_g), ffn1_w_gate[0], ffn1_w_up[0], ffn1_w_down[0], row(mix_norm_g),
      w_in_t, w_a2, row(gla_b_a), *later)

    ltri2, lmask = _gla_tables()
    dret, gq, gk, aret = _retention_tables()
    consts = [jnp.asarray(_key_mask(), BF16), jnp.asarray(dret), jnp.asarray(gq),
              jnp.asarray(gk), jnp.asarray(aret), jnp.asarray(ltri2, BF16), jnp.asarray(lmask)]

    tile = MIXER_TILE_ROWS
    assert seq % tile == 0
    n_tiles = n_rows // tile
    tok_spec = lambda w: pl.BlockSpec((tile, w), lambda i: (jnp.minimum(i, n_tiles - 1), 0))
    out = pl.pallas_call(
        functools.partial(_mixer_ffn2_kernel, steps_per_seq=seq // tile),
        grid=(n_tiles + 1,),
        in_specs=[tok_spec(d), tok_spec(PROJ_W), tok_spec(QK_W)]
                 + [_const_spec(c.shape) for c in consts]
                 + [_const_spec((1, V_W)), _const_spec((1, V_W)), _const_spec((d, d)), _const_spec((1, d)),
                    _const_spec((d, D_FF)), _const_spec((d, D_FF)), _const_spec((D_FF, d)),
                    _const_spec((1, d))],
        out_specs=pl.BlockSpec((tile, d), lambda i: (jnp.maximum(i - 1, 0), 0)),
        out_shape=jax.ShapeDtypeStruct((n_rows, d), F32),
        scratch_shapes=[pltpu.VMEM((tile, 2 * V_W), BF16), pltpu.VMEM((tile, D_FF), BF16),
                        pltpu.VMEM((tile, d), F32), pltpu.VMEM((tile, d), BF16),
                        pltpu.VMEM((QK_W, HEAD_V), F32), pltpu.VMEM((QK_W, HEAD_V), F32)],
        compiler_params=pltpu.CompilerParams(dimension_semantics=("arbitrary",),
                                             vmem_limit_bytes=VMEM_LIMIT),
        name="mixer_ffn2",
    )(x1, proj, la, *consts, row(ret_norm_g), row(gla_norm_g), w_out_b, row(ffn2_norm_g),
      wg2_b, wu2_b, wd2_b, row(final_norm_g))
    return out.reshape(bsz, seq, d)
```

```python
import functools

import numpy as np
import jax
import jax.numpy as jnp
from jax import lax
from jax.experimental import pallas as pl
from jax.experimental.pallas import tpu as pltpu

D_MODEL = 1024
CHUNK = 64
RMS_EPS = 1e-6
ROPE_BASE = 10000.0
N_HEADS = 4
QK_W = 256
V_W = 512
HEAD_V = V_W // N_HEADS
HEAD_QK = QK_W // N_HEADS
GATE_RANK = 16
GATE_NORM = 16.0
D_FF = 2816
PROJ_W = 2 * (2 * QK_W + 2 * V_W)
RQ, RK, RV, RG = 0, 256, 512, 1024
GQ, GK, GV, GG = 1536, 1792, 2048, 2560
GLOW = 3072

LANE = 128
SUBLANES = 8
BF16_SUBLANES = 16
MXU_N = 256
FFN_STREAM_ROWS = 512
MIXER_TILE_ROWS = 512
WEIGHT_CHUNK_ROWS = 128
WEIGHT_STAGE_SLOTS = 6
VMEM_LIMIT = 56 * 1024 * 1024

GLA_LEVELS = (64, 32, 16, 8, 4, 2)
MIXER_GROUP_CHUNKS = 4
MIXER_STAGES_PER_ROUND = (3, 2) * 5 + (2,)
LOG2_E = 1.4426950408889634

BF16 = jnp.bfloat16
F32 = jnp.float32


def _dot(a, b):
    return jnp.dot(a, b, preferred_element_type=F32)


def _dot_nt(a, b):
    return lax.dot_general(a, b, (((1,), (1,)), ((), ())), preferred_element_type=F32)


def _rms(x, g):
    ms = jnp.mean(x * x, axis=-1, keepdims=True)
    return x * lax.rsqrt(ms + RMS_EPS) * g


def _silu(x):
    h = 0.5 * x
    return h * jnp.tanh(h) + h


def _load_weight_as_bf16(src_hbm, dst_ref, stage_ref, sem):
    n_rows, n_cols = dst_ref.shape
    assert n_rows % WEIGHT_CHUNK_ROWS == 0 and n_cols <= stage_ref.shape[2]
    n_chunks = n_rows // WEIGHT_CHUNK_ROWS
    n_slots = stage_ref.shape[0]

    def chunk_copy(k):
        slot = k % n_slots
        return pltpu.make_async_copy(
            src_hbm.at[pl.ds(k * WEIGHT_CHUNK_ROWS, WEIGHT_CHUNK_ROWS), :],
            stage_ref.at[slot, :, pl.ds(0, n_cols)], sem.at[slot])

    for k in range(min(n_slots - 1, n_chunks)):
        chunk_copy(k).start()

    def body(k, carry):
        @pl.when(k + n_slots - 1 < n_chunks)
        def _():
            chunk_copy(k + n_slots - 1).start()

        chunk_copy(k).wait()
        rows = pl.ds(pl.multiple_of(k * WEIGHT_CHUNK_ROWS, WEIGHT_CHUNK_ROWS), WEIGHT_CHUNK_ROWS)
        dst_ref[rows, :] = stage_ref[k % n_slots, :, pl.ds(0, n_cols)].astype(BF16)
        return carry

    lax.fori_loop(0, n_chunks, body, 0)


def _ffn1_proj_kernel(x_ref, cos_ref, sin_ref, g1_ref, wg_hbm, wu_hbm, wd_hbm, gm_ref, win_hbm,
                      wa2_ref, ba_ref, *rest):
    later_f32, (x1_ref, proj_ref, la_ref), later_bf16 = rest[:4], rest[4:7], rest[7:11]
    act_ref, wg_ref, wu_ref, wd_ref, win_ref, wlow_ref, stage_ref, sem = rest[11:]

    @pl.when(pl.program_id(0) == 0)
    def _():
        for src, dst in ((wg_hbm, wg_ref), (wu_hbm, wu_ref), (wd_hbm, wd_ref), (win_hbm, win_ref)):
            _load_weight_as_bf16(src, dst, stage_ref, sem)
        n_cols = wlow_ref.shape[1]
        tail = pltpu.make_async_copy(win_hbm.at[pl.ds(GLOW, GATE_RANK), :],
                                     stage_ref.at[0, pl.ds(0, GATE_RANK), pl.ds(0, n_cols)], sem.at[0])
        tail.start()
        wlow_ref[...] = jnp.zeros_like(wlow_ref)
        tail.wait()
        wlow_ref[0:GATE_RANK, :] = stage_ref[0, 0:GATE_RANK, 0:n_cols].astype(BF16)

    def cast_later_weights():
        yield
        for src, dst in zip(later_f32, later_bf16):
            dst[...] = src[...].astype(BF16)

    def half_tile(rows):
        x = x_ref[rows, :]
        h = _rms(x, g1_ref[...]).astype(BF16)
        yield
        for c in range(D_FF // MXU_N):
            cols = slice(c * MXU_N, (c + 1) * MXU_N)
            gate = _dot(h, wg_ref[:, cols])
            up = _dot(h, wu_ref[:, cols])
            act_ref[rows, cols] = (_silu(gate) * up).astype(BF16)
        yield
        x1 = x + 0.5 * _dot(act_ref[rows, :], wd_ref[...])
        x1_ref[rows, :] = x1
        h = _rms(x1, gm_ref[...]).astype(BF16)
        yield
        cos = cos_ref[rows, :]
        sin = sin_ref[rows, :]
        first_half = (lax.broadcasted_iota(jnp.int32, cos.shape, 1) % HEAD_QK) < HEAD_QK // 2
        for off, scale in ((RQ, 1.0), (RK, HEAD_QK ** -0.5)):
            t = _dot_nt(h, win_ref[off:off + QK_W, :])
            for col in range(QK_W // LANE):
                tc = t[:, col * LANE:(col + 1) * LANE]
                partner = jnp.where(first_half, pltpu.roll(tc, LANE - HEAD_QK // 2, 1),
                                    pltpu.roll(tc, HEAD_QK // 2, 1))
                proj_ref[rows, off + col * LANE:off + (col + 1) * LANE] = (
                    (tc * cos + partner * sin) * scale).astype(BF16)
        t = _dot_nt(h, win_ref[RV:GQ, :])
        proj_ref[rows, RV:RG] = t[:, :RG - RV].astype(BF16)
        proj_ref[rows, RG:GQ] = _silu(t[:, RG - RV:]).astype(BF16)
        proj_ref[rows, GQ:GK] = (_dot_nt(h, win_ref[GQ:GK, :]) * (HEAD_QK ** -0.5)).astype(BF16)
        t = _dot_nt(h, win_ref[GK:PROJ_W, :])
        proj_ref[rows, GK:GG] = t[:, :GG - GK].astype(BF16)
        proj_ref[rows, GG:PROJ_W] = _silu(t[:, GG - GK:]).astype(BF16)
        low = _dot_nt(h, wlow_ref[...]).astype(BF16)
        yield
        z = _dot(low, wa2_ref[...]) + ba_ref[...]
        la_ref[rows, :] = (jnp.minimum(z, 0.0) - jnp.log(1.0 + jnp.exp(-jnp.abs(z)))) * (1.0 / GATE_NORM)

    n_half = x_ref.shape[0] // 2
    streams = [half_tile(pl.ds(i * n_half, n_half)) for i in range(2)] + [cast_later_weights()]
    while streams:
        streams = [g for g in streams if next(g, StopIteration) is not StopIteration]


def _head_norm_gate(o, norm_g, gate):
    outs = []
    for h in range(N_HEADS):
        cols = slice(h * HEAD_V, (h + 1) * HEAD_V)
        oh = o[:, cols]
        ms = jnp.mean(oh * oh, axis=-1, keepdims=True)
        outs.append(oh * lax.rsqrt(ms + RMS_EPS) * norm_g[:, cols] * gate[:, cols])
    return jnp.concatenate(outs, axis=-1)


def _midpoint_rows(b, s):
    rows, w = b.shape
    if s >= SUBLANES:
        blocks = b.reshape(rows // s, s, w)
        return jnp.broadcast_to(blocks[:, s // 2 - 1:s // 2, :], blocks.shape).reshape(rows, w)
    groups = b.reshape(rows // SUBLANES, SUBLANES, w)
    sub = lax.broadcasted_iota(jnp.int32, groups.shape, 1)
    out = None
    for first in reversed(range(0, SUBLANES, s)):
        ref_row = jnp.broadcast_to(groups[:, first + s // 2 - 1:first + s // 2, :], groups.shape)
        out = ref_row if out is None else jnp.where(sub < first + s, ref_row, out)
    return out.reshape(rows, w)


def _mixer_ffn2_kernel(x1_ref, proj_ref, la_ref,
                       kmask_ref, dret_ref, gq_ref, gk_ref, aret_ref, ltri_ref, lmask_ref,
                       rn_ref, gn_ref, wout_ref, g2_ref, wg_ref, wu_ref, wd_ref, gf_ref,
                       out_ref, o_ref, act_ref, x2_ref, h2_ref, sr_ref, sg_ref, *, steps_per_seq):
    rows_per_step = x1_ref.shape[0]
    step = pl.program_id(0)

    @pl.when(step == 0)
    def _():
        x2_ref[...] = jnp.zeros_like(x2_ref)
        h2_ref[...] = jnp.zeros_like(h2_ref)

    @pl.when(step % steps_per_seq == 0)
    def _():
        sr_ref[...] = jnp.zeros_like(sr_ref)
        sg_ref[...] = jnp.zeros_like(sg_ref)

    def scores(ql, kl):
        out = []
        for col in range(QK_W // LANE):
            lanes = slice(col * LANE, (col + 1) * LANE)
            w = jnp.concatenate([kl[:, lanes]] * 2, axis=0).T * kmask_ref[...]
            out.append(_dot(ql[:, lanes], w))
        return jnp.concatenate(out, axis=1)

    def attend(s, q_in, v, state):
        s = s.astype(BF16)
        zero = jnp.zeros((CHUNK, HEAD_V), BF16)
        out = []
        for col in range(QK_W // LANE):
            lanes = slice(col * LANE, (col + 1) * LANE)
            rhs = []
            for h in (2 * col, 2 * col + 1):
                vh = v[:, h * HEAD_V:(h + 1) * HEAD_V]
                rhs.append([vh, zero] if h % 2 == 0 else [zero, vh])
            for h in (2 * col, 2 * col + 1):
                sh = state[h * HEAD_QK:(h + 1) * HEAD_QK, :].astype(BF16)
                rhs.append([sh, zero] if h % 2 == 0 else [zero, sh])
            rhs = jnp.concatenate([jnp.concatenate(r, axis=1) for r in rhs], axis=0)
            out.append(_dot(jnp.concatenate([s[:, lanes], q_in[:, lanes]], axis=1), rhs))
        return jnp.concatenate(out, axis=1)

    def state_update(k_out, v):
        k_t = k_out.T
        return jnp.concatenate([_dot(k_t[h * HEAD_QK:(h + 1) * HEAD_QK, :], v[:, h * HEAD_V:(h + 1) * HEAD_V])
                                for h in range(N_HEADS)], axis=0)

    def mixer_stages():
        state = [sr_ref[...], sg_ref[...]]
        for first in range(0, rows_per_step // CHUNK, MIXER_GROUP_CHUNKS):
            yield from chunk_group_stages([pl.ds((first + c) * CHUNK, CHUNK)
                                           for c in range(MIXER_GROUP_CHUNKS)], state)
        sr_ref[...], sg_ref[...] = state

    def chunk_group_stages(chunk_rows, state):
        q_r = [proj_ref[r, RQ:RQ + QK_W] for r in chunk_rows]
        k_r = [proj_ref[r, RK:RK + QK_W] for r in chunk_rows]
        v_r = [proj_ref[r, RV:RV + V_W] for r in chunk_rows]
        s_r = [scores(q, k) * dret_ref[...] for q, k in zip(q_r, k_r)]
        qin_r = [(q.astype(F32) * gq_ref[...]).astype(BF16) for q in q_r]
        yield
        kv_r = [state_update((k.astype(F32) * gk_ref[...]).astype(BF16), v) for k, v in zip(k_r, v_r)]
        yield

        b = []
        for r in chunk_rows:
            la = la_ref[r, :]
            la_hi = la.astype(BF16)
            la_lo = (la - la_hi.astype(F32)).astype(BF16)
            b.append(_dot(ltri_ref[...], jnp.concatenate([la_hi, la_lo], axis=0)) * LOG2_E)
        b_last = [jnp.broadcast_to(bc[CHUNK - 1:CHUNK, :], (2 * CHUNK, QK_W)) for bc in b]
        a_col = [jnp.exp2(bl.T)[:, :LANE] for bl in b_last]
        yield
        q_g = [proj_ref[r, GQ:GQ + QK_W] for r in chunk_rows]
        k_g = [proj_ref[r, GK:GK + QK_W] for r in chunk_rows]
        v_g = [proj_ref[r, GV:GV + V_W] for r in chunk_rows]
        s_g = [scores(q, k) * lmask_ref[0] for q, k in zip(q_g, k_g)]
        for l, s in enumerate(GLA_LEVELS):
            for c in range(len(chunk_rows)):
                if c % 4 == 0:
                    yield
                f = jnp.exp2(-jnp.abs(b[c] - _midpoint_rows(b[c], s))).astype(BF16)
                s_g[c] = s_g[c] + scores(q_g[c] * f, k_g[c] * f) * lmask_ref[1 + l]
        yield
        qin_g = [q * jnp.exp2(bc).astype(BF16) for q, bc in zip(q_g, b)]
        kv_g = [state_update(k * jnp.exp2(bl[:CHUNK, :] - bc).astype(BF16), v)
                for k, bc, bl, v in zip(k_g, b, b_last, v_g)]

        sr, sg = state
        for c, r in enumerate(chunk_rows):
            if c % 2 == 0:
                yield
            o = attend(s_r[c], qin_r[c], v_r[c], sr)
            sr = sr * aret_ref[...] + kv_r[c]
            gate = proj_ref[r, RG:RG + V_W].astype(F32)
            o_ref[r, 0:V_W] = _head_norm_gate(o, rn_ref[...], gate).astype(BF16)
            o = attend(s_g[c], qin_g[c], v_g[c], sg)
            sg = sg * a_col[c] + kv_g[c]
            gate = proj_ref[r, GG:GG + V_W].astype(F32)
            o_ref[r, V_W:2 * V_W] = _head_norm_gate(o, gn_ref[...], gate).astype(BF16)
        state[:] = [sr, sg]

    def finish_previous(rows):
        h = h2_ref[rows, :]
        for c in range(D_FF // MXU_N):
            cols = slice(c * MXU_N, (c + 1) * MXU_N)
            gate = _dot(h, wg_ref[:, cols])
            up = _dot(h, wu_ref[:, cols])
            act_ref[rows, cols] = (_silu(gate) * up).astype(BF16)
            yield
        y = _dot(act_ref[rows, :], wd_ref[...])
        out_ref[rows, :] = _rms(x2_ref[rows, :] + 0.5 * y, gf_ref[...])

    n_rows = FFN_STREAM_ROWS
    mixer = mixer_stages()
    ffn_streams = [finish_previous(pl.ds(i * n_rows, n_rows)) for i in range(rows_per_step // n_rows)]
    for n_mixer in MIXER_STAGES_PER_ROUND:
        for g in ffn_streams:
            next(g, None)
        for _ in range(n_mixer):
            next(mixer, None)
    for g in ffn_streams + [mixer]:
        for _ in g:
            pass
    for i in range(rows_per_step // n_rows):
        rows = pl.ds(i * n_rows, n_rows)
        x2 = x1_ref[rows, :] + _dot(o_ref[rows, :], wout_ref[...])
        x2_ref[rows, :] = x2
        h2_ref[rows, :] = _rms(x2, g2_ref[...]).astype(BF16)


def _gla_tables():
    t = np.arange(CHUNK)[:, None]
    u = np.arange(CHUNK)[None, :]
    masks = [np.eye(CHUNK, dtype=bool)]
    for s in GLA_LEVELS:
        same_block = (t // s) == (u // s)
        masks.append(same_block & (((t % s) < s // 2) != ((u % s) < s // 2)))
    ltri = (u <= t).astype(np.float32)
    ltri2 = np.concatenate([ltri, ltri], axis=1)
    lmask = np.stack([np.tile(m, (1, N_HEADS)) for m in masks]).astype(np.float32)
    return ltri2, lmask


def _retention_tables():
    gamma = 1.0 - 2.0 ** (-5.0 - np.arange(N_HEADS, dtype=np.float64))
    head_of_lane = np.arange(QK_W) // HEAD_QK
    i = np.arange(CHUNK)
    dist = np.abs(i[:, None] - i[None, :])
    dret = np.concatenate([gamma[h] ** dist for h in range(N_HEADS)], axis=1)
    gq = gamma[head_of_lane][None, :] ** (i[:, None] + 1.0)
    gk = gamma[head_of_lane][None, :] ** (CHUNK - 1.0 - i[:, None])
    aret = np.broadcast_to((gamma[head_of_lane] ** CHUNK)[:, None], (QK_W, HEAD_V))
    return tuple(np.asarray(a, np.float32) for a in (dret, gq, gk, aret))


def _rotary_tables(seq):
    half = HEAD_QK // 2
    inv = ROPE_BASE ** (-np.arange(half, dtype=np.float64) * 2.0 / HEAD_QK)
    ang = np.arange(seq, dtype=np.float64)[:, None] * inv[None, :]
    sign = np.where((np.arange(LANE) % HEAD_QK) < half, -1.0, 1.0)
    cos = np.tile(np.cos(ang), (1, LANE // half))
    sin = np.tile(np.sin(ang), (1, LANE // half)) * sign[None, :]
    return cos.astype(np.float32), sin.astype(np.float32)


def _key_mask():
    idx = np.arange(LANE) // HEAD_QK
    return idx[:, None] == idx[None, :]


def _const_spec(shape):
    nd = len(shape)
    return pl.BlockSpec(shape, lambda *_: (0,) * nd, pipeline_mode=pl.Buffered(1))


def _slab_spec(shape, n_steps):
    rows, cols = shape
    slab = next(s for s in range(BF16_SUBLANES, rows + 1, BF16_SUBLANES)
                if rows % s == 0 and n_steps % (rows // s) == 0 and rows // s <= n_steps)
    repeat = n_steps // (rows // slab)
    return pl.BlockSpec((slab, cols), lambda i: (i // repeat, 0))


def _tile_rows(total_rows):
    tile = 512
    assert total_rows % tile == 0
    return tile


@jax.jit
def kernel(x, ffn1_norm_g, ffn1_w_gate, ffn1_w_up, ffn1_w_down, mix_norm_g, w_in, ret_norm_g, gla_w_a2,
           gla_b_a, gla_norm_g, w_out, ffn2_norm_g, ffn2_w_gate, ffn2_w_up, ffn2_w_down, final_norm_g):
    bsz, seq, d = x.shape
    assert d == D_MODEL and seq % CHUNK == 0 and ffn1_norm_g.shape[0] == 1
    tile = _tile_rows(seq)
    steps = seq // tile
    n_rows = bsz * seq

    row = lambda g: g.reshape(1, -1).astype(F32)
    w_in_t = jnp.swapaxes(w_in, 1, 2)[0]
    w_a2 = jnp.pad(gla_w_a2[0], ((0, LANE - GATE_RANK), (0, 0))).astype(BF16)

    cos, sin = _rotary_tables(seq)

    x2d = x.reshape(n_rows, d)
    n_tiles = n_rows // tile
    row_spec = lambda w: pl.BlockSpec((tile, w), lambda i: (i, 0))
    pos_spec = pl.BlockSpec((tile, LANE), lambda i: (i % steps, 0))
    hbm_spec = pl.BlockSpec(memory_space=pl.ANY)
    later = [w_out[0], ffn2_w_gate[0], ffn2_w_up[0], ffn2_w_down[0]]
    later_specs = [_slab_spec(w.shape, n_tiles) for w in later]
    x1, proj, la, w_out_b, wg2_b, wu2_b, wd2_b = pl.pallas_call(
        _ffn1_proj_kernel,
        grid=(n_tiles,),
        in_specs=[row_spec(d), pos_spec, pos_spec, _const_spec((1, d)),
                  hbm_spec, hbm_spec, hbm_spec, _const_spec((1, d)), hbm_spec,
                  _const_spec((LANE, QK_W)), _const_spec((1, QK_W))] + later_specs,
        out_specs=[row_spec(d), row_spec(PROJ_W), row_spec(QK_W)] + later_specs,
        out_shape=[jax.ShapeDtypeStruct((n_rows, d), F32),
                   jax.ShapeDtypeStruct((n_rows, PROJ_W), BF16),
                   jax.ShapeDtypeStruct((n_rows, QK_W), F32)]
                  + [jax.ShapeDtypeStruct(w.shape, BF16) for w in later],
        scratch_shapes=[pltpu.VMEM((tile, D_FF), BF16),
                        pltpu.VMEM((d, D_FF), BF16), pltpu.VMEM((d, D_FF), BF16), pltpu.VMEM((D_FF, d), BF16),
                        pltpu.VMEM((PROJ_W, d), BF16), pltpu.VMEM((LANE, d), BF16),
                        pltpu.VMEM((WEIGHT_STAGE_SLOTS, WEIGHT_CHUNK_ROWS, D_FF), F32),
                        pltpu.SemaphoreType.DMA((WEIGHT_STAGE_SLOTS,))],
        compiler_params=pltpu.CompilerParams(dimension_semantics=("arbitrary",),
                                             vmem_limit_bytes=VMEM_LIMIT),
        name="ffn1_proj",
    )(x2d, cos, sin, row(ffn1_norm_g), ffn1_w_gate[0], ffn1_w_up[0], ffn1_w_down[0], row(mix_norm_g),
      w_in_t, w_a2, row(gla_b_a), *later)

    ltri2, lmask = _gla_tables()
    dret, gq, gk, aret = _retention_tables()
    consts = [jnp.asarray(_key_mask(), BF16), jnp.asarray(dret), jnp.asarray(gq),
              jnp.asarray(gk), jnp.asarray(aret), jnp.asarray(ltri2, BF16), jnp.asarray(lmask)]

    tile = MIXER_TILE_ROWS
    assert seq % tile == 0
    n_tiles = n_rows // tile
    tok_spec = lambda w: pl.BlockSpec((tile, w), lambda i: (jnp.minimum(i, n_tiles - 1), 0))
    out = pl.pallas_call(
        functools.partial(_mixer_ffn2_kernel, steps_per_seq=seq // tile),
        grid=(n_tiles + 1,),
        in_specs=[tok_spec(d), tok_spec(PROJ_W), tok_spec(QK_W)]
                 + [_const_spec(c.shape) for c in consts]
                 + [_const_spec((1, V_W)), _const_spec((1, V_W)), _const_spec((d, d)), _const_spec((1, d)),
                    _const_spec((d, D_FF)), _const_spec((d, D_FF)), _const_spec((D_FF, d)),
                    _const_spec((1, d))],
        out_specs=pl.BlockSpec((tile, d), lambda i: (jnp.maximum(i - 1, 0), 0)),
        out_shape=jax.ShapeDtypeStruct((n_rows, d), F32),
        scratch_shapes=[pltpu.VMEM((tile, 2 * V_W), BF16), pltpu.VMEM((tile, D_FF), BF16),
                        pltpu.VMEM((tile, d), F32), pltpu.VMEM((tile, d), BF16),
                        pltpu.VMEM((QK_W, HEAD_V), F32), pltpu.VMEM((QK_W, HEAD_V), F32)],
        compiler_params=pltpu.CompilerParams(dimension_semantics=("arbitrary",),
                                             vmem_limit_bytes=VMEM_LIMIT),
        name="mixer_ffn2",
    )(x1, proj, la, *consts, row(ret_norm_g), row(gla_norm_g), w_out_b, row(ffn2_norm_g),
      wg2_b, wu2_b, wd2_b, row(final_norm_g))
    return out.reshape(bsz, seq, d)
```

```python
import functools

import numpy as np
import jax
import jax.numpy as jnp
from jax import lax
from jax.experimental import pallas as pl
from jax.experimental.pallas import tpu as pltpu

D_MODEL = 1024
CHUNK = 64
RMS_EPS = 1e-6
ROPE_BASE = 10000.0
N_HEADS = 4
QK_W = 256
V_W = 512
HEAD_V = V_W // N_HEADS
HEAD_QK = QK_W // N_HEADS
GATE_RANK = 16
GATE_NORM = 16.0
D_FF = 2816
PROJ_W = 2 * (2 * QK_W + 2 * V_W)
RQ, RK, RV, RG = 0, 256, 512, 1024
GQ, GK, GV, GG = 1536, 1792, 2048, 2560
GLOW = 3072

LANE = 128
SUBLANES = 8
BF16_SUBLANES = 16
MXU_N = 256
FFN_STREAM_ROWS = 512
MIXER_TILE_ROWS = 512
WEIGHT_CHUNK_ROWS = 128
WEIGHT_STAGE_SLOTS = 8
VMEM_LIMIT = 60 * 1024 * 1024

GLA_LEVELS = (64, 32, 16, 8, 4, 2)
MIXER_GROUP_CHUNKS = 4
MIXER_STAGES_PER_ROUND = (3, 2) * 5 + (2,)
LOG2_E = 1.4426950408889634

BF16 = jnp.bfloat16
F32 = jnp.float32


def _dot(a, b):
    return jnp.dot(a, b, preferred_element_type=F32)


def _dot_nt(a, b):
    return lax.dot_general(a, b, (((1,), (1,)), ((), ())), preferred_element_type=F32)


def _rms(x, g):
    ms = jnp.mean(x * x, axis=-1, keepdims=True)
    return x * lax.rsqrt(ms + RMS_EPS) * g


def _silu(x):
    h = 0.5 * x
    return h * jnp.tanh(h) + h


def _load_weight_as_bf16(src_hbm, dst_ref, stage_ref, sem):
    n_rows, n_cols = dst_ref.shape
    assert n_rows % WEIGHT_CHUNK_ROWS == 0 and n_cols <= stage_ref.shape[2]
    n_chunks = n_rows // WEIGHT_CHUNK_ROWS
    n_slots = stage_ref.shape[0]

    def chunk_copy(k):
        slot = k % n_slots
        return pltpu.make_async_copy(
            src_hbm.at[pl.ds(k * WEIGHT_CHUNK_ROWS, WEIGHT_CHUNK_ROWS), :],
            stage_ref.at[slot, :, pl.ds(0, n_cols)], sem.at[slot])

    for k in range(min(n_slots - 1, n_chunks)):
        chunk_copy(k).start()

    def body(k, carry):
        @pl.when(k + n_slots - 1 < n_chunks)
        def _():
            chunk_copy(k + n_slots - 1).start()

        chunk_copy(k).wait()
        rows = pl.ds(pl.multiple_of(k * WEIGHT_CHUNK_ROWS, WEIGHT_CHUNK_ROWS), WEIGHT_CHUNK_ROWS)
        dst_ref[rows, :] = stage_ref[k % n_slots, :, pl.ds(0, n_cols)].astype(BF16)
        return carry

    lax.fori_loop(0, n_chunks, body, 0)


def _ffn1_proj_kernel(x_ref, cos_ref, sin_ref, g1_ref, wg_hbm, wu_hbm, wd_hbm, gm_ref, win_hbm,
                      wa2_ref, ba_ref, *rest):
    later_f32, (x1_ref, proj_ref, la_ref), later_bf16 = rest[:4], rest[4:7], rest[7:11]
    act_ref, wg_ref, wu_ref, wd_ref, win_ref, wlow_ref, stage_ref, sem = rest[11:]

    @pl.when(pl.program_id(0) == 0)
    def _():
        for src, dst in ((wg_hbm, wg_ref), (wu_hbm, wu_ref), (wd_hbm, wd_ref), (win_hbm, win_ref)):
            _load_weight_as_bf16(src, dst, stage_ref, sem)
        n_cols = wlow_ref.shape[1]
        tail = pltpu.make_async_copy(win_hbm.at[pl.ds(GLOW, GATE_RANK), :],
                                     stage_ref.at[0, pl.ds(0, GATE_RANK), pl.ds(0, n_cols)], sem.at[0])
        tail.start()
        wlow_ref[...] = jnp.zeros_like(wlow_ref)
        tail.wait()
        wlow_ref[0:GATE_RANK, :] = stage_ref[0, 0:GATE_RANK, 0:n_cols].astype(BF16)

    def cast_later_weights():
        yield
        for src, dst in zip(later_f32, later_bf16):
            dst[...] = src[...].astype(BF16)

    def half_tile(rows):
        x = x_ref[rows, :]
        h = _rms(x, g1_ref[...]).astype(BF16)
        yield
        for c in range(D_FF // MXU_N):
            cols = slice(c * MXU_N, (c + 1) * MXU_N)
            gate = _dot(h, wg_ref[:, cols])
            up = _dot(h, wu_ref[:, cols])
            act_ref[rows, cols] = (_silu(gate) * up).astype(BF16)
        yield
        x1 = x + 0.5 * _dot(act_ref[rows, :], wd_ref[...])
        x1_ref[rows, :] = x1
        h = _rms(x1, gm_ref[...]).astype(BF16)
        yield
        cos = cos_ref[rows, :]
        sin = sin_ref[rows, :]
        first_half = (lax.broadcasted_iota(jnp.int32, cos.shape, 1) % HEAD_QK) < HEAD_QK // 2
        for off, scale in ((RQ, 1.0), (RK, HEAD_QK ** -0.5)):
            t = _dot_nt(h, win_ref[off:off + QK_W, :])
            for col in range(QK_W // LANE):
                tc = t[:, col * LANE:(col + 1) * LANE]
                partner = jnp.where(first_half, pltpu.roll(tc, LANE - HEAD_QK // 2, 1),
                                    pltpu.roll(tc, HEAD_QK // 2, 1))
                proj_ref[rows, off + col * LANE:off + (col + 1) * LANE] = (
                    (tc * cos + partner * sin) * scale).astype(BF16)
        t = _dot_nt(h, win_ref[RV:GQ, :])
        proj_ref[rows, RV:RG] = t[:, :RG - RV].astype(BF16)
        proj_ref[rows, RG:GQ] = _silu(t[:, RG - RV:]).astype(BF16)
        proj_ref[rows, GQ:GK] = (_dot_nt(h, win_ref[GQ:GK, :]) * (HEAD_QK ** -0.5)).astype(BF16)
        t = _dot_nt(h, win_ref[GK:PROJ_W, :])
        proj_ref[rows, GK:GG] = t[:, :GG - GK].astype(BF16)
        proj_ref[rows, GG:PROJ_W] = _silu(t[:, GG - GK:]).astype(BF16)
        low = _dot_nt(h, wlow_ref[...]).astype(BF16)
        yield
        z = _dot(low, wa2_ref[...]) + ba_ref[...]
        la_ref[rows, :] = (jnp.minimum(z, 0.0) - jnp.log(1.0 + jnp.exp(-jnp.abs(z)))) * (1.0 / GATE_NORM)

    n_half = x_ref.shape[0] // 2
    streams = [half_tile(pl.ds(i * n_half, n_half)) for i in range(2)] + [cast_later_weights()]
    while streams:
        streams = [g for g in streams if next(g, StopIteration) is not StopIteration]


def _head_norm_gate(o, norm_g, gate):
    outs = []
    for h in range(N_HEADS):
        cols = slice(h * HEAD_V, (h + 1) * HEAD_V)
        oh = o[:, cols]
        ms = jnp.mean(oh * oh, axis=-1, keepdims=True)
        outs.append(oh * lax.rsqrt(ms + RMS_EPS) * norm_g[:, cols] * gate[:, cols])
    return jnp.concatenate(outs, axis=-1)


def _midpoint_rows(b, s):
    rows, w = b.shape
    if s >= SUBLANES:
        blocks = b.reshape(rows // s, s, w)
        return jnp.broadcast_to(blocks[:, s // 2 - 1:s // 2, :], blocks.shape).reshape(rows, w)
    groups = b.reshape(rows // SUBLANES, SUBLANES, w)
    sub = lax.broadcasted_iota(jnp.int32, groups.shape, 1)
    out = None
    for first in reversed(range(0, SUBLANES, s)):
        ref_row = jnp.broadcast_to(groups[:, first + s // 2 - 1:first + s // 2, :], groups.shape)
        out = ref_row if out is None else jnp.where(sub < first + s, ref_row, out)
    return out.reshape(rows, w)


def _mixer_ffn2_kernel(x1_ref, proj_ref, la_ref,
                       kmask_ref, dret_ref, gq_ref, gk_ref, aret_ref, ltri_ref, lmask_ref,
                       rn_ref, gn_ref, wout_ref, g2_ref, wg_ref, wu_ref, wd_ref, gf_ref,
                       out_ref, o_ref, act_ref, x2_ref, h2_ref, sr_ref, sg_ref, *, steps_per_seq):
    rows_per_step = x1_ref.shape[0]
    step = pl.program_id(0)

    @pl.when(step == 0)
    def _():
        x2_ref[...] = jnp.zeros_like(x2_ref)
        h2_ref[...] = jnp.zeros_like(h2_ref)

    @pl.when(step % steps_per_seq == 0)
    def _():
        sr_ref[...] = jnp.zeros_like(sr_ref)
        sg_ref[...] = jnp.zeros_like(sg_ref)

    def scores(ql, kl):
        out = []
        for col in range(QK_W // LANE):
            lanes = slice(col * LANE, (col + 1) * LANE)
            w = jnp.concatenate([kl[:, lanes]] * 2, axis=0).T * kmask_ref[...]
            out.append(_dot(ql[:, lanes], w))
        return jnp.concatenate(out, axis=1)

    def attend(s, q_in, v, state):
        s = s.astype(BF16)
        zero = jnp.zeros((CHUNK, HEAD_V), BF16)
        out = []
        for col in range(QK_W // LANE):
            lanes = slice(col * LANE, (col + 1) * LANE)
            rhs = []
            for h in (2 * col, 2 * col + 1):
                vh = v[:, h * HEAD_V:(h + 1) * HEAD_V]
                rhs.append([vh, zero] if h % 2 == 0 else [zero, vh])
            for h in (2 * col, 2 * col + 1):
                sh = state[h * HEAD_QK:(h + 1) * HEAD_QK, :].astype(BF16)
                rhs.append([sh, zero] if h % 2 == 0 else [zero, sh])
            rhs = jnp.concatenate([jnp.concatenate(r, axis=1) for r in rhs], axis=0)
            out.append(_dot(jnp.concatenate([s[:, lanes], q_in[:, lanes]], axis=1), rhs))
        return jnp.concatenate(out, axis=1)

    def state_update(k_out, v):
        k_t = k_out.T
        return jnp.concatenate([_dot(k_t[h * HEAD_QK:(h + 1) * HEAD_QK, :], v[:, h * HEAD_V:(h + 1) * HEAD_V])
                                for h in range(N_HEADS)], axis=0)

    def mixer_stages():
        state = [sr_ref[...], sg_ref[...]]
        for first in range(0, rows_per_step // CHUNK, MIXER_GROUP_CHUNKS):
            yield from chunk_group_stages([pl.ds((first + c) * CHUNK, CHUNK)
                                           for c in range(MIXER_GROUP_CHUNKS)], state)
        sr_ref[...], sg_ref[...] = state

    def chunk_group_stages(chunk_rows, state):
        q_r = [proj_ref[r, RQ:RQ + QK_W] for r in chunk_rows]
        k_r = [proj_ref[r, RK:RK + QK_W] for r in chunk_rows]
        v_r = [proj_ref[r, RV:RV + V_W] for r in chunk_rows]
        s_r = [scores(q, k) * dret_ref[...] for q, k in zip(q_r, k_r)]
        qin_r = [(q.astype(F32) * gq_ref[...]).astype(BF16) for q in q_r]
        yield
        kv_r = [state_update((k.astype(F32) * gk_ref[...]).astype(BF16), v) for k, v in zip(k_r, v_r)]
        yield

        b = []
        for r in chunk_rows:
            la = la_ref[r, :]
            la_hi = la.astype(BF16)
            la_lo = (la - la_hi.astype(F32)).astype(BF16)
            b.append(_dot(ltri_ref[...], jnp.concatenate([la_hi, la_lo], axis=0)) * LOG2_E)
        b_last = [jnp.broadcast_to(bc[CHUNK - 1:CHUNK, :], (2 * CHUNK, QK_W)) for bc in b]
        a_col = [jnp.exp2(bl.T)[:, :LANE] for bl in b_last]
        yield
        q_g = [proj_ref[r, GQ:GQ + QK_W] for r in chunk_rows]
        k_g = [proj_ref[r, GK:GK + QK_W] for r in chunk_rows]
        v_g = [proj_ref[r, GV:GV + V_W] for r in chunk_rows]
        s_g = [scores(q, k) * lmask_ref[0] for q, k in zip(q_g, k_g)]
        for l, s in enumerate(GLA_LEVELS):
            for c in range(len(chunk_rows)):
                if c % 4 == 0:
                    yield
                f = jnp.exp2(-jnp.abs(b[c] - _midpoint_rows(b[c], s))).astype(BF16)
                s_g[c] = s_g[c] + scores(q_g[c] * f, k_g[c] * f) * lmask_ref[1 + l]
        yield
        qin_g = [q * jnp.exp2(bc).astype(BF16) for q, bc in zip(q_g, b)]
        kv_g = [state_update(k * jnp.exp2(bl[:CHUNK, :] - bc).astype(BF16), v)
                for k, bc, bl, v in zip(k_g, b, b_last, v_g)]

        sr, sg = state
        for c, r in enumerate(chunk_rows):
            if c % 2 == 0:
                yield
            o = attend(s_r[c], qin_r[c], v_r[c], sr)
            sr = sr * aret_ref[...] + kv_r[c]
            gate = proj_ref[r, RG:RG + V_W].astype(F32)
            o_ref[r, 0:V_W] = _head_norm_gate(o, rn_ref[...], gate).astype(BF16)
            o = attend(s_g[c], qin_g[c], v_g[c], sg)
            sg = sg * a_col[c] + kv_g[c]
            gate = proj_ref[r, GG:GG + V_W].astype(F32)
            o_ref[r, V_W:2 * V_W] = _head_norm_gate(o, gn_ref[...], gate).astype(BF16)
        state[:] = [sr, sg]

    def finish_previous(rows):
        h = h2_ref[rows, :]
        for c in range(D_FF // MXU_N):
            cols = slice(c * MXU_N, (c + 1) * MXU_N)
            gate = _dot(h, wg_ref[:, cols])
            up = _dot(h, wu_ref[:, cols])
            act_ref[rows, cols] = (_silu(gate) * up).astype(BF16)
            yield
        y = _dot(act_ref[rows, :], wd_ref[...])
        out_ref[rows, :] = _rms(x2_ref[rows, :] + 0.5 * y, gf_ref[...])

    n_rows = FFN_STREAM_ROWS
    mixer = mixer_stages()
    ffn_streams = [finish_previous(pl.ds(i * n_rows, n_rows)) for i in range(rows_per_step // n_rows)]
    for n_mixer in MIXER_STAGES_PER_ROUND:
        for g in ffn_streams:
            next(g, None)
        for _ in range(n_mixer):
            next(mixer, None)
    for g in ffn_streams + [mixer]:
        for _ in g:
            pass
    for i in range(rows_per_step // n_rows):
        rows = pl.ds(i * n_rows, n_rows)
        x2 = x1_ref[rows, :] + _dot(o_ref[rows, :], wout_ref[...])
        x2_ref[rows, :] = x2
        h2_ref[rows, :] = _rms(x2, g2_ref[...]).astype(BF16)


def _gla_tables():
    t = np.arange(CHUNK)[:, None]
    u = np.arange(CHUNK)[None, :]
    masks = [np.eye(CHUNK, dtype=bool)]
    for s in GLA_LEVELS:
        same_block = (t // s) == (u // s)
        masks.append(same_block & (((t % s) < s // 2) != ((u % s) < s // 2)))
    ltri = (u <= t).astype(np.float32)
    ltri2 = np.concatenate([ltri, ltri], axis=1)
    lmask = np.stack([np.tile(m, (1, N_HEADS)) for m in masks]).astype(np.float32)
    return ltri2, lmask


def _retention_tables():
    gamma = 1.0 - 2.0 ** (-5.0 - np.arange(N_HEADS, dtype=np.float64))
    head_of_lane = np.arange(QK_W) // HEAD_QK
    i = np.arange(CHUNK)
    dist = np.abs(i[:, None] - i[None, :])
    dret = np.concatenate([gamma[h] ** dist for h in range(N_HEADS)], axis=1)
    gq = gamma[head_of_lane][None, :] ** (i[:, None] + 1.0)
    gk = gamma[head_of_lane][None, :] ** (CHUNK - 1.0 - i[:, None])
    aret = np.broadcast_to((gamma[head_of_lane] ** CHUNK)[:, None], (QK_W, HEAD_V))
    return tuple(np.asarray(a, np.float32) for a in (dret, gq, gk, aret))


def _rotary_tables(seq):
    half = HEAD_QK // 2
    inv = ROPE_BASE ** (-np.arange(half, dtype=np.float64) * 2.0 / HEAD_QK)
    ang = np.arange(seq, dtype=np.float64)[:, None] * inv[None, :]
    sign = np.where((np.arange(LANE) % HEAD_QK) < half, -1.0, 1.0)
    cos = np.tile(np.cos(ang), (1, LANE // half))
    sin = np.tile(np.sin(ang), (1, LANE // half)) * sign[None, :]
    return cos.astype(np.float32), sin.astype(np.float32)


def _key_mask():
    idx = np.arange(LANE) // HEAD_QK
    return idx[:, None] == idx[None, :]


def _const_spec(shape):
    nd = len(shape)
    return pl.BlockSpec(shape, lambda *_: (0,) * nd, pipeline_mode=pl.Buffered(1))


def _slab_spec(shape, n_steps):
    rows, cols = shape
    slab = next(s for s in range(BF16_SUBLANES, rows + 1, BF16_SUBLANES)
                if rows % s == 0 and n_steps % (rows // s) == 0 and rows // s <= n_steps)
    repeat = n_steps // (rows // slab)
    return pl.BlockSpec((slab, cols), lambda i: (i // repeat, 0))


def _tile_rows(total_rows):
    tile = 512
    assert total_rows % tile == 0
    return tile


@jax.jit
def kernel(x, ffn1_norm_g, ffn1_w_gate, ffn1_w_up, ffn1_w_down, mix_norm_g, w_in, ret_norm_g, gla_w_a2,
           gla_b_a, gla_norm_g, w_out, ffn2_norm_g, ffn2_w_gate, ffn2_w_up, ffn2_w_down, final_norm_g):
    bsz, seq, d = x.shape
    assert d == D_MODEL and seq % CHUNK == 0 and ffn1_norm_g.shape[0] == 1
    tile = _tile_rows(seq)
    steps = seq // tile
    n_rows = bsz * seq

    row = lambda g: g.reshape(1, -1).astype(F32)
    w_in_t = jnp.swapaxes(w_in, 1, 2)[0]
    w_a2 = jnp.pad(gla_w_a2[0], ((0, LANE - GATE_RANK), (0, 0))).astype(BF16)

    cos, sin = _rotary_tables(seq)

    x2d = x.reshape(n_rows, d)
    n_tiles = n_rows // tile
    row_spec = lambda w: pl.BlockSpec((tile, w), lambda i: (i, 0))
    pos_spec = pl.BlockSpec((tile, LANE), lambda i: (i % steps, 0))
    hbm_spec = pl.BlockSpec(memory_space=pl.ANY)
    later = [w_out[0], ffn2_w_gate[0], ffn2_w_up[0], ffn2_w_down[0]]
    later_specs = [_slab_spec(w.shape, n_tiles) for w in later]
    x1, proj, la, w_out_b, wg2_b, wu2_b, wd2_b = pl.pallas_call(
        _ffn1_proj_kernel,
        grid=(n_tiles,),
        in_specs=[row_spec(d), pos_spec, pos_spec, _const_spec((1, d)),
                  hbm_spec, hbm_spec, hbm_spec, _const_spec((1, d)), hbm_spec,
                  _const_spec((LANE, QK_W)), _const_spec((1, QK_W))] + later_specs,
        out_specs=[row_spec(d), row_spec(PROJ_W), row_spec(QK_W)] + later_specs,
        out_shape=[jax.ShapeDtypeStruct((n_rows, d), F32),
                   jax.ShapeDtypeStruct((n_rows, PROJ_W), BF16),
                   jax.ShapeDtypeStruct((n_rows, QK_W), F32)]
                  + [jax.ShapeDtypeStruct(w.shape, BF16) for w in later],
        scratch_shapes=[pltpu.VMEM((tile, D_FF), BF16),
                        pltpu.VMEM((d, D_FF), BF16), pltpu.VMEM((d, D_FF), BF16), pltpu.VMEM((D_FF, d), BF16),
                        pltpu.VMEM((PROJ_W, d), BF16), pltpu.VMEM((LANE, d), BF16),
                        pltpu.VMEM((WEIGHT_STAGE_SLOTS, WEIGHT_CHUNK_ROWS, D_FF), F32),
                        pltpu.SemaphoreType.DMA((WEIGHT_STAGE_SLOTS,))],
        compiler_params=pltpu.CompilerParams(dimension_semantics=("arbitrary",),
                                             vmem_limit_bytes=VMEM_LIMIT),
        name="ffn1_proj",
    )(x2d, cos, sin, row(ffn1_norm_g), ffn1_w_gate[0], ffn1_w_up[0], ffn1_w_down[0], row(mix_norm_g),
      w_in_t, w_a2, row(gla_b_a), *later)

    ltri2, lmask = _gla_tables()
    dret, gq, gk, aret = _retention_tables()
    consts = [jnp.asarray(_key_mask(), BF16), jnp.asarray(dret), jnp.asarray(gq),
              jnp.asarray(gk), jnp.asarray(aret), jnp.asarray(ltri2, BF16), jnp.asarray(lmask)]

    tile = MIXER_TILE_ROWS
    assert seq % tile == 0
    n_tiles = n_rows // tile
    tok_spec = lambda w: pl.BlockSpec((tile, w), lambda i: (jnp.minimum(i, n_tiles - 1), 0))
    out = pl.pallas_call(
        functools.partial(_mixer_ffn2_kernel, steps_per_seq=seq // tile),
        grid=(n_tiles + 1,),
        in_specs=[tok_spec(d), tok_spec(PROJ_W), tok_spec(QK_W)]
                 + [_const_spec(c.shape) for c in consts]
                 + [_const_spec((1, V_W)), _const_spec((1, V_W)), _const_spec((d, d)), _const_spec((1, d)),
                    _const_spec((d, D_FF)), _const_spec((d, D_FF)), _const_spec((D_FF, d)),
                    _const_spec((1, d))],
        out_specs=pl.BlockSpec((tile, d), lambda i: (jnp.maximum(i - 1, 0), 0)),
        out_shape=jax.ShapeDtypeStruct((n_rows, d), F32),
        scratch_shapes=[pltpu.VMEM((tile, 2 * V_W), BF16), pltpu.VMEM((tile, D_FF), BF16),
                        pltpu.VMEM((tile, d), F32), pltpu.VMEM((tile, d), BF16),
                        pltpu.VMEM((QK_W, HEAD_V), F32), pltpu.VMEM((QK_W, HEAD_V), F32)],
        compiler_params=pltpu.CompilerParams(dimension_semantics=("arbitrary",),
                                             vmem_limit_bytes=VMEM_LIMIT),
        name="mixer_ffn2",
    )(x1, proj, la, *consts, row(ret_norm_g), row(gla_norm_g), w_out_b, row(ffn2_norm_g),
      wg2_b, wu2_b, wd2_b, row(final_norm_g))
    return out.reshape(bsz, seq, d)
```

```python
import functools

import numpy as np
import jax
import jax.numpy as jnp
from jax import lax
from jax.experimental import pallas as pl
from jax.experimental.pallas import tpu as pltpu

D_MODEL = 1024
CHUNK = 64
RMS_EPS = 1e-6
ROPE_BASE = 10000.0
N_HEADS = 4
QK_W = 256
V_W = 512
HEAD_V = V_W // N_HEADS
HEAD_QK = QK_W // N_HEADS
GATE_RANK = 16
GATE_NORM = 16.0
D_FF = 2816
PROJ_W = 2 * (2 * QK_W + 2 * V_W)
RQ, RK, RV, RG = 0, 256, 512, 1024
GQ, GK, GV, GG = 1536, 1792, 2048, 2560
GLOW = 3072

LANE = 128
SUBLANES = 8
BF16_SUBLANES = 16
MXU_N = 256
FFN_STREAM_ROWS = 512
MIXER_TILE_ROWS = 512
WEIGHT_CHUNK_BYTES = 3 * 512 * 1024
WEIGHT_STAGE_SLOTS = 6
VMEM_LIMIT = 56 * 1024 * 1024

GLA_LEVELS = (64, 32, 16, 8, 4, 2)
MIXER_GROUP_CHUNKS = 4
MIXER_STAGES_PER_ROUND = (3, 2) * 5 + (2,)
LOG2_E = 1.4426950408889634

BF16 = jnp.bfloat16
F32 = jnp.float32


def _dot(a, b):
    return jnp.dot(a, b, preferred_element_type=F32)


def _dot_nt(a, b):
    return lax.dot_general(a, b, (((1,), (1,)), ((), ())), preferred_element_type=F32)


def _rms(x, g):
    ms = jnp.mean(x * x, axis=-1, keepdims=True)
    return x * lax.rsqrt(ms + RMS_EPS) * g


def _silu(x):
    h = 0.5 * x
    return h * jnp.tanh(h) + h


def _weight_chunk_rows(n_rows, n_cols):
    best = max(r for r in range(BF16_SUBLANES, n_rows + 1, BF16_SUBLANES)
               if n_rows % r == 0 and r * n_cols * 4 <= WEIGHT_CHUNK_BYTES)
    return best


def _load_weight_as_bf16(src_hbm, dst_ref):
    n_rows, n_cols = dst_ref.shape
    chunk_rows = _weight_chunk_rows(n_rows, n_cols)
    n_chunks = n_rows // chunk_rows
    n_slots = WEIGHT_STAGE_SLOTS

    def stream(stage_ref, sem):
        def chunk_copy(k):
            slot = k % n_slots
            return pltpu.make_async_copy(src_hbm.at[pl.ds(k * chunk_rows, chunk_rows), :],
                                         stage_ref.at[slot], sem.at[slot])

        for k in range(min(n_slots - 1, n_chunks)):
            chunk_copy(k).start()

        def body(k, carry):
            @pl.when(k + n_slots - 1 < n_chunks)
            def _():
                chunk_copy(k + n_slots - 1).start()

            chunk_copy(k).wait()
            rows = pl.ds(pl.multiple_of(k * chunk_rows, chunk_rows), chunk_rows)
            dst_ref[rows, :] = stage_ref[k % n_slots].astype(BF16)
            return carry

        lax.fori_loop(0, n_chunks, body, 0)

    pl.run_scoped(stream, pltpu.VMEM((n_slots, chunk_rows, n_cols), F32),
                  pltpu.SemaphoreType.DMA((n_slots,)))


def _ffn1_proj_kernel(x_ref, cos_ref, sin_ref, g1_ref, wg_hbm, wu_hbm, wd_hbm, gm_ref, win_hbm,
                      wa2_ref, ba_ref, *rest):
    later_f32, (x1_ref, proj_ref, la_ref), later_bf16 = rest[:4], rest[4:7], rest[7:11]
    act_ref, wg_ref, wu_ref, wd_ref, win_ref, wlow_ref = rest[11:]

    @pl.when(pl.program_id(0) == 0)
    def _():
        for src, dst in ((wg_hbm, wg_ref), (wu_hbm, wu_ref), (wd_hbm, wd_ref), (win_hbm, win_ref)):
            _load_weight_as_bf16(src, dst)

        def load_low_rank(stage_ref, sem):
            tail = pltpu.make_async_copy(win_hbm.at[pl.ds(GLOW, GATE_RANK), :], stage_ref, sem.at[0])
            tail.start()
            wlow_ref[...] = jnp.zeros_like(wlow_ref)
            tail.wait()
            wlow_ref[0:GATE_RANK, :] = stage_ref[...].astype(BF16)

        pl.run_scoped(load_low_rank, pltpu.VMEM((GATE_RANK, wlow_ref.shape[1]), F32),
                      pltpu.SemaphoreType.DMA((1,)))

    def cast_later_weights():
        yield
        for src, dst in zip(later_f32, later_bf16):
            dst[...] = src[...].astype(BF16)

    def half_tile(rows):
        x = x_ref[rows, :]
        h = _rms(x, g1_ref[...]).astype(BF16)
        yield
        for c in range(D_FF // MXU_N):
            cols = slice(c * MXU_N, (c + 1) * MXU_N)
            gate = _dot(h, wg_ref[:, cols])
            up = _dot(h, wu_ref[:, cols])
            act_ref[rows, cols] = (_silu(gate) * up).astype(BF16)
        yield
        x1 = x + 0.5 * _dot(act_ref[rows, :], wd_ref[...])
        x1_ref[rows, :] = x1
        h = _rms(x1, gm_ref[...]).astype(BF16)
        yield
        cos = cos_ref[rows, :]
        sin = sin_ref[rows, :]
        first_half = (lax.broadcasted_iota(jnp.int32, cos.shape, 1) % HEAD_QK) < HEAD_QK // 2
        for off, scale in ((RQ, 1.0), (RK, HEAD_QK ** -0.5)):
            t = _dot_nt(h, win_ref[off:off + QK_W, :])
            for col in range(QK_W // LANE):
                tc = t[:, col * LANE:(col + 1) * LANE]
                partner = jnp.where(first_half, pltpu.roll(tc, LANE - HEAD_QK // 2, 1),
                                    pltpu.roll(tc, HEAD_QK // 2, 1))
                proj_ref[rows, off + col * LANE:off + (col + 1) * LANE] = (
                    (tc * cos + partner * sin) * scale).astype(BF16)
        t = _dot_nt(h, win_ref[RV:GQ, :])
        proj_ref[rows, RV:RG] = t[:, :RG - RV].astype(BF16)
        proj_ref[rows, RG:GQ] = _silu(t[:, RG - RV:]).astype(BF16)
        proj_ref[rows, GQ:GK] = (_dot_nt(h, win_ref[GQ:GK, :]) * (HEAD_QK ** -0.5)).astype(BF16)
        t = _dot_nt(h, win_ref[GK:PROJ_W, :])
        proj_ref[rows, GK:GG] = t[:, :GG - GK].astype(BF16)
        proj_ref[rows, GG:PROJ_W] = _silu(t[:, GG - GK:]).astype(BF16)
        low = _dot_nt(h, wlow_ref[...]).astype(BF16)
        yield
        z = _dot(low, wa2_ref[...]) + ba_ref[...]
        la_ref[rows, :] = (jnp.minimum(z, 0.0) - jnp.log(1.0 + jnp.exp(-jnp.abs(z)))) * (1.0 / GATE_NORM)

    n_half = x_ref.shape[0] // 2
    streams = [half_tile(pl.ds(i * n_half, n_half)) for i in range(2)] + [cast_later_weights()]
    while streams:
        streams = [g for g in streams if next(g, StopIteration) is not StopIteration]


def _head_norm_gate(o, norm_g, gate):
    outs = []
    for h in range(N_HEADS):
        cols = slice(h * HEAD_V, (h + 1) * HEAD_V)
        oh = o[:, cols]
        ms = jnp.mean(oh * oh, axis=-1, keepdims=True)
        outs.append(oh * lax.rsqrt(ms + RMS_EPS) * norm_g[:, cols] * gate[:, cols])
    return jnp.concatenate(outs, axis=-1)


def _midpoint_rows(b, s):
    rows, w = b.shape
    if s >= SUBLANES:
        blocks = b.reshape(rows // s, s, w)
        return jnp.broadcast_to(blocks[:, s // 2 - 1:s // 2, :], blocks.shape).reshape(rows, w)
    groups = b.reshape(rows // SUBLANES, SUBLANES, w)
    sub = lax.broadcasted_iota(jnp.int32, groups.shape, 1)
    out = None
    for first in reversed(range(0, SUBLANES, s)):
        ref_row = jnp.broadcast_to(groups[:, first + s // 2 - 1:first + s // 2, :], groups.shape)
        out = ref_row if out is None else jnp.where(sub < first + s, ref_row, out)
    return out.reshape(rows, w)


def _mixer_ffn2_kernel(x1_ref, proj_ref, la_ref,
                       kmask_ref, dret_ref, gq_ref, gk_ref, aret_ref, ltri_ref, lmask_ref,
                       rn_ref, gn_ref, wout_ref, g2_ref, wg_ref, wu_ref, wd_ref, gf_ref,
                       out_ref, o_ref, act_ref, x2_ref, h2_ref, sr_ref, sg_ref, *, steps_per_seq):
    rows_per_step = x1_ref.shape[0]
    step = pl.program_id(0)

    @pl.when(step == 0)
    def _():
        x2_ref[...] = jnp.zeros_like(x2_ref)
        h2_ref[...] = jnp.zeros_like(h2_ref)

    @pl.when(step % steps_per_seq == 0)
    def _():
        sr_ref[...] = jnp.zeros_like(sr_ref)
        sg_ref[...] = jnp.zeros_like(sg_ref)

    def scores(ql, kl):
        out = []
        for col in range(QK_W // LANE):
            lanes = slice(col * LANE, (col + 1) * LANE)
            w = jnp.concatenate([kl[:, lanes]] * 2, axis=0).T * kmask_ref[...]
            out.append(_dot(ql[:, lanes], w))
        return jnp.concatenate(out, axis=1)

    def attend(s, q_in, v, state):
        s = s.astype(BF16)
        zero = jnp.zeros((CHUNK, HEAD_V), BF16)
        out = []
        for col in range(QK_W // LANE):
            lanes = slice(col * LANE, (col + 1) * LANE)
            rhs = []
            for h in (2 * col, 2 * col + 1):
                vh = v[:, h * HEAD_V:(h + 1) * HEAD_V]
                rhs.append([vh, zero] if h % 2 == 0 else [zero, vh])
            for h in (2 * col, 2 * col + 1):
                sh = state[h * HEAD_QK:(h + 1) * HEAD_QK, :].astype(BF16)
                rhs.append([sh, zero] if h % 2 == 0 else [zero, sh])
            rhs = jnp.concatenate([jnp.concatenate(r, axis=1) for r in rhs], axis=0)
            out.append(_dot(jnp.concatenate([s[:, lanes], q_in[:, lanes]], axis=1), rhs))
        return jnp.concatenate(out, axis=1)

    def state_update(k_out, v):
        k_t = k_out.T
        return jnp.concatenate([_dot(k_t[h * HEAD_QK:(h + 1) * HEAD_QK, :], v[:, h * HEAD_V:(h + 1) * HEAD_V])
                                for h in range(N_HEADS)], axis=0)

    def mixer_stages():
        state = [sr_ref[...], sg_ref[...]]
        for first in range(0, rows_per_step // CHUNK, MIXER_GROUP_CHUNKS):
            yield from chunk_group_stages([pl.ds((first + c) * CHUNK, CHUNK)
                                           for c in range(MIXER_GROUP_CHUNKS)], state)
        sr_ref[...], sg_ref[...] = state

    def chunk_group_stages(chunk_rows, state):
        q_r = [proj_ref[r, RQ:RQ + QK_W] for r in chunk_rows]
        k_r = [proj_ref[r, RK:RK + QK_W] for r in chunk_rows]
        v_r = [proj_ref[r, RV:RV + V_W] for r in chunk_rows]
        s_r = [scores(q, k) * dret_ref[...] for q, k in zip(q_r, k_r)]
        qin_r = [(q.astype(F32) * gq_ref[...]).astype(BF16) for q in q_r]
        yield
        kv_r = [state_update((k.astype(F32) * gk_ref[...]).astype(BF16), v) for k, v in zip(k_r, v_r)]
        yield

        b = []
        for r in chunk_rows:
            la = la_ref[r, :]
            la_hi = la.astype(BF16)
            la_lo = (la - la_hi.astype(F32)).astype(BF16)
            b.append(_dot(ltri_ref[...], jnp.concatenate([la_hi, la_lo], axis=0)) * LOG2_E)
        b_last = [jnp.broadcast_to(bc[CHUNK - 1:CHUNK, :], (2 * CHUNK, QK_W)) for bc in b]
        a_col = [jnp.exp2(bl.T)[:, :LANE] for bl in b_last]
        yield
        q_g = [proj_ref[r, GQ:GQ + QK_W] for r in chunk_rows]
        k_g = [proj_ref[r, GK:GK + QK_W] for r in chunk_rows]
        v_g = [proj_ref[r, GV:GV + V_W] for r in chunk_rows]
        s_g = [scores(q, k) * lmask_ref[0] for q, k in zip(q_g, k_g)]
        for l, s in enumerate(GLA_LEVELS):
            for c in range(len(chunk_rows)):
                if c % 4 == 0:
                    yield
                f = jnp.exp2(-jnp.abs(b[c] - _midpoint_rows(b[c], s))).astype(BF16)
                s_g[c] = s_g[c] + scores(q_g[c] * f, k_g[c] * f) * lmask_ref[1 + l]
        yield
        qin_g = [q * jnp.exp2(bc).astype(BF16) for q, bc in zip(q_g, b)]
        kv_g = [state_update(k * jnp.exp2(bl[:CHUNK, :] - bc).astype(BF16), v)
                for k, bc, bl, v in zip(k_g, b, b_last, v_g)]

        sr, sg = state
        for c, r in enumerate(chunk_rows):
            if c % 2 == 0:
                yield
            o = attend(s_r[c], qin_r[c], v_r[c], sr)
            sr = sr * aret_ref[...] + kv_r[c]
            gate = proj_ref[r, RG:RG + V_W].astype(F32)
            o_ref[r, 0:V_W] = _head_norm_gate(o, rn_ref[...], gate).astype(BF16)
            o = attend(s_g[c], qin_g[c], v_g[c], sg)
            sg = sg * a_col[c] + kv_g[c]
            gate = proj_ref[r, GG:GG + V_W].astype(F32)
            o_ref[r, V_W:2 * V_W] = _head_norm_gate(o, gn_ref[...], gate).astype(BF16)
        state[:] = [sr, sg]

    def finish_previous(rows):
        h = h2_ref[rows, :]
        for c in range(D_FF // MXU_N):
            cols = slice(c * MXU_N, (c + 1) * MXU_N)
            gate = _dot(h, wg_ref[:, cols])
            up = _dot(h, wu_ref[:, cols])
            act_ref[rows, cols] = (_silu(gate) * up).astype(BF16)
            yield
        y = _dot(act_ref[rows, :], wd_ref[...])
        out_ref[rows, :] = _rms(x2_ref[rows, :] + 0.5 * y, gf_ref[...])

    n_rows = FFN_STREAM_ROWS
    mixer = mixer_stages()
    ffn_streams = [finish_previous(pl.ds(i * n_rows, n_rows)) for i in range(rows_per_step // n_rows)]
    for n_mixer in MIXER_STAGES_PER_ROUND:
        for g in ffn_streams:
            next(g, None)
        for _ in range(n_mixer):
            next(mixer, None)
    for g in ffn_streams + [mixer]:
        for _ in g:
            pass
    for i in range(rows_per_step // n_rows):
        rows = pl.ds(i * n_rows, n_rows)
        x2 = x1_ref[rows, :] + _dot(o_ref[rows, :], wout_ref[...])
        x2_ref[rows, :] = x2
        h2_ref[rows, :] = _rms(x2, g2_ref[...]).astype(BF16)


def _gla_tables():
    t = np.arange(CHUNK)[:, None]
    u = np.arange(CHUNK)[None, :]
    masks = [np.eye(CHUNK, dtype=bool)]
    for s in GLA_LEVELS:
        same_block = (t // s) == (u // s)
        masks.append(same_block & (((t % s) < s // 2) != ((u % s) < s // 2)))
    ltri = (u <= t).astype(np.float32)
    ltri2 = np.concatenate([ltri, ltri], axis=1)
    lmask = np.stack([np.tile(m, (1, N_HEADS)) for m in masks]).astype(np.float32)
    return ltri2, lmask


def _retention_tables():
    gamma = 1.0 - 2.0 ** (-5.0 - np.arange(N_HEADS, dtype=np.float64))
    head_of_lane = np.arange(QK_W) // HEAD_QK
    i = np.arange(CHUNK)
    dist = np.abs(i[:, None] - i[None, :])
    dret = np.concatenate([gamma[h] ** dist for h in range(N_HEADS)], axis=1)
    gq = gamma[head_of_lane][None, :] ** (i[:, None] + 1.0)
    gk = gamma[head_of_lane][None, :] ** (CHUNK - 1.0 - i[:, None])
    aret = np.broadcast_to((gamma[head_of_lane] ** CHUNK)[:, None], (QK_W, HEAD_V))
    return tuple(np.asarray(a, np.float32) for a in (dret, gq, gk, aret))


def _rotary_tables(seq):
    half = HEAD_QK // 2
    inv = ROPE_BASE ** (-np.arange(half, dtype=np.float64) * 2.0 / HEAD_QK)
    ang = np.arange(seq, dtype=np.float64)[:, None] * inv[None, :]
    sign = np.where((np.arange(LANE) % HEAD_QK) < half, -1.0, 1.0)
    cos = np.tile(np.cos(ang), (1, LANE // half))
    sin = np.tile(np.sin(ang), (1, LANE // half)) * sign[None, :]
    return cos.astype(np.float32), sin.astype(np.float32)


def _key_mask():
    idx = np.arange(LANE) // HEAD_QK
    return idx[:, None] == idx[None, :]


def _const_spec(shape):
    nd = len(shape)
    return pl.BlockSpec(shape, lambda *_: (0,) * nd, pipeline_mode=pl.Buffered(1))


def _slab_spec(shape, n_steps):
    rows, cols = shape
    slab = next(s for s in range(BF16_SUBLANES, rows + 1, BF16_SUBLANES)
                if rows % s == 0 and n_steps % (rows // s) == 0 and rows // s <= n_steps)
    repeat = n_steps // (rows // slab)
    return pl.BlockSpec((slab, cols), lambda i: (i // repeat, 0))


def _tile_rows(total_rows):
    tile = 512
    assert total_rows % tile == 0
    return tile


@jax.jit
def kernel(x, ffn1_norm_g, ffn1_w_gate, ffn1_w_up, ffn1_w_down, mix_norm_g, w_in, ret_norm_g, gla_w_a2,
           gla_b_a, gla_norm_g, w_out, ffn2_norm_g, ffn2_w_gate, ffn2_w_up, ffn2_w_down, final_norm_g):
    bsz, seq, d = x.shape
    assert d == D_MODEL and seq % CHUNK == 0 and ffn1_norm_g.shape[0] == 1
    tile = _tile_rows(seq)
    steps = seq // tile
    n_rows = bsz * seq

    row = lambda g: g.reshape(1, -1).astype(F32)
    w_in_t = jnp.swapaxes(w_in, 1, 2)[0]
    w_a2 = jnp.pad(gla_w_a2[0], ((0, LANE - GATE_RANK), (0, 0))).astype(BF16)

    cos, sin = _rotary_tables(seq)

    x2d = x.reshape(n_rows, d)
    n_tiles = n_rows // tile
    row_spec = lambda w: pl.BlockSpec((tile, w), lambda i: (i, 0))
    pos_spec = pl.BlockSpec((tile, LANE), lambda i: (i % steps, 0))
    hbm_spec = pl.BlockSpec(memory_space=pl.ANY)
    later = [w_out[0], ffn2_w_gate[0], ffn2_w_up[0], ffn2_w_down[0]]
    later_specs = [_slab_spec(w.shape, n_tiles) for w in later]
    x1, proj, la, w_out_b, wg2_b, wu2_b, wd2_b = pl.pallas_call(
        _ffn1_proj_kernel,
        grid=(n_tiles,),
        in_specs=[row_spec(d), pos_spec, pos_spec, _const_spec((1, d)),
                  hbm_spec, hbm_spec, hbm_spec, _const_spec((1, d)), hbm_spec,
                  _const_spec((LANE, QK_W)), _const_spec((1, QK_W))] + later_specs,
        out_specs=[row_spec(d), row_spec(PROJ_W), row_spec(QK_W)] + later_specs,
        out_shape=[jax.ShapeDtypeStruct((n_rows, d), F32),
                   jax.ShapeDtypeStruct((n_rows, PROJ_W), BF16),
                   jax.ShapeDtypeStruct((n_rows, QK_W), F32)]
                  + [jax.ShapeDtypeStruct(w.shape, BF16) for w in later],
        scratch_shapes=[pltpu.VMEM((tile, D_FF), BF16),
                        pltpu.VMEM((d, D_FF), BF16), pltpu.VMEM((d, D_FF), BF16), pltpu.VMEM((D_FF, d), BF16),
                        pltpu.VMEM((PROJ_W, d), BF16), pltpu.VMEM((LANE, d), BF16)],
        compiler_params=pltpu.CompilerParams(dimension_semantics=("arbitrary",),
                                             vmem_limit_bytes=VMEM_LIMIT),
        name="ffn1_proj",
    )(x2d, cos, sin, row(ffn1_norm_g), ffn1_w_gate[0], ffn1_w_up[0], ffn1_w_down[0], row(mix_norm_g),
      w_in_t, w_a2, row(gla_b_a), *later)

    ltri2, lmask = _gla_tables()
    dret, gq, gk, aret = _retention_tables()
    consts = [jnp.asarray(_key_mask(), BF16), jnp.asarray(dret), jnp.asarray(gq),
              jnp.asarray(gk), jnp.asarray(aret), jnp.asarray(ltri2, BF16), jnp.asarray(lmask)]

    tile = MIXER_TILE_ROWS
    assert seq % tile == 0
    n_tiles = n_rows // tile
    tok_spec = lambda w: pl.BlockSpec((tile, w), lambda i: (jnp.minimum(i, n_tiles - 1), 0))
    out = pl.pallas_call(
        functools.partial(_mixer_ffn2_kernel, steps_per_seq=seq // tile),
        grid=(n_tiles + 1,),
        in_specs=[tok_spec(d), tok_spec(PROJ_W), tok_spec(QK_W)]
                 + [_const_spec(c.shape) for c in consts]
                 + [_const_spec((1, V_W)), _const_spec((1, V_W)), _const_spec((d, d)), _const_spec((1, d)),
                    _const_spec((d, D_FF)), _const_spec((d, D_FF)), _const_spec((D_FF, d)),
                    _const_spec((1, d))],
        out_specs=pl.BlockSpec((tile, d), lambda i: (jnp.maximum(i - 1, 0), 0)),
        out_shape=jax.ShapeDtypeStruct((n_rows, d), F32),
        scratch_shapes=[pltpu.VMEM((tile, 2 * V_W), BF16), pltpu.VMEM((tile, D_FF), BF16),
                        pltpu.VMEM((tile, d), F32), pltpu.VMEM((tile, d), BF16),
                        pltpu.VMEM((QK_W, HEAD_V), F32), pltpu.VMEM((QK_W, HEAD_V), F32)],
        compiler_params=pltpu.CompilerParams(dimension_semantics=("arbitrary",),
                                             vmem_limit_bytes=VMEM_LIMIT),
        name="mixer_ffn2",
    )(x1, proj, la, *consts, row(ret_norm_g), row(gla_norm_g), w_out_b, row(ffn2_norm_g),
      wg2_b, wu2_b, wd2_b, row(final_norm_g))
    return out.reshape(bsz, seq, d)
```

```python
import functools

import numpy as np
import jax
import jax.numpy as jnp
from jax import lax
from jax.experimental import pallas as pl
from jax.experimental.pallas import tpu as pltpu

D_MODEL = 1024
CHUNK = 64
RMS_EPS = 1e-6
ROPE_BASE = 10000.0
N_HEADS = 4
QK_W = 256
V_W = 512
HEAD_V = V_W // N_HEADS
HEAD_QK = QK_W // N_HEADS
GATE_RANK = 16
GATE_NORM = 16.0
D_FF = 2816
PROJ_W = 2 * (2 * QK_W + 2 * V_W)
RQ, RK, RV, RG = 0, 256, 512, 1024
GQ, GK, GV, GG = 1536, 1792, 2048, 2560
GLOW = 3072

LANE = 128
SUBLANES = 8
BF16_SUBLANES = 16
MXU_N = 256
FFN_STREAM_ROWS = 512
MIXER_TILE_ROWS = 512
WEIGHT_CHUNK_BYTES = 3 * 512 * 1024
WEIGHT_STAGE_SLOTS = 6
VMEM_LIMIT = 56 * 1024 * 1024

GLA_LEVELS = (64, 32, 16, 8, 4, 2)
MIXER_GROUP_CHUNKS = 4
MIXER_STAGES_PER_ROUND = (3, 2) * 5 + (2,)
LOG2_E = 1.4426950408889634

BF16 = jnp.bfloat16
F32 = jnp.float32


def _dot(a, b):
    return jnp.dot(a, b, preferred_element_type=F32)


def _dot_nt(a, b):
    return lax.dot_general(a, b, (((1,), (1,)), ((), ())), preferred_element_type=F32)


def _rms(x, g):
    ms = jnp.mean(x * x, axis=-1, keepdims=True)
    return x * lax.rsqrt(ms + RMS_EPS) * g


def _silu(x):
    h = 0.5 * x
    return h * jnp.tanh(h) + h


def _weight_chunk_rows(n_rows, n_cols):
    best = max(r for r in range(BF16_SUBLANES, n_rows + 1, BF16_SUBLANES)
               if n_rows % r == 0 and r * n_cols * 4 <= WEIGHT_CHUNK_BYTES)
    return best


def _load_weight_as_bf16(src_hbm, dst_ref):
    n_rows, n_cols = dst_ref.shape
    chunk_rows = _weight_chunk_rows(n_rows, n_cols)
    n_chunks = n_rows // chunk_rows
    n_slots = WEIGHT_STAGE_SLOTS

    def stream(stage_ref, sem):
        def chunk_copy(k):
            slot = k % n_slots
            return pltpu.make_async_copy(src_hbm.at[pl.ds(k * chunk_rows, chunk_rows), :],
                                         stage_ref.at[slot], sem.at[slot])

        for k in range(min(n_slots - 1, n_chunks)):
            chunk_copy(k).start()

        def body(k, carry):
            @pl.when(k + n_slots - 1 < n_chunks)
            def _():
                chunk_copy(k + n_slots - 1).start()

            chunk_copy(k).wait()
            rows = pl.ds(pl.multiple_of(k * chunk_rows, chunk_rows), chunk_rows)
            dst_ref[rows, :] = stage_ref[k % n_slots].astype(BF16)
            return carry

        lax.fori_loop(0, n_chunks, body, 0)

    pl.run_scoped(stream, pltpu.VMEM((n_slots, chunk_rows, n_cols), F32),
                  pltpu.SemaphoreType.DMA((n_slots,)))


def _ffn1_proj_kernel(x_ref, cos_ref, sin_ref, g1_ref, wg_hbm, wu_hbm, wd_hbm, gm_ref, win_hbm,
                      wa2_ref, ba_ref, *rest):
    later_f32, (x1_ref, proj_ref, la_ref), later_bf16 = rest[:4], rest[4:7], rest[7:11]
    act_ref, wg_ref, wu_ref, wd_ref, win_ref, wlow_ref = rest[11:]

    @pl.when(pl.program_id(0) == 0)
    def _():
        for src, dst in ((wg_hbm, wg_ref), (wu_hbm, wu_ref), (wd_hbm, wd_ref), (win_hbm, win_ref)):
            _load_weight_as_bf16(src, dst)

        def load_low_rank(stage_ref, sem):
            tail = pltpu.make_async_copy(win_hbm.at[pl.ds(GLOW, GATE_RANK), :], stage_ref, sem.at[0])
            tail.start()
            wlow_ref[...] = jnp.zeros_like(wlow_ref)
            tail.wait()
            wlow_ref[0:GATE_RANK, :] = stage_ref[...].astype(BF16)

        pl.run_scoped(load_low_rank, pltpu.VMEM((GATE_RANK, wlow_ref.shape[1]), F32),
                      pltpu.SemaphoreType.DMA((1,)))

    def cast_later_weights():
        yield
        for src, dst in zip(later_f32, later_bf16):
            dst[...] = src[...].astype(BF16)

    def half_tile(rows):
        x = x_ref[rows, :]
        h = _rms(x, g1_ref[...]).astype(BF16)
        yield
        for c in range(D_FF // MXU_N):
            cols = slice(c * MXU_N, (c + 1) * MXU_N)
            gate = _dot(h, wg_ref[:, cols])
            up = _dot(h, wu_ref[:, cols])
            act_ref[rows, cols] = (_silu(gate) * up).astype(BF16)
        yield
        x1 = x + 0.5 * _dot(act_ref[rows, :], wd_ref[...])
        x1_ref[rows, :] = x1
        h = _rms(x1, gm_ref[...]).astype(BF16)
        yield
        cos = cos_ref[rows, :]
        sin = sin_ref[rows, :]
        first_half = (lax.broadcasted_iota(jnp.int32, cos.shape, 1) % HEAD_QK) < HEAD_QK // 2
        for off, scale in ((RQ, 1.0), (RK, HEAD_QK ** -0.5)):
            t = _dot_nt(h, win_ref[off:off + QK_W, :])
            for col in range(QK_W // LANE):
                tc = t[:, col * LANE:(col + 1) * LANE]
                partner = jnp.where(first_half, pltpu.roll(tc, LANE - HEAD_QK // 2, 1),
                                    pltpu.roll(tc, HEAD_QK // 2, 1))
                proj_ref[rows, off + col * LANE:off + (col + 1) * LANE] = (
                    (tc * cos + partner * sin) * scale).astype(BF16)
        t = _dot_nt(h, win_ref[RV:GQ, :])
        proj_ref[rows, RV:RG] = t[:, :RG - RV].astype(BF16)
        proj_ref[rows, RG:GQ] = _silu(t[:, RG - RV:]).astype(BF16)
        proj_ref[rows, GQ:GK] = (_dot_nt(h, win_ref[GQ:GK, :]) * (HEAD_QK ** -0.5)).astype(BF16)
        t = _dot_nt(h, win_ref[GK:PROJ_W, :])
        proj_ref[rows, GK:GG] = t[:, :GG - GK].astype(BF16)
        proj_ref[rows, GG:PROJ_W] = _silu(t[:, GG - GK:]).astype(BF16)
        low = _dot_nt(h, wlow_ref[...]).astype(BF16)
        yield
        z = _dot(low, wa2_ref[...]) + ba_ref[...]
        la_ref[rows, :] = (jnp.minimum(z, 0.0) - jnp.log(1.0 + jnp.exp(-jnp.abs(z)))) * (1.0 / GATE_NORM)

    n_half = x_ref.shape[0] // 2
    streams = [half_tile(pl.ds(i * n_half, n_half)) for i in range(2)] + [cast_later_weights()]
    while streams:
        streams = [g for g in streams if next(g, StopIteration) is not StopIteration]


def _head_norm_gate(o, norm_g, gate):
    outs = []
    for h in range(N_HEADS):
        cols = slice(h * HEAD_V, (h + 1) * HEAD_V)
        oh = o[:, cols]
        ms = jnp.mean(oh * oh, axis=-1, keepdims=True)
        outs.append(oh * lax.rsqrt(ms + RMS_EPS) * norm_g[:, cols] * gate[:, cols])
    return jnp.concatenate(outs, axis=-1)


def _midpoint_rows(b, s):
    rows, w = b.shape
    if s >= SUBLANES:
        blocks = b.reshape(rows // s, s, w)
        return jnp.broadcast_to(blocks[:, s // 2 - 1:s // 2, :], blocks.shape).reshape(rows, w)
    groups = b.reshape(rows // SUBLANES, SUBLANES, w)
    sub = lax.broadcasted_iota(jnp.int32, groups.shape, 1)
    out = None
    for first in reversed(range(0, SUBLANES, s)):
        ref_row = jnp.broadcast_to(groups[:, first + s // 2 - 1:first + s // 2, :], groups.shape)
        out = ref_row if out is None else jnp.where(sub < first + s, ref_row, out)
    return out.reshape(rows, w)


def _mixer_ffn2_kernel(x1_ref, proj_ref, la_ref,
                       kmask_ref, dret_ref, gq_ref, gk_ref, aret_ref, ltri_ref, lmask_ref,
                       rn_ref, gn_ref, wout_ref, g2_ref, wg_ref, wu_ref, wd_ref, gf_ref,
                       out_ref, o_ref, act_ref, x2_ref, h2_ref, sr_ref, sg_ref, *, steps_per_seq):
    rows_per_step = x1_ref.shape[0]
    step = pl.program_id(0)

    @pl.when(step == 0)
    def _():
        x2_ref[...] = jnp.zeros_like(x2_ref)
        h2_ref[...] = jnp.zeros_like(h2_ref)

    @pl.when(step % steps_per_seq == 0)
    def _():
        sr_ref[...] = jnp.zeros_like(sr_ref)
        sg_ref[...] = jnp.zeros_like(sg_ref)

    def scores(ql, kl):
        out = []
        for col in range(QK_W // LANE):
            lanes = slice(col * LANE, (col + 1) * LANE)
            w = jnp.concatenate([kl[:, lanes]] * 2, axis=0).T * kmask_ref[...]
            out.append(_dot(ql[:, lanes], w))
        return jnp.concatenate(out, axis=1)

    def attend(s, q_in, v, state):
        s = s.astype(BF16)
        zero = jnp.zeros((CHUNK, HEAD_V), BF16)
        out = []
        for col in range(QK_W // LANE):
            lanes = slice(col * LANE, (col + 1) * LANE)
            rhs = []
            for h in (2 * col, 2 * col + 1):
                vh = v[:, h * HEAD_V:(h + 1) * HEAD_V]
                rhs.append([vh, zero] if h % 2 == 0 else [zero, vh])
            for h in (2 * col, 2 * col + 1):
                sh = state[h * HEAD_QK:(h + 1) * HEAD_QK, :].astype(BF16)
                rhs.append([sh, zero] if h % 2 == 0 else [zero, sh])
            rhs = jnp.concatenate([jnp.concatenate(r, axis=1) for r in rhs], axis=0)
            out.append(_dot(jnp.concatenate([s[:, lanes], q_in[:, lanes]], axis=1), rhs))
        return jnp.concatenate(out, axis=1)

    def state_update(k_out, v):
        k_t = k_out.T
        return jnp.concatenate([_dot(k_t[h * HEAD_QK:(h + 1) * HEAD_QK, :], v[:, h * HEAD_V:(h + 1) * HEAD_V])
                                for h in range(N_HEADS)], axis=0)

    def mixer_stages():
        state = [sr_ref[...], sg_ref[...]]
        for first in range(0, rows_per_step // CHUNK, MIXER_GROUP_CHUNKS):
            yield from chunk_group_stages([pl.ds((first + c) * CHUNK, CHUNK)
                                           for c in range(MIXER_GROUP_CHUNKS)], state)
        sr_ref[...], sg_ref[...] = state

    def chunk_group_stages(chunk_rows, state):
        q_r = [proj_ref[r, RQ:RQ + QK_W] for r in chunk_rows]
        k_r = [proj_ref[r, RK:RK + QK_W] for r in chunk_rows]
        v_r = [proj_ref[r, RV:RV + V_W] for r in chunk_rows]
        s_r = [scores(q, k) * dret_ref[...] for q, k in zip(q_r, k_r)]
        gq, gk = gq_ref[...].astype(BF16), gk_ref[...].astype(BF16)
        qin_r = [q * gq for q in q_r]
        yield
        kv_r = [state_update(k * gk, v) for k, v in zip(k_r, v_r)]
        yield

        b = []
        for r in chunk_rows:
            la = la_ref[r, :]
            la_hi = la.astype(BF16)
            la_lo = (la - la_hi.astype(F32)).astype(BF16)
            b.append(_dot(ltri_ref[...], jnp.concatenate([la_hi, la_lo], axis=0)) * LOG2_E)
        b_last = [jnp.broadcast_to(bc[CHUNK - 1:CHUNK, :], (2 * CHUNK, QK_W)) for bc in b]
        a_col = [jnp.exp2(bl.T)[:, :LANE] for bl in b_last]
        yield
        q_g = [proj_ref[r, GQ:GQ + QK_W] for r in chunk_rows]
        k_g = [proj_ref[r, GK:GK + QK_W] for r in chunk_rows]
        v_g = [proj_ref[r, GV:GV + V_W] for r in chunk_rows]
        s_g = [scores(q, k) * lmask_ref[0] for q, k in zip(q_g, k_g)]
        for l, s in enumerate(GLA_LEVELS):
            for c in range(len(chunk_rows)):
                if c % 4 == 0:
                    yield
                f = jnp.exp2(-jnp.abs(b[c] - _midpoint_rows(b[c], s))).astype(BF16)
                s_g[c] = s_g[c] + scores(q_g[c] * f, k_g[c] * f) * lmask_ref[1 + l]
        yield
        qin_g = [q * jnp.exp2(bc).astype(BF16) for q, bc in zip(q_g, b)]
        kv_g = [state_update(k * jnp.exp2(bl[:CHUNK, :] - bc).astype(BF16), v)
                for k, bc, bl, v in zip(k_g, b, b_last, v_g)]

        sr, sg = state
        for c, r in enumerate(chunk_rows):
            if c % 2 == 0:
                yield
            o = attend(s_r[c], qin_r[c], v_r[c], sr)
            sr = sr * aret_ref[...] + kv_r[c]
            gate = proj_ref[r, RG:RG + V_W].astype(F32)
            o_ref[r, 0:V_W] = _head_norm_gate(o, rn_ref[...], gate).astype(BF16)
            o = attend(s_g[c], qin_g[c], v_g[c], sg)
            sg = sg * a_col[c] + kv_g[c]
            gate = proj_ref[r, GG:GG + V_W].astype(F32)
            o_ref[r, V_W:2 * V_W] = _head_norm_gate(o, gn_ref[...], gate).astype(BF16)
        state[:] = [sr, sg]

    def finish_previous(rows):
        h = h2_ref[rows, :]
        for c in range(D_FF // MXU_N):
            cols = slice(c * MXU_N, (c + 1) * MXU_N)
            gate = _dot(h, wg_ref[:, cols])
            up = _dot(h, wu_ref[:, cols])
            act_ref[rows, cols] = (_silu(gate) * up).astype(BF16)
            yield
        y = _dot(act_ref[rows, :], wd_ref[...])
        out_ref[rows, :] = _rms(x2_ref[rows, :] + 0.5 * y, gf_ref[...])

    n_rows = FFN_STREAM_ROWS
    mixer = mixer_stages()
    ffn_streams = [finish_previous(pl.ds(i * n_rows, n_rows)) for i in range(rows_per_step // n_rows)]
    for n_mixer in MIXER_STAGES_PER_ROUND:
        for g in ffn_streams:
            next(g, None)
        for _ in range(n_mixer):
            next(mixer, None)
    for g in ffn_streams + [mixer]:
        for _ in g:
            pass
    for i in range(rows_per_step // n_rows):
        rows = pl.ds(i * n_rows, n_rows)
        x2 = x1_ref[rows, :] + _dot(o_ref[rows, :], wout_ref[...])
        x2_ref[rows, :] = x2
        h2_ref[rows, :] = _rms(x2, g2_ref[...]).astype(BF16)


def _gla_tables():
    t = np.arange(CHUNK)[:, None]
    u = np.arange(CHUNK)[None, :]
    masks = [np.eye(CHUNK, dtype=bool)]
    for s in GLA_LEVELS:
        same_block = (t // s) == (u // s)
        masks.append(same_block & (((t % s) < s // 2) != ((u % s) < s // 2)))
    ltri = (u <= t).astype(np.float32)
    ltri2 = np.concatenate([ltri, ltri], axis=1)
    lmask = np.stack([np.tile(m, (1, N_HEADS)) for m in masks]).astype(np.float32)
    return ltri2, lmask


def _retention_tables():
    gamma = 1.0 - 2.0 ** (-5.0 - np.arange(N_HEADS, dtype=np.float64))
    head_of_lane = np.arange(QK_W) // HEAD_QK
    i = np.arange(CHUNK)
    dist = np.abs(i[:, None] - i[None, :])
    dret = np.concatenate([gamma[h] ** dist for h in range(N_HEADS)], axis=1)
    gq = gamma[head_of_lane][None, :] ** (i[:, None] + 1.0)
    gk = gamma[head_of_lane][None, :] ** (CHUNK - 1.0 - i[:, None])
    aret = np.broadcast_to((gamma[head_of_lane] ** CHUNK)[:, None], (QK_W, HEAD_V))
    return tuple(np.asarray(a, np.float32) for a in (dret, gq, gk, aret))


def _rotary_tables(seq):
    half = HEAD_QK // 2
    inv = ROPE_BASE ** (-np.arange(half, dtype=np.float64) * 2.0 / HEAD_QK)
    ang = np.arange(seq, dtype=np.float64)[:, None] * inv[None, :]
    sign = np.where((np.arange(LANE) % HEAD_QK) < half, -1.0, 1.0)
    cos = np.tile(np.cos(ang), (1, LANE // half))
    sin = np.tile(np.sin(ang), (1, LANE // half)) * sign[None, :]
    return cos.astype(np.float32), sin.astype(np.float32)


def _key_mask():
    idx = np.arange(LANE) // HEAD_QK
    return idx[:, None] == idx[None, :]


def _const_spec(shape):
    nd = len(shape)
    return pl.BlockSpec(shape, lambda *_: (0,) * nd, pipeline_mode=pl.Buffered(1))


def _slab_spec(shape, n_steps):
    rows, cols = shape
    slab = next(s for s in range(BF16_SUBLANES, rows + 1, BF16_SUBLANES)
                if rows % s == 0 and n_steps % (rows // s) == 0 and rows // s <= n_steps)
    repeat = n_steps // (rows // slab)
    return pl.BlockSpec((slab, cols), lambda i: (i // repeat, 0))


def _tile_rows(total_rows):
    tile = 512
    assert total_rows % tile == 0
    return tile


@jax.jit
def kernel(x, ffn1_norm_g, ffn1_w_gate, ffn1_w_up, ffn1_w_down, mix_norm_g, w_in, ret_norm_g, gla_w_a2,
           gla_b_a, gla_norm_g, w_out, ffn2_norm_g, ffn2_w_gate, ffn2_w_up, ffn2_w_down, final_norm_g):
    bsz, seq, d = x.shape
    assert d == D_MODEL and seq % CHUNK == 0 and ffn1_norm_g.shape[0] == 1
    tile = _tile_rows(seq)
    steps = seq // tile
    n_rows = bsz * seq

    row = lambda g: g.reshape(1, -1).astype(F32)
    w_in_t = jnp.swapaxes(w_in, 1, 2)[0]
    w_a2 = jnp.pad(gla_w_a2[0], ((0, LANE - GATE_RANK), (0, 0))).astype(BF16)

    cos, sin = _rotary_tables(seq)

    x2d = x.reshape(n_rows, d)
    n_tiles = n_rows // tile
    row_spec = lambda w: pl.BlockSpec((tile, w), lambda i: (i, 0))
    pos_spec = pl.BlockSpec((tile, LANE), lambda i: (i % steps, 0))
    hbm_spec = pl.BlockSpec(memory_space=pl.ANY)
    later = [w_out[0], ffn2_w_gate[0], ffn2_w_up[0], ffn2_w_down[0]]
    later_specs = [_slab_spec(w.shape, n_tiles) for w in later]
    x1, proj, la, w_out_b, wg2_b, wu2_b, wd2_b = pl.pallas_call(
        _ffn1_proj_kernel,
        grid=(n_tiles,),
        in_specs=[row_spec(d), pos_spec, pos_spec, _const_spec((1, d)),
                  hbm_spec, hbm_spec, hbm_spec, _const_spec((1, d)), hbm_spec,
                  _const_spec((LANE, QK_W)), _const_spec((1, QK_W))] + later_specs,
        out_specs=[row_spec(d), row_spec(PROJ_W), row_spec(QK_W)] + later_specs,
        out_shape=[jax.ShapeDtypeStruct((n_rows, d), F32),
                   jax.ShapeDtypeStruct((n_rows, PROJ_W), BF16),
                   jax.ShapeDtypeStruct((n_rows, QK_W), F32)]
                  + [jax.ShapeDtypeStruct(w.shape, BF16) for w in later],
        scratch_shapes=[pltpu.VMEM((tile, D_FF), BF16),
                        pltpu.VMEM((d, D_FF), BF16), pltpu.VMEM((d, D_FF), BF16), pltpu.VMEM((D_FF, d), BF16),
                        pltpu.VMEM((PROJ_W, d), BF16), pltpu.VMEM((LANE, d), BF16)],
        compiler_params=pltpu.CompilerParams(dimension_semantics=("arbitrary",),
                                             vmem_limit_bytes=VMEM_LIMIT),
        name="ffn1_proj",
    )(x2d, cos, sin, row(ffn1_norm_g), ffn1_w_gate[0], ffn1_w_up[0], ffn1_w_down[0], row(mix_norm_g),
      w_in_t, w_a2, row(gla_b_a), *later)

    ltri2, lmask = _gla_tables()
    dret, gq, gk, aret = _retention_tables()
    consts = [jnp.asarray(_key_mask(), BF16), jnp.asarray(dret), jnp.asarray(gq),
              jnp.asarray(gk), jnp.asarray(aret), jnp.asarray(ltri2, BF16), jnp.asarray(lmask)]

    tile = MIXER_TILE_ROWS
    assert seq % tile == 0
    n_tiles = n_rows // tile
    tok_spec = lambda w: pl.BlockSpec((tile, w), lambda i: (jnp.minimum(i, n_tiles - 1), 0))
    out = pl.pallas_call(
        functools.partial(_mixer_ffn2_kernel, steps_per_seq=seq // tile),
        grid=(n_tiles + 1,),
        in_specs=[tok_spec(d), tok_spec(PROJ_W), tok_spec(QK_W)]
                 + [_const_spec(c.shape) for c in consts]
                 + [_const_spec((1, V_W)), _const_spec((1, V_W)), _const_spec((d, d)), _const_spec((1, d)),
                    _const_spec((d, D_FF)), _const_spec((d, D_FF)), _const_spec((D_FF, d)),
                    _const_spec((1, d))],
        out_specs=pl.BlockSpec((tile, d), lambda i: (jnp.maximum(i - 1, 0), 0)),
        out_shape=jax.ShapeDtypeStruct((n_rows, d), F32),
        scratch_shapes=[pltpu.VMEM((tile, 2 * V_W), BF16), pltpu.VMEM((tile, D_FF), BF16),
                        pltpu.VMEM((tile, d), F32), pltpu.VMEM((tile, d), BF16),
                        pltpu.VMEM((QK_W, HEAD_V), F32), pltpu.VMEM((QK_W, HEAD_V), F32)],
        compiler_params=pltpu.CompilerParams(dimension_semantics=("arbitrary",),
                                             vmem_limit_bytes=VMEM_LIMIT),
        name="mixer_ffn2",
    )(x1, proj, la, *consts, row(ret_norm_g), row(gla_norm_g), w_out_b, row(ffn2_norm_g),
      wg2_b, wu2_b, wd2_b, row(final_norm_g))
    return out.reshape(bsz, seq, d)
```

```python
import functools

import numpy as np
import jax
import jax.numpy as jnp
from jax import lax
from jax.experimental import pallas as pl
from jax.experimental.pallas import tpu as pltpu

D_MODEL = 1024
CHUNK = 64
RMS_EPS = 1e-6
ROPE_BASE = 10000.0
N_HEADS = 4
QK_W = 256
V_W = 512
HEAD_V = V_W // N_HEADS
HEAD_QK = QK_W // N_HEADS
GATE_RANK = 16
GATE_NORM = 16.0
D_FF = 2816
PROJ_W = 2 * (2 * QK_W + 2 * V_W)
RQ, RK, RV, RG = 0, 256, 512, 1024
GQ, GK, GV, GG = 1536, 1792, 2048, 2560
GLOW = 3072

LANE = 128
SUBLANES = 8
BF16_SUBLANES = 16
MXU_N = 256
FFN_STREAM_ROWS = 512
MIXER_TILE_ROWS = 512
WEIGHT_CHUNK_BYTES = 3 * 512 * 1024
WEIGHT_STAGE_SLOTS = 6
VMEM_LIMIT = 56 * 1024 * 1024

GLA_LEVELS = (64, 32, 16, 8, 4, 2)
MIXER_GROUP_CHUNKS = 4
MIXER_STAGES_PER_ROUND = (3, 2) * 5 + (2,)
LOG2_E = 1.4426950408889634

BF16 = jnp.bfloat16
F32 = jnp.float32


def _dot(a, b):
    return jnp.dot(a, b, preferred_element_type=F32)


def _dot_nt(a, b):
    return lax.dot_general(a, b, (((1,), (1,)), ((), ())), preferred_element_type=F32)


def _rms(x, g):
    ms = jnp.mean(x * x, axis=-1, keepdims=True)
    return x * lax.rsqrt(ms + RMS_EPS) * g


def _silu(x):
    h = 0.5 * x
    return h * jnp.tanh(h) + h


def _weight_chunk_rows(n_rows, n_cols):
    best = max(r for r in range(BF16_SUBLANES, n_rows + 1, BF16_SUBLANES)
               if n_rows % r == 0 and r * n_cols * 4 <= WEIGHT_CHUNK_BYTES)
    return best


def _load_weight_as_bf16(src_hbm, dst_ref):
    n_rows, n_cols = dst_ref.shape
    chunk_rows = _weight_chunk_rows(n_rows, n_cols)
    n_chunks = n_rows // chunk_rows
    n_slots = WEIGHT_STAGE_SLOTS

    def stream(stage_ref, sem):
        def chunk_copy(k):
            slot = k % n_slots
            return pltpu.make_async_copy(src_hbm.at[pl.ds(k * chunk_rows, chunk_rows), :],
                                         stage_ref.at[slot], sem.at[slot])

        assert n_chunks % 2 == 0
        for k in range(min(n_slots - 1, n_chunks)):
            chunk_copy(k).start(priority=k % 2)

        def body(pair, carry):
            for j in range(2):
                k = 2 * pair + j

                @pl.when(k + n_slots - 1 < n_chunks)
                def _():
                    chunk_copy(k + n_slots - 1).start(priority=(j + n_slots - 1) % 2)

                chunk_copy(k).wait()
                rows = pl.ds(pl.multiple_of(k * chunk_rows, chunk_rows), chunk_rows)
                dst_ref[rows, :] = stage_ref[k % n_slots].astype(BF16)
            return carry

        lax.fori_loop(0, n_chunks // 2, body, 0)

    pl.run_scoped(stream, pltpu.VMEM((n_slots, chunk_rows, n_cols), F32),
                  pltpu.SemaphoreType.DMA((n_slots,)))


def _ffn1_proj_kernel(x_ref, cos_ref, sin_ref, g1_ref, wg_hbm, wu_hbm, wd_hbm, gm_ref, win_hbm,
                      wa2_ref, ba_ref, *rest):
    later_f32, (x1_ref, proj_ref, la_ref), later_bf16 = rest[:4], rest[4:7], rest[7:11]
    act_ref, wg_ref, wu_ref, wd_ref, win_ref, wlow_ref = rest[11:]

    @pl.when(pl.program_id(0) == 0)
    def _():
        for src, dst in ((wg_hbm, wg_ref), (wu_hbm, wu_ref), (wd_hbm, wd_ref), (win_hbm, win_ref)):
            _load_weight_as_bf16(src, dst)

        def load_low_rank(stage_ref, sem):
            tail = pltpu.make_async_copy(win_hbm.at[pl.ds(GLOW, GATE_RANK), :], stage_ref, sem.at[0])
            tail.start()
            wlow_ref[...] = jnp.zeros_like(wlow_ref)
            tail.wait()
            wlow_ref[0:GATE_RANK, :] = stage_ref[...].astype(BF16)

        pl.run_scoped(load_low_rank, pltpu.VMEM((GATE_RANK, wlow_ref.shape[1]), F32),
                      pltpu.SemaphoreType.DMA((1,)))

    def cast_later_weights():
        yield
        for src, dst in zip(later_f32, later_bf16):
            dst[...] = src[...].astype(BF16)

    def half_tile(rows):
        x = x_ref[rows, :]
        h = _rms(x, g1_ref[...]).astype(BF16)
        yield
        for c in range(D_FF // MXU_N):
            cols = slice(c * MXU_N, (c + 1) * MXU_N)
            gate = _dot(h, wg_ref[:, cols])
            up = _dot(h, wu_ref[:, cols])
            act_ref[rows, cols] = (_silu(gate) * up).astype(BF16)
        yield
        x1 = x + 0.5 * _dot(act_ref[rows, :], wd_ref[...])
        x1_ref[rows, :] = x1
        h = _rms(x1, gm_ref[...]).astype(BF16)
        yield
        cos = cos_ref[rows, :]
        sin = sin_ref[rows, :]
        first_half = (lax.broadcasted_iota(jnp.int32, cos.shape, 1) % HEAD_QK) < HEAD_QK // 2
        for off, scale in ((RQ, 1.0), (RK, HEAD_QK ** -0.5)):
            t = _dot_nt(h, win_ref[off:off + QK_W, :])
            for col in range(QK_W // LANE):
                tc = t[:, col * LANE:(col + 1) * LANE]
                partner = jnp.where(first_half, pltpu.roll(tc, LANE - HEAD_QK // 2, 1),
                                    pltpu.roll(tc, HEAD_QK // 2, 1))
                proj_ref[rows, off + col * LANE:off + (col + 1) * LANE] = (
                    (tc * cos + partner * sin) * scale).astype(BF16)
        t = _dot_nt(h, win_ref[RV:GQ, :])
        proj_ref[rows, RV:RG] = t[:, :RG - RV].astype(BF16)
        proj_ref[rows, RG:GQ] = _silu(t[:, RG - RV:]).astype(BF16)
        proj_ref[rows, GQ:GK] = (_dot_nt(h, win_ref[GQ:GK, :]) * (HEAD_QK ** -0.5)).astype(BF16)
        t = _dot_nt(h, win_ref[GK:PROJ_W, :])
        proj_ref[rows, GK:GG] = t[:, :GG - GK].astype(BF16)
        proj_ref[rows, GG:PROJ_W] = _silu(t[:, GG - GK:]).astype(BF16)
        low = _dot_nt(h, wlow_ref[...]).astype(BF16)
        yield
        z = _dot(low, wa2_ref[...]) + ba_ref[...]
        la_ref[rows, :] = (jnp.minimum(z, 0.0) - jnp.log(1.0 + jnp.exp(-jnp.abs(z)))) * (1.0 / GATE_NORM)

    n_half = x_ref.shape[0] // 2
    streams = [half_tile(pl.ds(i * n_half, n_half)) for i in range(2)] + [cast_later_weights()]
    while streams:
        streams = [g for g in streams if next(g, StopIteration) is not StopIteration]


def _head_norm_gate(o, norm_g, gate):
    outs = []
    for h in range(N_HEADS):
        cols = slice(h * HEAD_V, (h + 1) * HEAD_V)
        oh = o[:, cols]
        ms = jnp.mean(oh * oh, axis=-1, keepdims=True)
        outs.append(oh * lax.rsqrt(ms + RMS_EPS) * norm_g[:, cols] * gate[:, cols])
    return jnp.concatenate(outs, axis=-1)


def _midpoint_rows(b, s):
    rows, w = b.shape
    if s >= SUBLANES:
        blocks = b.reshape(rows // s, s, w)
        return jnp.broadcast_to(blocks[:, s // 2 - 1:s // 2, :], blocks.shape).reshape(rows, w)
    groups = b.reshape(rows // SUBLANES, SUBLANES, w)
    sub = lax.broadcasted_iota(jnp.int32, groups.shape, 1)
    out = None
    for first in reversed(range(0, SUBLANES, s)):
        ref_row = jnp.broadcast_to(groups[:, first + s // 2 - 1:first + s // 2, :], groups.shape)
        out = ref_row if out is None else jnp.where(sub < first + s, ref_row, out)
    return out.reshape(rows, w)


def _mixer_ffn2_kernel(x1_ref, proj_ref, la_ref,
                       kmask_ref, dret_ref, gq_ref, gk_ref, aret_ref, ltri_ref, lmask_ref,
                       rn_ref, gn_ref, wout_ref, g2_ref, wg_ref, wu_ref, wd_ref, gf_ref,
                       out_ref, o_ref, act_ref, x2_ref, h2_ref, sr_ref, sg_ref, *, steps_per_seq):
    rows_per_step = x1_ref.shape[0]
    step = pl.program_id(0)

    @pl.when(step == 0)
    def _():
        x2_ref[...] = jnp.zeros_like(x2_ref)
        h2_ref[...] = jnp.zeros_like(h2_ref)

    @pl.when(step % steps_per_seq == 0)
    def _():
        sr_ref[...] = jnp.zeros_like(sr_ref)
        sg_ref[...] = jnp.zeros_like(sg_ref)

    def scores(ql, kl):
        out = []
        for col in range(QK_W // LANE):
            lanes = slice(col * LANE, (col + 1) * LANE)
            w = jnp.concatenate([kl[:, lanes]] * 2, axis=0).T * kmask_ref[...]
            out.append(_dot(ql[:, lanes], w))
        return jnp.concatenate(out, axis=1)

    def attend(s, q_in, v, state):
        s = s.astype(BF16)
        zero = jnp.zeros((CHUNK, HEAD_V), BF16)
        out = []
        for col in range(QK_W // LANE):
            lanes = slice(col * LANE, (col + 1) * LANE)
            rhs = []
            for h in (2 * col, 2 * col + 1):
                vh = v[:, h * HEAD_V:(h + 1) * HEAD_V]
                rhs.append([vh, zero] if h % 2 == 0 else [zero, vh])
            for h in (2 * col, 2 * col + 1):
                sh = state[h * HEAD_QK:(h + 1) * HEAD_QK, :].astype(BF16)
                rhs.append([sh, zero] if h % 2 == 0 else [zero, sh])
            rhs = jnp.concatenate([jnp.concatenate(r, axis=1) for r in rhs], axis=0)
            out.append(_dot(jnp.concatenate([s[:, lanes], q_in[:, lanes]], axis=1), rhs))
        return jnp.concatenate(out, axis=1)

    def state_update(k_out, v):
        k_t = k_out.T
        return jnp.concatenate([_dot(k_t[h * HEAD_QK:(h + 1) * HEAD_QK, :], v[:, h * HEAD_V:(h + 1) * HEAD_V])
                                for h in range(N_HEADS)], axis=0)

    def mixer_stages():
        state = [sr_ref[...], sg_ref[...]]
        for first in range(0, rows_per_step // CHUNK, MIXER_GROUP_CHUNKS):
            yield from chunk_group_stages([pl.ds((first + c) * CHUNK, CHUNK)
                                           for c in range(MIXER_GROUP_CHUNKS)], state)
        sr_ref[...], sg_ref[...] = state

    def chunk_group_stages(chunk_rows, state):
        q_r = [proj_ref[r, RQ:RQ + QK_W] for r in chunk_rows]
        k_r = [proj_ref[r, RK:RK + QK_W] for r in chunk_rows]
        v_r = [proj_ref[r, RV:RV + V_W] for r in chunk_rows]
        s_r = [scores(q, k) * dret_ref[...] for q, k in zip(q_r, k_r)]
        qin_r = [(q.astype(F32) * gq_ref[...]).astype(BF16) for q in q_r]
        yield
        kv_r = [state_update((k.astype(F32) * gk_ref[...]).astype(BF16), v) for k, v in zip(k_r, v_r)]
        yield

        b = []
        for r in chunk_rows:
            la = la_ref[r, :]
            la_hi = la.astype(BF16)
            la_lo = (la - la_hi.astype(F32)).astype(BF16)
            b.append(_dot(ltri_ref[...], jnp.concatenate([la_hi, la_lo], axis=0)) * LOG2_E)
        b_last = [jnp.broadcast_to(bc[CHUNK - 1:CHUNK, :], (2 * CHUNK, QK_W)) for bc in b]
        a_col = [jnp.exp2(bl.T)[:, :LANE] for bl in b_last]
        yield
        q_g = [proj_ref[r, GQ:GQ + QK_W] for r in chunk_rows]
        k_g = [proj_ref[r, GK:GK + QK_W] for r in chunk_rows]
        v_g = [proj_ref[r, GV:GV + V_W] for r in chunk_rows]
        s_g = [scores(q, k) * lmask_ref[0] for q, k in zip(q_g, k_g)]
        for l, s in enumerate(GLA_LEVELS):
            for c in range(len(chunk_rows)):
                if c % 4 == 0:
                    yield
                f = jnp.exp2(-jnp.abs(b[c] - _midpoint_rows(b[c], s))).astype(BF16)
                s_g[c] = s_g[c] + scores(q_g[c] * f, k_g[c] * f) * lmask_ref[1 + l]
        yield
        qin_g = [q * jnp.exp2(bc).astype(BF16) for q, bc in zip(q_g, b)]
        kv_g = [state_update(k * jnp.exp2(bl[:CHUNK, :] - bc).astype(BF16), v)
                for k, bc, bl, v in zip(k_g, b, b_last, v_g)]

        sr, sg = state
        for c, r in enumerate(chunk_rows):
            if c % 2 == 0:
                yield
            o = attend(s_r[c], qin_r[c], v_r[c], sr)
            sr = sr * aret_ref[...] + kv_r[c]
            gate = proj_ref[r, RG:RG + V_W].astype(F32)
            o_ref[r, 0:V_W] = _head_norm_gate(o, rn_ref[...], gate).astype(BF16)
            o = attend(s_g[c], qin_g[c], v_g[c], sg)
            sg = sg * a_col[c] + kv_g[c]
            gate = proj_ref[r, GG:GG + V_W].astype(F32)
            o_ref[r, V_W:2 * V_W] = _head_norm_gate(o, gn_ref[...], gate).astype(BF16)
        state[:] = [sr, sg]

    def finish_previous(rows):
        h = h2_ref[rows, :]
        for c in range(D_FF // MXU_N):
            cols = slice(c * MXU_N, (c + 1) * MXU_N)
            gate = _dot(h, wg_ref[:, cols])
            up = _dot(h, wu_ref[:, cols])
            act_ref[rows, cols] = (_silu(gate) * up).astype(BF16)
            yield
        y = _dot(act_ref[rows, :], wd_ref[...])
        out_ref[rows, :] = _rms(x2_ref[rows, :] + 0.5 * y, gf_ref[...])

    n_rows = FFN_STREAM_ROWS
    mixer = mixer_stages()
    ffn_streams = [finish_previous(pl.ds(i * n_rows, n_rows)) for i in range(rows_per_step // n_rows)]
    for n_mixer in MIXER_STAGES_PER_ROUND:
        for g in ffn_streams:
            next(g, None)
        for _ in range(n_mixer):
            next(mixer, None)
    for g in ffn_streams + [mixer]:
        for _ in g:
            pass
    for i in range(rows_per_step // n_rows):
        rows = pl.ds(i * n_rows, n_rows)
        x2 = x1_ref[rows, :] + _dot(o_ref[rows, :], wout_ref[...])
        x2_ref[rows, :] = x2
        h2_ref[rows, :] = _rms(x2, g2_ref[...]).astype(BF16)


def _gla_tables():
    t = np.arange(CHUNK)[:, None]
    u = np.arange(CHUNK)[None, :]
    masks = [np.eye(CHUNK, dtype=bool)]
    for s in GLA_LEVELS:
        same_block = (t // s) == (u // s)
        masks.append(same_block & (((t % s) < s // 2) != ((u % s) < s // 2)))
    ltri = (u <= t).astype(np.float32)
    ltri2 = np.concatenate([ltri, ltri], axis=1)
    lmask = np.stack([np.tile(m, (1, N_HEADS)) for m in masks]).astype(np.float32)
    return ltri2, lmask


def _retention_tables():
    gamma = 1.0 - 2.0 ** (-5.0 - np.arange(N_HEADS, dtype=np.float64))
    head_of_lane = np.arange(QK_W) // HEAD_QK
    i = np.arange(CHUNK)
    dist = np.abs(i[:, None] - i[None, :])
    dret = np.concatenate([gamma[h] ** dist for h in range(N_HEADS)], axis=1)
    gq = gamma[head_of_lane][None, :] ** (i[:, None] + 1.0)
    gk = gamma[head_of_lane][None, :] ** (CHUNK - 1.0 - i[:, None])
    aret = np.broadcast_to((gamma[head_of_lane] ** CHUNK)[:, None], (QK_W, HEAD_V))
    return tuple(np.asarray(a, np.float32) for a in (dret, gq, gk, aret))


def _rotary_tables(seq):
    half = HEAD_QK // 2
    inv = ROPE_BASE ** (-np.arange(half, dtype=np.float64) * 2.0 / HEAD_QK)
    ang = np.arange(seq, dtype=np.float64)[:, None] * inv[None, :]
    sign = np.where((np.arange(LANE) % HEAD_QK) < half, -1.0, 1.0)
    cos = np.tile(np.cos(ang), (1, LANE // half))
    sin = np.tile(np.sin(ang), (1, LANE // half)) * sign[None, :]
    return cos.astype(np.float32), sin.astype(np.float32)


def _key_mask():
    idx = np.arange(LANE) // HEAD_QK
    return idx[:, None] == idx[None, :]


def _const_spec(shape):
    nd = len(shape)
    return pl.BlockSpec(shape, lambda *_: (0,) * nd, pipeline_mode=pl.Buffered(1))


def _slab_spec(shape, n_steps):
    rows, cols = shape
    slab = next(s for s in range(BF16_SUBLANES, rows + 1, BF16_SUBLANES)
                if rows % s == 0 and n_steps % (rows // s) == 0 and rows // s <= n_steps)
    repeat = n_steps // (rows // slab)
    return pl.BlockSpec((slab, cols), lambda i: (i // repeat, 0))


def _tile_rows(total_rows):
    tile = 512
    assert total_rows % tile == 0
    return tile


@jax.jit
def kernel(x, ffn1_norm_g, ffn1_w_gate, ffn1_w_up, ffn1_w_down, mix_norm_g, w_in, ret_norm_g, gla_w_a2,
           gla_b_a, gla_norm_g, w_out, ffn2_norm_g, ffn2_w_gate, ffn2_w_up, ffn2_w_down, final_norm_g):
    bsz, seq, d = x.shape
    assert d == D_MODEL and seq % CHUNK == 0 and ffn1_norm_g.shape[0] == 1
    tile = _tile_rows(seq)
    steps = seq // tile
    n_rows = bsz * seq

    row = lambda g: g.reshape(1, -1).astype(F32)
    w_in_t = jnp.swapaxes(w_in, 1, 2)[0]
    w_a2 = jnp.pad(gla_w_a2[0], ((0, LANE - GATE_RANK), (0, 0))).astype(BF16)

    cos, sin = _rotary_tables(seq)

    x2d = x.reshape(n_rows, d)
    n_tiles = n_rows // tile
    row_spec = lambda w: pl.BlockSpec((tile, w), lambda i: (i, 0))
    pos_spec = pl.BlockSpec((tile, LANE), lambda i: (i % steps, 0))
    hbm_spec = pl.BlockSpec(memory_space=pl.ANY)
    later = [w_out[0], ffn2_w_gate[0], ffn2_w_up[0], ffn2_w_down[0]]
    later_specs = [_slab_spec(w.shape, n_tiles) for w in later]
    x1, proj, la, w_out_b, wg2_b, wu2_b, wd2_b = pl.pallas_call(
        _ffn1_proj_kernel,
        grid=(n_tiles,),
        in_specs=[row_spec(d), pos_spec, pos_spec, _const_spec((1, d)),
                  hbm_spec, hbm_spec, hbm_spec, _const_spec((1, d)), hbm_spec,
                  _const_spec((LANE, QK_W)), _const_spec((1, QK_W))] + later_specs,
        out_specs=[row_spec(d), row_spec(PROJ_W), row_spec(QK_W)] + later_specs,
        out_shape=[jax.ShapeDtypeStruct((n_rows, d), F32),
                   jax.ShapeDtypeStruct((n_rows, PROJ_W), BF16),
                   jax.ShapeDtypeStruct((n_rows, QK_W), F32)]
                  + [jax.ShapeDtypeStruct(w.shape, BF16) for w in later],
        scratch_shapes=[pltpu.VMEM((tile, D_FF), BF16),
                        pltpu.VMEM((d, D_FF), BF16), pltpu.VMEM((d, D_FF), BF16), pltpu.VMEM((D_FF, d), BF16),
                        pltpu.VMEM((PROJ_W, d), BF16), pltpu.VMEM((LANE, d), BF16)],
        compiler_params=pltpu.CompilerParams(dimension_semantics=("arbitrary",),
                                             vmem_limit_bytes=VMEM_LIMIT),
        name="ffn1_proj",
    )(x2d, cos, sin, row(ffn1_norm_g), ffn1_w_gate[0], ffn1_w_up[0], ffn1_w_down[0], row(mix_norm_g),
      w_in_t, w_a2, row(gla_b_a), *later)

    ltri2, lmask = _gla_tables()
    dret, gq, gk, aret = _retention_tables()
    consts = [jnp.asarray(_key_mask(), BF16), jnp.asarray(dret), jnp.asarray(gq),
              jnp.asarray(gk), jnp.asarray(aret), jnp.asarray(ltri2, BF16), jnp.asarray(lmask)]

    tile = MIXER_TILE_ROWS
    assert seq % tile == 0
    n_tiles = n_rows // tile
    tok_spec = lambda w: pl.BlockSpec((tile, w), lambda i: (jnp.minimum(i, n_tiles - 1), 0))
    out = pl.pallas_call(
        functools.partial(_mixer_ffn2_kernel, steps_per_seq=seq // tile),
        grid=(n_tiles + 1,),
        in_specs=[tok_spec(d), tok_spec(PROJ_W), tok_spec(QK_W)]
                 + [_const_spec(c.shape) for c in consts]
                 + [_const_spec((1, V_W)), _const_spec((1, V_W)), _const_spec((d, d)), _const_spec((1, d)),
                    _const_spec((d, D_FF)), _const_spec((d, D_FF)), _const_spec((D_FF, d)),
                    _const_spec((1, d))],
        out_specs=pl.BlockSpec((tile, d), lambda i: (jnp.maximum(i - 1, 0), 0)),
        out_shape=jax.ShapeDtypeStruct((n_rows, d), F32),
        scratch_shapes=[pltpu.VMEM((tile, 2 * V_W), BF16), pltpu.VMEM((tile, D_FF), BF16),
                        pltpu.VMEM((tile, d), F32), pltpu.VMEM((tile, d), BF16),
                        pltpu.VMEM((QK_W, HEAD_V), F32), pltpu.VMEM((QK_W, HEAD_V), F32)],
        compiler_params=pltpu.CompilerParams(dimension_semantics=("arbitrary",),
                                             vmem_limit_bytes=VMEM_LIMIT),
        name="mixer_ffn2",
    )(x1, proj, la, *consts, row(ret_norm_g), row(gla_norm_g), w_out_b, row(ffn2_norm_g),
      wg2_b, wu2_b, wd2_b, row(final_norm_g))
    return out.reshape(bsz, seq, d)
```

```python
import functools

import numpy as np
import jax
import jax.numpy as jnp
from jax import lax
from jax.experimental import pallas as pl
from jax.experimental.pallas import tpu as pltpu

D_MODEL = 1024
CHUNK = 64
RMS_EPS = 1e-6
ROPE_BASE = 10000.0
N_HEADS = 4
QK_W = 256
V_W = 512
HEAD_V = V_W // N_HEADS
HEAD_QK = QK_W // N_HEADS
GATE_RANK = 16
GATE_NORM = 16.0
D_FF = 2816
PROJ_W = 2 * (2 * QK_W + 2 * V_W)
RQ, RK, RV, RG = 0, 256, 512, 1024
GQ, GK, GV, GG = 1536, 1792, 2048, 2560
GLOW = 3072

LANE = 128
SUBLANES = 8
BF16_SUBLANES = 16
MXU_N = 256
FFN_STREAM_ROWS = 512
MIXER_TILE_ROWS = 512
WEIGHT_CHUNK_BYTES = 3 * 512 * 1024
WEIGHT_STAGE_SLOTS = 6
VMEM_LIMIT = 56 * 1024 * 1024

GLA_LEVELS = (64, 32, 16, 8, 4, 2)
MIXER_GROUP_CHUNKS = 4
MIXER_STAGES_PER_ROUND = (27,) + (0,) * 10
LOG2_E = 1.4426950408889634

BF16 = jnp.bfloat16
F32 = jnp.float32


def _dot(a, b):
    return jnp.dot(a, b, preferred_element_type=F32)


def _dot_nt(a, b):
    return lax.dot_general(a, b, (((1,), (1,)), ((), ())), preferred_element_type=F32)


def _rms(x, g):
    ms = jnp.mean(x * x, axis=-1, keepdims=True)
    return x * lax.rsqrt(ms + RMS_EPS) * g


def _silu(x):
    h = 0.5 * x
    return h * jnp.tanh(h) + h


def _weight_chunk_rows(n_rows, n_cols):
    best = max(r for r in range(BF16_SUBLANES, n_rows + 1, BF16_SUBLANES)
               if n_rows % r == 0 and r * n_cols * 4 <= WEIGHT_CHUNK_BYTES)
    return best


def _load_weight_as_bf16(src_hbm, dst_ref):
    n_rows, n_cols = dst_ref.shape
    chunk_rows = _weight_chunk_rows(n_rows, n_cols)
    n_chunks = n_rows // chunk_rows
    n_slots = WEIGHT_STAGE_SLOTS

    def stream(stage_ref, sem):
        def chunk_copy(k):
            slot = k % n_slots
            return pltpu.make_async_copy(src_hbm.at[pl.ds(k * chunk_rows, chunk_rows), :],
                                         stage_ref.at[slot], sem.at[slot])

        for k in range(min(n_slots - 1, n_chunks)):
            chunk_copy(k).start()

        def body(k, carry):
            @pl.when(k + n_slots - 1 < n_chunks)
            def _():
                chunk_copy(k + n_slots - 1).start()

            chunk_copy(k).wait()
            rows = pl.ds(pl.multiple_of(k * chunk_rows, chunk_rows), chunk_rows)
            dst_ref[rows, :] = stage_ref[k % n_slots].astype(BF16)
            return carry

        lax.fori_loop(0, n_chunks, body, 0)

    pl.run_scoped(stream, pltpu.VMEM((n_slots, chunk_rows, n_cols), F32),
                  pltpu.SemaphoreType.DMA((n_slots,)))


def _ffn1_proj_kernel(x_ref, cos_ref, sin_ref, g1_ref, wg_hbm, wu_hbm, wd_hbm, gm_ref, win_hbm,
                      wa2_ref, ba_ref, *rest):
    later_f32, (x1_ref, proj_ref, la_ref), later_bf16 = rest[:4], rest[4:7], rest[7:11]
    act_ref, wg_ref, wu_ref, wd_ref, win_ref, wlow_ref = rest[11:]

    @pl.when(pl.program_id(0) == 0)
    def _():
        for src, dst in ((wg_hbm, wg_ref), (wu_hbm, wu_ref), (wd_hbm, wd_ref), (win_hbm, win_ref)):
            _load_weight_as_bf16(src, dst)

        def load_low_rank(stage_ref, sem):
            tail = pltpu.make_async_copy(win_hbm.at[pl.ds(GLOW, GATE_RANK), :], stage_ref, sem.at[0])
            tail.start()
            wlow_ref[...] = jnp.zeros_like(wlow_ref)
            tail.wait()
            wlow_ref[0:GATE_RANK, :] = stage_ref[...].astype(BF16)

        pl.run_scoped(load_low_rank, pltpu.VMEM((GATE_RANK, wlow_ref.shape[1]), F32),
                      pltpu.SemaphoreType.DMA((1,)))

    def cast_later_weights():
        yield
        for src, dst in zip(later_f32, later_bf16):
            dst[...] = src[...].astype(BF16)

    def half_tile(rows):
        x = x_ref[rows, :]
        h = _rms(x, g1_ref[...]).astype(BF16)
        yield
        for c in range(D_FF // MXU_N):
            cols = slice(c * MXU_N, (c + 1) * MXU_N)
            gate = _dot(h, wg_ref[:, cols])
            up = _dot(h, wu_ref[:, cols])
            act_ref[rows, cols] = (_silu(gate) * up).astype(BF16)
        yield
        x1 = x + 0.5 * _dot(act_ref[rows, :], wd_ref[...])
        x1_ref[rows, :] = x1
        h = _rms(x1, gm_ref[...]).astype(BF16)
        yield
        cos = cos_ref[rows, :]
        sin = sin_ref[rows, :]
        first_half = (lax.broadcasted_iota(jnp.int32, cos.shape, 1) % HEAD_QK) < HEAD_QK // 2
        for off, scale in ((RQ, 1.0), (RK, HEAD_QK ** -0.5)):
            t = _dot_nt(h, win_ref[off:off + QK_W, :])
            for col in range(QK_W // LANE):
                tc = t[:, col * LANE:(col + 1) * LANE]
                partner = jnp.where(first_half, pltpu.roll(tc, LANE - HEAD_QK // 2, 1),
                                    pltpu.roll(tc, HEAD_QK // 2, 1))
                proj_ref[rows, off + col * LANE:off + (col + 1) * LANE] = (
                    (tc * cos + partner * sin) * scale).astype(BF16)
        t = _dot_nt(h, win_ref[RV:GQ, :])
        proj_ref[rows, RV:RG] = t[:, :RG - RV].astype(BF16)
        proj_ref[rows, RG:GQ] = _silu(t[:, RG - RV:]).astype(BF16)
        proj_ref[rows, GQ:GK] = (_dot_nt(h, win_ref[GQ:GK, :]) * (HEAD_QK ** -0.5)).astype(BF16)
        t = _dot_nt(h, win_ref[GK:PROJ_W, :])
        proj_ref[rows, GK:GG] = t[:, :GG - GK].astype(BF16)
        proj_ref[rows, GG:PROJ_W] = _silu(t[:, GG - GK:]).astype(BF16)
        low = _dot_nt(h, wlow_ref[...]).astype(BF16)
        yield
        z = _dot(low, wa2_ref[...]) + ba_ref[...]
        la_ref[rows, :] = (jnp.minimum(z, 0.0) - jnp.log(1.0 + jnp.exp(-jnp.abs(z)))) * (1.0 / GATE_NORM)

    n_half = x_ref.shape[0] // 2
    streams = [half_tile(pl.ds(i * n_half, n_half)) for i in range(2)] + [cast_later_weights()]
    while streams:
        streams = [g for g in streams if next(g, StopIteration) is not StopIteration]


def _head_norm_gate(o, norm_g, gate):
    outs = []
    for h in range(N_HEADS):
        cols = slice(h * HEAD_V, (h + 1) * HEAD_V)
        oh = o[:, cols]
        ms = jnp.mean(oh * oh, axis=-1, keepdims=True)
        outs.append(oh * lax.rsqrt(ms + RMS_EPS) * norm_g[:, cols] * gate[:, cols])
    return jnp.concatenate(outs, axis=-1)


def _midpoint_rows(b, s):
    rows, w = b.shape
    if s >= SUBLANES:
        blocks = b.reshape(rows // s, s, w)
        return jnp.broadcast_to(blocks[:, s // 2 - 1:s // 2, :], blocks.shape).reshape(rows, w)
    groups = b.reshape(rows // SUBLANES, SUBLANES, w)
    sub = lax.broadcasted_iota(jnp.int32, groups.shape, 1)
    out = None
    for first in reversed(range(0, SUBLANES, s)):
        ref_row = jnp.broadcast_to(groups[:, first + s // 2 - 1:first + s // 2, :], groups.shape)
        out = ref_row if out is None else jnp.where(sub < first + s, ref_row, out)
    return out.reshape(rows, w)


def _mixer_ffn2_kernel(x1_ref, proj_ref, la_ref,
                       kmask_ref, dret_ref, gq_ref, gk_ref, aret_ref, ltri_ref, lmask_ref,
                       rn_ref, gn_ref, wout_ref, g2_ref, wg_ref, wu_ref, wd_ref, gf_ref,
                       out_ref, o_ref, act_ref, x2_ref, h2_ref, sr_ref, sg_ref, *, steps_per_seq):
    rows_per_step = x1_ref.shape[0]
    step = pl.program_id(0)

    @pl.when(step == 0)
    def _():
        x2_ref[...] = jnp.zeros_like(x2_ref)
        h2_ref[...] = jnp.zeros_like(h2_ref)

    @pl.when(step % steps_per_seq == 0)
    def _():
        sr_ref[...] = jnp.zeros_like(sr_ref)
        sg_ref[...] = jnp.zeros_like(sg_ref)

    def scores(ql, kl):
        out = []
        for col in range(QK_W // LANE):
            lanes = slice(col * LANE, (col + 1) * LANE)
            w = jnp.concatenate([kl[:, lanes]] * 2, axis=0).T * kmask_ref[...]
            out.append(_dot(ql[:, lanes], w))
        return jnp.concatenate(out, axis=1)

    def attend(s, q_in, v, state):
        s = s.astype(BF16)
        zero = jnp.zeros((CHUNK, HEAD_V), BF16)
        out = []
        for col in range(QK_W // LANE):
            lanes = slice(col * LANE, (col + 1) * LANE)
            rhs = []
            for h in (2 * col, 2 * col + 1):
                vh = v[:, h * HEAD_V:(h + 1) * HEAD_V]
                rhs.append([vh, zero] if h % 2 == 0 else [zero, vh])
            for h in (2 * col, 2 * col + 1):
                sh = state[h * HEAD_QK:(h + 1) * HEAD_QK, :].astype(BF16)
                rhs.append([sh, zero] if h % 2 == 0 else [zero, sh])
            rhs = jnp.concatenate([jnp.concatenate(r, axis=1) for r in rhs], axis=0)
            out.append(_dot(jnp.concatenate([s[:, lanes], q_in[:, lanes]], axis=1), rhs))
        return jnp.concatenate(out, axis=1)

    def state_update(k_out, v):
        k_t = k_out.T
        return jnp.concatenate([_dot(k_t[h * HEAD_QK:(h + 1) * HEAD_QK, :], v[:, h * HEAD_V:(h + 1) * HEAD_V])
                                for h in range(N_HEADS)], axis=0)

    def mixer_stages():
        state = [sr_ref[...], sg_ref[...]]
        for first in range(0, rows_per_step // CHUNK, MIXER_GROUP_CHUNKS):
            yield from chunk_group_stages([pl.ds((first + c) * CHUNK, CHUNK)
                                           for c in range(MIXER_GROUP_CHUNKS)], state)
        sr_ref[...], sg_ref[...] = state

    def chunk_group_stages(chunk_rows, state):
        q_r = [proj_ref[r, RQ:RQ + QK_W] for r in chunk_rows]
        k_r = [proj_ref[r, RK:RK + QK_W] for r in chunk_rows]
        v_r = [proj_ref[r, RV:RV + V_W] for r in chunk_rows]
        s_r = [scores(q, k) * dret_ref[...] for q, k in zip(q_r, k_r)]
        qin_r = [(q.astype(F32) * gq_ref[...]).astype(BF16) for q in q_r]
        yield
        kv_r = [state_update((k.astype(F32) * gk_ref[...]).astype(BF16), v) for k, v in zip(k_r, v_r)]
        yield

        b = []
        for r in chunk_rows:
            la = la_ref[r, :]
            la_hi = la.astype(BF16)
            la_lo = (la - la_hi.astype(F32)).astype(BF16)
            b.append(_dot(ltri_ref[...], jnp.concatenate([la_hi, la_lo], axis=0)) * LOG2_E)
        b_last = [jnp.broadcast_to(bc[CHUNK - 1:CHUNK, :], (2 * CHUNK, QK_W)) for bc in b]
        a_col = [jnp.exp2(bl.T)[:, :LANE] for bl in b_last]
        yield
        q_g = [proj_ref[r, GQ:GQ + QK_W] for r in chunk_rows]
        k_g = [proj_ref[r, GK:GK + QK_W] for r in chunk_rows]
        v_g = [proj_ref[r, GV:GV + V_W] for r in chunk_rows]
        s_g = [scores(q, k) * lmask_ref[0] for q, k in zip(q_g, k_g)]
        for l, s in enumerate(GLA_LEVELS):
            for c in range(len(chunk_rows)):
                if c % 4 == 0:
                    yield
                f = jnp.exp2(-jnp.abs(b[c] - _midpoint_rows(b[c], s))).astype(BF16)
                s_g[c] = s_g[c] + scores(q_g[c] * f, k_g[c] * f) * lmask_ref[1 + l]
        yield
        qin_g = [q * jnp.exp2(bc).astype(BF16) for q, bc in zip(q_g, b)]
        kv_g = [state_update(k * jnp.exp2(bl[:CHUNK, :] - bc).astype(BF16), v)
                for k, bc, bl, v in zip(k_g, b, b_last, v_g)]

        sr, sg = state
        for c, r in enumerate(chunk_rows):
            if c % 2 == 0:
                yield
            o = attend(s_r[c], qin_r[c], v_r[c], sr)
            sr = sr * aret_ref[...] + kv_r[c]
            gate = proj_ref[r, RG:RG + V_W].astype(F32)
            o_ref[r, 0:V_W] = _head_norm_gate(o, rn_ref[...], gate).astype(BF16)
            o = attend(s_g[c], qin_g[c], v_g[c], sg)
            sg = sg * a_col[c] + kv_g[c]
            gate = proj_ref[r, GG:GG + V_W].astype(F32)
            o_ref[r, V_W:2 * V_W] = _head_norm_gate(o, gn_ref[...], gate).astype(BF16)
        state[:] = [sr, sg]

    def finish_previous(rows):
        h = h2_ref[rows, :]
        for c in range(D_FF // MXU_N):
            cols = slice(c * MXU_N, (c + 1) * MXU_N)
            gate = _dot(h, wg_ref[:, cols])
            up = _dot(h, wu_ref[:, cols])
            act_ref[rows, cols] = (_silu(gate) * up).astype(BF16)
            yield
        y = _dot(act_ref[rows, :], wd_ref[...])
        out_ref[rows, :] = _rms(x2_ref[rows, :] + 0.5 * y, gf_ref[...])

    n_rows = FFN_STREAM_ROWS
    mixer = mixer_stages()
    ffn_streams = [finish_previous(pl.ds(i * n_rows, n_rows)) for i in range(rows_per_step // n_rows)]
    for n_mixer in MIXER_STAGES_PER_ROUND:
        for g in ffn_streams:
            next(g, None)
        for _ in range(n_mixer):
            next(mixer, None)
    for g in ffn_streams + [mixer]:
        for _ in g:
            pass
    for i in range(rows_per_step // n_rows):
        rows = pl.ds(i * n_rows, n_rows)
        x2 = x1_ref[rows, :] + _dot(o_ref[rows, :], wout_ref[...])
        x2_ref[rows, :] = x2
        h2_ref[rows, :] = _rms(x2, g2_ref[...]).astype(BF16)


def _gla_tables():
    t = np.arange(CHUNK)[:, None]
    u = np.arange(CHUNK)[None, :]
    masks = [np.eye(CHUNK, dtype=bool)]
    for s in GLA_LEVELS:
        same_block = (t // s) == (u // s)
        masks.append(same_block & (((t % s) < s // 2) != ((u % s) < s // 2)))
    ltri = (u <= t).astype(np.float32)
    ltri2 = np.concatenate([ltri, ltri], axis=1)
    lmask = np.stack([np.tile(m, (1, N_HEADS)) for m in masks]).astype(np.float32)
    return ltri2, lmask


def _retention_tables():
    gamma = 1.0 - 2.0 ** (-5.0 - np.arange(N_HEADS, dtype=np.float64))
    head_of_lane = np.arange(QK_W) // HEAD_QK
    i = np.arange(CHUNK)
    dist = np.abs(i[:, None] - i[None, :])
    dret = np.concatenate([gamma[h] ** dist for h in range(N_HEADS)], axis=1)
    gq = gamma[head_of_lane][None, :] ** (i[:, None] + 1.0)
    gk = gamma[head_of_lane][None, :] ** (CHUNK - 1.0 - i[:, None])
    aret = np.broadcast_to((gamma[head_of_lane] ** CHUNK)[:, None], (QK_W, HEAD_V))
    return tuple(np.asarray(a, np.float32) for a in (dret, gq, gk, aret))


def _rotary_tables(seq):
    half = HEAD_QK // 2
    inv = ROPE_BASE ** (-np.arange(half, dtype=np.float64) * 2.0 / HEAD_QK)
    ang = np.arange(seq, dtype=np.float64)[:, None] * inv[None, :]
    sign = np.where((np.arange(LANE) % HEAD_QK) < half, -1.0, 1.0)
    cos = np.tile(np.cos(ang), (1, LANE // half))
    sin = np.tile(np.sin(ang), (1, LANE // half)) * sign[None, :]
    return cos.astype(np.float32), sin.astype(np.float32)


def _key_mask():
    idx = np.arange(LANE) // HEAD_QK
    return idx[:, None] == idx[None, :]


def _const_spec(shape):
    nd = len(shape)
    return pl.BlockSpec(shape, lambda *_: (0,) * nd, pipeline_mode=pl.Buffered(1))


def _slab_spec(shape, n_steps):
    rows, cols = shape
    slab = next(s for s in range(BF16_SUBLANES, rows + 1, BF16_SUBLANES)
                if rows % s == 0 and n_steps % (rows // s) == 0 and rows // s <= n_steps)
    repeat = n_steps // (rows // slab)
    return pl.BlockSpec((slab, cols), lambda i: (i // repeat, 0))


def _tile_rows(total_rows):
    tile = 512
    assert total_rows % tile == 0
    return tile


@jax.jit
def kernel(x, ffn1_norm_g, ffn1_w_gate, ffn1_w_up, ffn1_w_down, mix_norm_g, w_in, ret_norm_g, gla_w_a2,
           gla_b_a, gla_norm_g, w_out, ffn2_norm_g, ffn2_w_gate, ffn2_w_up, ffn2_w_down, final_norm_g):
    bsz, seq, d = x.shape
    assert d == D_MODEL and seq % CHUNK == 0 and ffn1_norm_g.shape[0] == 1
    tile = _tile_rows(seq)
    steps = seq // tile
    n_rows = bsz * seq

    row = lambda g: g.reshape(1, -1).astype(F32)
    w_in_t = jnp.swapaxes(w_in, 1, 2)[0]
    w_a2 = jnp.pad(gla_w_a2[0], ((0, LANE - GATE_RANK), (0, 0))).astype(BF16)

    cos, sin = _rotary_tables(seq)

    x2d = x.reshape(n_rows, d)
    n_tiles = n_rows // tile
    row_spec = lambda w: pl.BlockSpec((tile, w), lambda i: (i, 0))
    pos_spec = pl.BlockSpec((tile, LANE), lambda i: (i % steps, 0))
    hbm_spec = pl.BlockSpec(memory_space=pl.ANY)
    later = [w_out[0], ffn2_w_gate[0], ffn2_w_up[0], ffn2_w_down[0]]
    later_specs = [_slab_spec(w.shape, n_tiles) for w in later]
    x1, proj, la, w_out_b, wg2_b, wu2_b, wd2_b = pl.pallas_call(
        _ffn1_proj_kernel,
        grid=(n_tiles,),
        in_specs=[row_spec(d), pos_spec, pos_spec, _const_spec((1, d)),
                  hbm_spec, hbm_spec, hbm_spec, _const_spec((1, d)), hbm_spec,
                  _const_spec((LANE, QK_W)), _const_spec((1, QK_W))] + later_specs,
        out_specs=[row_spec(d), row_spec(PROJ_W), row_spec(QK_W)] + later_specs,
        out_shape=[jax.ShapeDtypeStruct((n_rows, d), F32),
                   jax.ShapeDtypeStruct((n_rows, PROJ_W), BF16),
                   jax.ShapeDtypeStruct((n_rows, QK_W), F32)]
                  + [jax.ShapeDtypeStruct(w.shape, BF16) for w in later],
        scratch_shapes=[pltpu.VMEM((tile, D_FF), BF16),
                        pltpu.VMEM((d, D_FF), BF16), pltpu.VMEM((d, D_FF), BF16), pltpu.VMEM((D_FF, d), BF16),
                        pltpu.VMEM((PROJ_W, d), BF16), pltpu.VMEM((LANE, d), BF16)],
        compiler_params=pltpu.CompilerParams(dimension_semantics=("arbitrary",),
                                             vmem_limit_bytes=VMEM_LIMIT),
        name="ffn1_proj",
    )(x2d, cos, sin, row(ffn1_norm_g), ffn1_w_gate[0], ffn1_w_up[0], ffn1_w_down[0], row(mix_norm_g),
      w_in_t, w_a2, row(gla_b_a), *later)

    ltri2, lmask = _gla_tables()
    dret, gq, gk, aret = _retention_tables()
    consts = [jnp.asarray(_key_mask(), BF16), jnp.asarray(dret), jnp.asarray(gq),
              jnp.asarray(gk), jnp.asarray(aret), jnp.asarray(ltri2, BF16), jnp.asarray(lmask)]

    tile = MIXER_TILE_ROWS
    assert seq % tile == 0
    n_tiles = n_rows // tile
    tok_spec = lambda w: pl.BlockSpec((tile, w), lambda i: (jnp.minimum(i, n_tiles - 1), 0))
    out = pl.pallas_call(
        functools.partial(_mixer_ffn2_kernel, steps_per_seq=seq // tile),
        grid=(n_tiles + 1,),
        in_specs=[tok_spec(d), tok_spec(PROJ_W), tok_spec(QK_W)]
                 + [_const_spec(c.shape) for c in consts]
                 + [_const_spec((1, V_W)), _const_spec((1, V_W)), _const_spec((d, d)), _const_spec((1, d)),
                    _const_spec((d, D_FF)), _const_spec((d, D_FF)), _const_spec((D_FF, d)),
                    _const_spec((1, d))],
        out_specs=pl.BlockSpec((tile, d), lambda i: (jnp.maximum(i - 1, 0), 0)),
        out_shape=jax.ShapeDtypeStruct((n_rows, d), F32),
        scratch_shapes=[pltpu.VMEM((tile, 2 * V_W), BF16), pltpu.VMEM((tile, D_FF), BF16),
                        pltpu.VMEM((tile, d), F32), pltpu.VMEM((tile, d), BF16),
                        pltpu.VMEM((QK_W, HEAD_V), F32), pltpu.VMEM((QK_W, HEAD_V), F32)],
        compiler_params=pltpu.CompilerParams(dimension_semantics=("arbitrary",),
                                             vmem_limit_bytes=VMEM_LIMIT),
        name="mixer_ffn2",
    )(x1, proj, la, *consts, row(ret_norm_g), row(gla_norm_g), w_out_b, row(ffn2_norm_g),
      wg2_b, wu2_b, wd2_b, row(final_norm_g))
    return out.reshape(bsz, seq, d)
```
